```python
import math
import jax
import jax.numpy as jnp
from jax import lax
import numpy as np

D_MODEL = 1024
BATCH = 1
SEQ = 16384
DEPTH = 2

GRID_W = 64
CTX_LEN = 256
ROPE_THETA = 10000.0
Q_BLOCK = 128
NORM_EPS = 1e-6
N_MOD = 6

MLA_HEADS = 8
MLA_NOPE = 64
MLA_ROPE = 32
MLA_V = 64
MLA_Q_RANK = 256
MLA_KV_RANK = 128
MLA_W = MLA_HEADS * MLA_V
MLA_SCALE = (MLA_NOPE + MLA_ROPE) ** -0.5

SSM_GROUP = 16
SSM_GROUPS = 32
SSM_STATE = 64
SSM_W = SSM_GROUP * SSM_GROUPS
SSM_DIRS = 2
DT_MIN = 1e-3
DT_MAX = 1e-1

GQA_HEADS = 8
GQA_KV_HEADS = 2
GQA_HD = 64
GQA_W = GQA_HEADS * GQA_HD
GQA_SCALE = GQA_HD ** -0.5

IN_MLA = MLA_Q_RANK + MLA_KV_RANK + MLA_ROPE
IN_SSM = SSM_W
IN_GQA = (GQA_HEADS + 2 * GQA_KV_HEADS) * GQA_HD
IN_GATE = 3 * D_MODEL
IN_COLS = IN_MLA + IN_SSM + IN_GQA + IN_GATE

N_GROUPS = 4
EXPERTS_PER_GROUP = 4
N_EXPERTS = N_GROUPS * EXPERTS_PER_GROUP
TOP_K = 2
D_EXPERT = 512

kernel_name = "hybrid_mla_s5_gqa_hmoe_diffusion_block"


def rms_norm(x, g):
    xf = x.astype(jnp.float32)
    y = xf * lax.rsqrt(jnp.mean(xf * xf, axis=-1, keepdims=True) + NORM_EPS)
    return (y * g.astype(jnp.float32)).astype(x.dtype)


def rope_1d(x, pos):
    half = x.shape[-1] // 2
    freqs = ROPE_THETA ** (-jnp.arange(half, dtype=jnp.float32) / half)
    ang = pos.astype(jnp.float32)[:, None] * freqs[None, :]
    shape = (1, pos.shape[0]) + (1,) * (x.ndim - 3) + (half,)
    cos = jnp.cos(ang).reshape(shape)
    sin = jnp.sin(ang).reshape(shape)
    xf = x.astype(jnp.float32)
    x1, x2 = xf[..., :half], xf[..., half:]
    return jnp.concatenate([x1 * cos - x2 * sin, x1 * sin + x2 * cos], axis=-1).astype(x.dtype)


def rope_2d(x, rows, cols):
    h = x.shape[-1] // 2
    return jnp.concatenate([rope_1d(x[..., :h], rows), rope_1d(x[..., h:], cols)], axis=-1)


def block_attention(q, k, v, scale):
    b, sq, hk, g, dk = q.shape
    nb = sq // Q_BLOCK
    qb = jnp.moveaxis(q.reshape(b, nb, Q_BLOCK, hk, g, dk), 1, 0)

    def attend(qblk):
        s = jnp.einsum('bqhgd,bkhd->bhgqk', qblk, k).astype(jnp.float32) * scale
        p = jax.nn.softmax(s, axis=-1).astype(v.dtype)
        return jnp.einsum('bhgqk,bkhd->bqhgd', p, v)

    o = lax.map(attend, qb)
    return jnp.moveaxis(o, 0, 1).reshape(b, sq, hk * g * v.shape[-1])


def mla_qkv(pm, p, pos):
    b, s, _ = pm.shape
    cq, ckv, kr = jnp.split(pm, [MLA_Q_RANK, MLA_Q_RANK + MLA_KV_RANK], axis=-1)
    q = (rms_norm(cq, p['g_cq']) @ p['w_uq']).reshape(b, s, MLA_HEADS, MLA_NOPE + MLA_ROPE)
    kv = (rms_norm(ckv, p['g_ckv']) @ p['w_ukv']).reshape(b, s, MLA_HEADS, MLA_NOPE + MLA_V)
    q_nope, q_rope = q[..., :MLA_NOPE], q[..., MLA_NOPE:]
    k_nope, v = kv[..., :MLA_NOPE], kv[..., MLA_NOPE:]
    if pos is not None:
        q_rope = rope_2d(q_rope, pos[0], pos[1])
        kr = rope_2d(kr, pos[0], pos[1])
    k_rope = jnp.broadcast_to(kr[:, :, None, :], (b, s, MLA_HEADS, MLA_ROPE))
    q = jnp.concatenate([q_nope, q_rope], axis=-1)[:, :, :, None, :]
    k = jnp.concatenate([k_nope, k_rope], axis=-1)
    return q, k, v


def gqa_qkv(pg, p, pos):
    b, s, _ = pg.shape
    q, k, v = jnp.split(pg, [GQA_W, GQA_W + GQA_KV_HEADS * GQA_HD], axis=-1)
    q = rms_norm(q.reshape(b, s, GQA_HEADS, GQA_HD), p['g_qn'])
    k = rms_norm(k.reshape(b, s, GQA_KV_HEADS, GQA_HD), p['g_kn'])
    v = v.reshape(b, s, GQA_KV_HEADS, GQA_HD)
    if pos is not None:
        q = rope_2d(q, pos[0], pos[1])
        k = rope_2d(k, pos[0], pos[1])
    q = q.reshape(b, s, GQA_KV_HEADS, GQA_HEADS // GQA_KV_HEADS, GQA_HD)
    return q, k, v


def s5_discretize(a_re, a_im, log_dt, b_re, b_im):
    a_re = a_re.astype(jnp.float32)
    a_im = a_im.astype(jnp.float32)
    b_re = b_re.astype(jnp.float32)
    b_im = b_im.astype(jnp.float32)
    dt = jnp.exp(log_dt.astype(jnp.float32))[:, None]
    mag = jnp.exp(a_re * dt)
    lr = mag * jnp.cos(a_im * dt)
    li = mag * jnp.sin(a_im * dt)
    den = a_re * a_re + a_im * a_im
    cr = ((lr - 1.0) * a_re + li * a_im) / den
    ci = (li * a_re - (lr - 1.0) * a_im) / den
    bbr = cr[..., None] * b_re - ci[..., None] * b_im
    bbi = cr[..., None] * b_im + ci[..., None] * b_re
    return lr, li, bbr, bbi


def s5_combine(e1, e2):
    a1r, a1i, b1r, b1i = e1
    a2r, a2i, b2r, b2i = e2
    return (a1r * a2r - a1i * a2i,
            a1r * a2i + a1i * a2r,
            a2r * b1r - a2i * b1i + b2r,
            a2r * b1i + a2i * b1r + b2i)


def s5_scan(bu_r, bu_i, lr, li, s0, reverse):
    if s0 is not None:
        s0r, s0i = s0
        idx = -1 if reverse else 0
        bu_r = bu_r.at[:, idx].add(lr * s0r - li * s0i)
        bu_i = bu_i.at[:, idx].add(lr * s0i + li * s0r)
    ar = jnp.broadcast_to(lr, bu_r.shape)
    ai = jnp.broadcast_to(li, bu_i.shape)
    _, _, sr, si = lax.associative_scan(s5_combine, (ar, ai, bu_r, bu_i), reverse=reverse, axis=1)
    return sr, si


def s5_readout(sr, si, c_re, c_im):
    return jnp.einsum('bsgp,gcp->bsgc', sr, c_re) - jnp.einsum('bsgp,gcp->bsgc', si, c_im)


def s5_mixer(u_lat, u_ctx, p, with_ctx_out):
    b, s, _ = u_lat.shape
    lc = u_ctx.shape[1]
    ux = u_lat.astype(jnp.float32).reshape(b, s, SSM_GROUPS, SSM_GROUP)
    uc = u_ctx.astype(jnp.float32).reshape(b, lc, SSM_GROUPS, SSM_GROUP)
    d = p['ssm_d'].astype(jnp.float32).reshape(SSM_GROUPS, SSM_GROUP)
    yx = d * ux
    yc = d * uc
    for dr in range(SSM_DIRS):
        reverse = dr == 1
        lr, li, bbr, bbi = s5_discretize(p['ssm_a_re'][dr], p['ssm_a_im'][dr], p['ssm_log_dt'][dr],
                                         p['ssm_b_re'][dr], p['ssm_b_im'][dr])
        c_re = p['ssm_c_re'][dr].astype(jnp.float32)
        c_im = p['ssm_c_im'][dr].astype(jnp.float32)
        scr, sci = s5_scan(jnp.einsum('bsgc,gpc->bsgp', uc, bbr), jnp.einsum('bsgc,gpc->bsgp', uc, bbi),
                           lr, li, None, reverse)
        end = 0 if reverse else -1
        sxr, sxi = s5_scan(jnp.einsum('bsgc,gpc->bsgp', ux, bbr), jnp.einsum('bsgc,gpc->bsgp', ux, bbi),
                           lr, li, (scr[:, end], sci[:, end]), reverse)
        yx = yx + s5_readout(sxr, sxi, c_re, c_im)
        if with_ctx_out:
            yc = yc + s5_readout(scr, sci, c_re, c_im)

    def glu(y):
        y1 = jax.nn.gelu(y.reshape(y.shape[0], y.shape[1], SSM_W))
        return y1 * jax.nn.sigmoid(y1 @ p['w_glu'] + p['b_glu'])

    out_x = glu(yx).astype(u_lat.dtype)
    out_c = glu(yc).astype(u_ctx.dtype) if with_ctx_out else None
    return out_x, out_c


def merge_branches(o_mla, o_ssm, o_gqa, gate_logits, p):
    g_mla, g_ssm, g_gqa = jnp.split(jax.nn.sigmoid(gate_logits), 3, axis=-1)
    m = (g_mla * (o_mla @ p['w_br_mla'])
         + g_ssm * (o_ssm @ p['w_br_ssm'])
         + g_gqa * (o_gqa @ p['w_br_gqa']))
    return m @ p['w_out']


def mixer_sublayer(hx, hc, pos, p, with_ctx_out):
    splits = [IN_MLA, IN_MLA + IN_SSM, IN_MLA + IN_SSM + IN_GQA]
    mla_in_x, ssm_in_x, gqa_in_x, gate_x = jnp.split(hx @ p['w_in'], splits, axis=-1)
    mla_in_c, ssm_in_c, gqa_in_c, gate_c = jnp.split(hc @ p['w_in'], splits, axis=-1)
    mq_x, mk_x, mv_x = mla_qkv(mla_in_x, p, pos)
    mq_c, mk_c, mv_c = mla_qkv(mla_in_c, p, None)
    mla_x = block_attention(mq_x, jnp.concatenate([mk_c, mk_x], axis=1),
                            jnp.concatenate([mv_c, mv_x], axis=1), MLA_SCALE)
    ssm_x, ssm_c = s5_mixer(ssm_in_x, ssm_in_c, p, with_ctx_out)
    gq_x, gk_x, gv_x = gqa_qkv(gqa_in_x, p, pos)
    gq_c, gk_c, gv_c = gqa_qkv(gqa_in_c, p, None)
    gqa_x = block_attention(gq_x, jnp.concatenate([gk_c, gk_x], axis=1),
                            jnp.concatenate([gv_c, gv_x], axis=1), GQA_SCALE)
    y_x = merge_branches(mla_x, ssm_x, gqa_x, gate_x, p)
    if not with_ctx_out:
        return y_x, None
    mla_c = block_attention(mq_c, mk_c, mv_c, MLA_SCALE)
    gqa_c = block_attention(gq_c, gk_c, gv_c, GQA_SCALE)
    return y_x, merge_branches(mla_c, ssm_c, gqa_c, gate_c, p)


def moe_sublayer(h, p):
    b, s, d = h.shape
    t = h.reshape(b * s, d)
    lg = (t @ p['w_group'] + p['b_group']).astype(jnp.float32)
    pg = jax.nn.softmax(lg, axis=-1)
    g_sel = jnp.argmax(lg, axis=-1)
    le = (t @ p['w_router'] + p['b_router']).astype(jnp.float32)
    le = le.reshape(b * s, N_GROUPS, EXPERTS_PER_GROUP)
    le_sel = jnp.take_along_axis(le, g_sel[:, None, None], axis=1)[:, 0]
    pe = jax.nn.softmax(le_sel, axis=-1)
    top_v, top_i = lax.top_k(pe, TOP_K)
    top_v = top_v / jnp.sum(top_v, axis=-1, keepdims=True)
    wts = jnp.take_along_axis(pg, g_sel[:, None], axis=1) * top_v
    expert_id = g_sel[:, None] * EXPERTS_PER_GROUP + top_i
    combine = jnp.sum(jax.nn.one_hot(expert_id, N_EXPERTS, dtype=jnp.float32) * wts[..., None], axis=1)
    out = jnp.zeros((b * s, d), jnp.float32)
    for e in range(N_EXPERTS):
        hid = jax.nn.silu(t @ p['w_exp_gate'][e]) * (t @ p['w_exp_up'][e])
        out = out + combine[:, e:e + 1] * (hid @ p['w_exp_down'][e]).astype(jnp.float32)
    return out.astype(h.dtype).reshape(b, s, d)


def modulation(cond, w_mod, b_mod):
    return jnp.split(jax.nn.silu(cond) @ w_mod + b_mod, N_MOD, axis=-1)


def modulate(x, g, shift, scale):
    return rms_norm(x, g) * (1.0 + scale) + shift


def setup_inputs(seed: int = 0) -> dict:
    key = jax.random.key(seed)
    ks = iter(jax.random.split(key, 48))

    def nrm(shape, scale):
        return jax.random.normal(next(ks), shape, jnp.float32) * scale

    def gain(shape):
        return 1.0 + nrm(shape, 0.05)

    L, D = DEPTH, D_MODEL
    ssm_shape = (L, SSM_DIRS, SSM_GROUPS, SSM_STATE)
    n_idx = jnp.arange(SSM_STATE, dtype=jnp.float32)
    return {
        'x': nrm((BATCH, SEQ, D), 1.0),
        'c': nrm((BATCH, D), 1.0),
        'ctx': nrm((BATCH, CTX_LEN, D), 1.0),
        'c_ctx': nrm((D,), 1.0),
        'w_mod': nrm((L, D, N_MOD * D), 0.5 * D ** -0.5),
        'b_mod': nrm((L, N_MOD * D), 0.01),
        'g_pre_mix': gain((L, D)),
        'g_post_mix': gain((L, D)),
        'g_pre_ffn': gain((L, D)),
        'g_post_ffn': gain((L, D)),
        'w_in': nrm((L, D, IN_COLS), D ** -0.5),
        'g_cq': gain((L, MLA_Q_RANK)),
        'g_ckv': gain((L, MLA_KV_RANK)),
        'w_uq': nrm((L, MLA_Q_RANK, MLA_HEADS * (MLA_NOPE + MLA_ROPE)), MLA_Q_RANK ** -0.5),
        'w_ukv': nrm((L, MLA_KV_RANK, MLA_HEADS * (MLA_NOPE + MLA_V)), MLA_KV_RANK ** -0.5),
        'g_qn': gain((L, GQA_HD)),
        'g_kn': gain((L, GQA_HD)),
        'ssm_a_re': -0.5 + nrm(ssm_shape, 0.01),
        'ssm_a_im': math.pi * n_idx + nrm(ssm_shape, 0.01),
        'ssm_log_dt': jax.random.uniform(next(ks), (L, SSM_DIRS, SSM_GROUPS), jnp.float32,
                                         minval=math.log(DT_MIN), maxval=math.log(DT_MAX)),
        'ssm_b_re': nrm((L, SSM_DIRS, SSM_GROUPS, SSM_STATE, SSM_GROUP), (2 * SSM_GROUP) ** -0.5),
        'ssm_b_im': nrm((L, SSM_DIRS, SSM_GROUPS, SSM_STATE, SSM_GROUP), (2 * SSM_GROUP) ** -0.5),
        'ssm_c_re': nrm((L, SSM_DIRS, SSM_GROUPS, SSM_GROUP, SSM_STATE), 0.5),
        'ssm_c_im': nrm((L, SSM_DIRS, SSM_GROUPS, SSM_GROUP, SSM_STATE), 0.5),
        'ssm_d': nrm((L, SSM_W), 1.0),
        'w_glu': nrm((L, SSM_W, SSM_W), SSM_W ** -0.5),
        'b_glu': nrm((L, SSM_W), 0.01),
        'w_br_mla': nrm((L, MLA_W, D), MLA_W ** -0.5),
        'w_br_ssm': nrm((L, SSM_W, D), SSM_W ** -0.5),
        'w_br_gqa': nrm((L, GQA_W, D), GQA_W ** -0.5),
        'w_out': nrm((L, D, D), D ** -0.5),
        'w_group': nrm((L, D, N_GROUPS), D ** -0.5),
        'b_group': nrm((L, N_GROUPS), 0.01),
        'w_router': nrm((L, D, N_EXPERTS), D ** -0.5),
        'b_router': nrm((L, N_EXPERTS), 0.01),
        'w_exp_gate': nrm((L, N_EXPERTS, D, D_EXPERT), D ** -0.5),
        'w_exp_up': nrm((L, N_EXPERTS, D, D_EXPERT), D ** -0.5),
        'w_exp_down': nrm((L, N_EXPERTS, D_EXPERT, D), D_EXPERT ** -0.5),
    }


def reference(x, c, ctx, c_ctx, w_mod, b_mod, g_pre_mix, g_post_mix, g_pre_ffn, g_post_ffn,
              w_in, g_cq, g_ckv, w_uq, w_ukv, g_qn, g_kn,
              ssm_a_re, ssm_a_im, ssm_log_dt, ssm_b_re, ssm_b_im, ssm_c_re, ssm_c_im, ssm_d,
              w_glu, b_glu, w_br_mla, w_br_ssm, w_br_gqa, w_out,
              w_group, b_group, w_router, b_router, w_exp_gate, w_exp_up, w_exp_down):
    s = x.shape[1]
    lc = ctx.shape[1]
    rows_n = s // GRID_W
    rows = jnp.repeat(jnp.arange(rows_n, dtype=jnp.int32), GRID_W)
    cols = jnp.tile(jnp.arange(GRID_W, dtype=jnp.int32), rows_n)
    pos = (rows, cols)
    for l in range(DEPTH):
        last = l == DEPTH - 1
        p = dict(w_in=w_in[l], g_cq=g_cq[l], g_ckv=g_ckv[l], w_uq=w_uq[l], w_ukv=w_ukv[l],
                 g_qn=g_qn[l], g_kn=g_kn[l],
                 ssm_a_re=ssm_a_re[l], ssm_a_im=ssm_a_im[l], ssm_log_dt=ssm_log_dt[l],
                 ssm_b_re=ssm_b_re[l], ssm_b_im=ssm_b_im[l], ssm_c_re=ssm_c_re[l], ssm_c_im=ssm_c_im[l],
                 ssm_d=ssm_d[l], w_glu=w_glu[l], b_glu=b_glu[l],
                 w_br_mla=w_br_mla[l], w_br_ssm=w_br_ssm[l], w_br_gqa=w_br_gqa[l], w_out=w_out[l],
                 w_group=w_group[l], b_group=b_group[l], w_router=w_router[l], b_router=b_router[l],
                 w_exp_gate=w_exp_gate[l], w_exp_up=w_exp_up[l], w_exp_down=w_exp_down[l])
        sh1, sc1, ga1, sh2, sc2, ga2 = [m[:, None, :] for m in modulation(c, w_mod[l], b_mod[l])]
        csh1, csc1, cga1, csh2, csc2, cga2 = modulation(c_ctx, w_mod[l], b_mod[l])
        hx = modulate(x, g_pre_mix[l], sh1, sc1)
        hc = modulate(ctx, g_pre_mix[l], csh1, csc1)
        y_x, y_c = mixer_sublayer(hx, hc, pos, p, not last)
        x = x + ga1 * rms_norm(y_x, g_post_mix[l])
        if last:
            hx = modulate(x, g_pre_ffn[l], sh2, sc2)
            x = x + ga2 * rms_norm(moe_sublayer(hx, p), g_post_ffn[l])
        else:
            ctx = ctx + cga1 * rms_norm(y_c, g_post_mix[l])
            hx = modulate(x, g_pre_ffn[l], sh2, sc2)
            hc = modulate(ctx, g_pre_ffn[l], csh2, csc2)
            y = moe_sublayer(jnp.concatenate([hc, hx], axis=1), p)
            x = x + ga2 * rms_norm(y[:, lc:], g_post_ffn[l])
            ctx = ctx + cga2 * rms_norm(y[:, :lc], g_post_ffn[l])
    return x
```

```python
import functools
import math

import jax
import jax.numpy as jnp
import numpy as np
from jax import lax
from jax.experimental import pallas as pl
from jax.experimental.pallas import tpu as pltpu

F32 = jnp.float32
BF16 = jnp.bfloat16

GRID_W = 64
ROPE_THETA = 10000.0
NORM_EPS = 1e-6
N_MOD = 6

MLA_HEADS = 8
MLA_NOPE = 64
MLA_ROPE = 32
MLA_V = 64
MLA_Q_RANK = 256
MLA_KV_RANK = 128
MLA_SCALE = (MLA_NOPE + MLA_ROPE) ** -0.5
MLA_HEAD_PAD = 128

SSM_GROUP = 16
SSM_GROUPS = 32
SSM_STATE = 64
SSM_W = SSM_GROUP * SSM_GROUPS
SSM_DIRS = 2
SSM_CHUNK = 16
SSM_SUPER = 16
SSM_CW = SSM_CHUNK * SSM_GROUP

GQA_HEADS = 8
GQA_KV_HEADS = 2
GQA_HD = 64
GQA_W = GQA_HEADS * GQA_HD
GQA_SCALE = GQA_HD ** -0.5

N_GROUPS = 4
EXPERTS_PER_GROUP = 4
N_EXPERTS = N_GROUPS * EXPERTS_PER_GROUP
D_EXPERT = 512

ROW_TILE = 256
LANES = 128
V_ROWS = 80
VMEM_LIMIT = 56 * 1024 * 1024


def _cparams(*sem):
    return pltpu.CompilerParams(dimension_semantics=sem, vmem_limit_bytes=VMEM_LIMIT)


def _dot(a, b):
    return jnp.dot(a, b, preferred_element_type=F32)


def _dot_f32(a, b):
    a_hi = a.astype(BF16)
    a_lo = (a - a_hi.astype(F32)).astype(BF16)
    return _dot(a_hi, b) + _dot(a_lo, b)


def _rms(x, g):
    return x * lax.rsqrt(jnp.mean(x * x, axis=-1, keepdims=True) + NORM_EPS) * g


def _sigmoid(x):
    return 1.0 / (1.0 + jnp.exp(-x))


def _mod_kernel(cond_ref, w_ref, b_ref, o_ref):
    a = cond_ref[...]
    s = a * _sigmoid(a)
    w = w_ref[...]
    s_hi = s.astype(BF16)
    s_lo = (s - s_hi.astype(F32)).astype(BF16)
    w_hi = w.astype(BF16)
    w_lo = (w - w_hi.astype(F32)).astype(BF16)
    acc = _dot(s_hi, w_hi) + _dot(s_lo, w_hi) + _dot(s_hi, w_lo)
    o_ref[...] = acc + b_ref[...]


def _modulation(cond8, w_mod, b_mod):
    nl, d, n = w_mod.shape
    tn = 1536
    return pl.pallas_call(
        _mod_kernel,
        grid=(nl, n // tn),
        in_specs=[
            pl.BlockSpec((8, d), lambda l, j: (0, 0)),
            pl.BlockSpec((None, d, tn), lambda l, j: (l, 0, j)),
            pl.BlockSpec((None, 1, tn), lambda l, j: (l, 0, j)),
        ],
        out_specs=pl.BlockSpec((None, 8, tn), lambda l, j: (l, 0, j)),
        out_shape=jax.ShapeDtypeStruct((nl, 8, n), F32),
        compiler_params=_cparams("parallel", "parallel"),
        name="modulation",
    )(cond8, w_mod, b_mod.reshape(nl, 1, n))


def _mod_row(mod_ref, idx):
    m = mod_ref[...]
    d = m.shape[1] // N_MOD
    lat = m[0:1, idx * d:(idx + 1) * d]
    ctx = m[1:2, idx * d:(idx + 1) * d]
    return jnp.where(pl.program_id(0) == 0, ctx, lat)


_SEG = {}
_off = 0
for _name, _w in (("cq", 256), ("ckv", 128), ("kr", 128), ("krs", 128), ("ssm", 512), ("gq", 512),
                  ("gqs", 512), ("gk", 128), ("gks", 128), ("gv", 128), ("gate", 3072)):
    _SEG[_name] = (_off, _off + _w)
    _off += _w
W_CAT = _off


def _premix_kernel(x_ref, mod_ref, gpre_ref, w_ref, wuq_ref, wuqs_ref, wuk_ref, wuv_ref,
                   gcq_ref, gckv_ref, gq_ref, gqs_ref, gk_ref, gks_ref, ones_ref,
                   cm_ref, sm_ref, cg_ref, sg_ref,
                   qm_ref, km_ref, vm_ref, qg_ref, kg_ref, vg_ref, u_ref, gate_ref):
    x = x_ref[...]
    h = _rms(x, gpre_ref[...]) * (1.0 + _mod_row(mod_ref, 1)) + _mod_row(mod_ref, 0)
    hb = h.astype(BF16)

    def proj(name):
        a, b = _SEG[name]
        return _dot(hb, w_ref[:, a:b])

    cm = cm_ref[...]
    sm = sm_ref[...]
    cg = cg_ref[...]
    sg = sg_ref[...]

    cqn = _rms(proj("cq"), gcq_ref[...]).astype(BF16)
    ckvn = _rms(proj("ckv"), gckv_ref[...]).astype(BF16)
    q = _dot(cqn, wuq_ref[...])
    qs = _dot(cqn, wuqs_ref[...])
    kn = _dot(ckvn, wuk_ref[...])
    kr = proj("kr") * cm + proj("krs") * sm
    for hd in range(MLA_HEADS):
        sl = slice(hd * MLA_HEAD_PAD, (hd + 1) * MLA_HEAD_PAD)
        qm_ref[:, sl] = ((q[:, sl] * cm + qs[:, sl] * sm) * MLA_SCALE).astype(BF16)
        km_ref[:, sl] = (kn[:, sl] + kr).astype(BF16)
    vm_ref[...] = _dot(ckvn, wuv_ref[...]).astype(BF16)

    ones = ones_ref[...]

    def head_rms_scale(v, width):
        ms = _dot_f32(v * v, ones[:width, :width]) * (1.0 / GQA_HD)
        return lax.rsqrt(ms + NORM_EPS)

    gq = proj("gq")
    gqs = proj("gqs")
    rq = head_rms_scale(gq, GQA_W)
    cg4 = jnp.concatenate([cg] * (GQA_W // LANES), axis=1)
    sg4 = jnp.concatenate([sg] * (GQA_W // LANES), axis=1)
    qg = rq * (gq * gq_ref[...] * cg4 + gqs * gqs_ref[...] * sg4)
    qg_ref[...] = (qg * GQA_SCALE).astype(BF16)
    gk = proj("gk")
    gks = proj("gks")
    rk = head_rms_scale(gk, GQA_KV_HEADS * GQA_HD)
    kg_ref[...] = (rk * (gk * gk_ref[...] * cg + gks * gks_ref[...] * sg)).astype(BF16)
    vg_ref[...] = proj("gv").astype(BF16)

    u_ref[...] = proj("ssm").astype(BF16)
    gate_ref[...] = _sigmoid(proj("gate")).astype(BF16)


def _premix(rows, mod_l, g_pre, wts, tabs):
    r, d = rows.shape
    nt = r // ROW_TILE
    row_spec = lambda w: pl.BlockSpec((ROW_TILE, w), lambda i: (i, 0))
    full = lambda a: pl.BlockSpec(a.shape, lambda i: (0,) * a.ndim)
    consts = [g_pre, wts["w_cat"], wts["wuq"], wts["wuqs"], wts["wuk"], wts["wuv"],
              wts["g_cq"], wts["g_ckv"], wts["g_q"], wts["g_qs"], wts["g_k"], wts["g_ks"], wts["ones"]]
    out_w = [MLA_HEADS * MLA_HEAD_PAD, MLA_HEADS * MLA_HEAD_PAD, MLA_HEADS * MLA_V,
             GQA_W, GQA_KV_HEADS * GQA_HD, GQA_KV_HEADS * GQA_HD, SSM_W, 3 * d]
    return pl.pallas_call(
        _premix_kernel,
        grid=(nt,),
        in_specs=[row_spec(d), full(mod_l)] + [full(a) for a in consts] + [row_spec(LANES)] * 4,
        out_specs=[row_spec(w) for w in out_w],
        out_shape=[jax.ShapeDtypeStruct((r, w), BF16) for w in out_w],
        compiler_params=_cparams("parallel"),
        name="premix",
    )(rows, mod_l, *consts, tabs["cm"], tabs["sm"], tabs["cg"], tabs["sg"])


def _attn_kernel(q_ref, k_ref, v_ref, o_ref, *, tk, n_kv, n_kv_ctx, dv):
    q = q_ref[...]
    tq = q.shape[1]
    nkv = jnp.where(pl.program_id(1) == 0, n_kv_ctx, n_kv)

    def body(j, carry):
        m, acc = carry
        off = pl.multiple_of(j * tk, tk)
        s = _dot(k_ref[pl.ds(off, tk), :], q)
        m_new = jnp.maximum(m, jnp.max(s, axis=0, keepdims=True))
        alpha = jnp.exp(m - m_new)
        p = jnp.exp(s - m_new).astype(BF16)
        acc = alpha * acc + _dot(v_ref[:, pl.ds(off, tk)], p)
        return m_new, acc

    m0 = jnp.full((1, tq), -1e30, F32)
    acc0 = jnp.zeros((V_ROWS, tq), F32)
    _, acc = lax.fori_loop(0, nkv, body, (m0, acc0))
    o_ref[...] = acc[:dv] / acc[dv:dv + 1]


def _attention(qt, k, vt, *, tq=256, tk=256):
    nh, dq, r = qt.shape
    nk = k.shape[0]
    grp = nh // nk
    dv = V_ROWS - 16
    kern = functools.partial(_attn_kernel, tk=tk, n_kv=r // tk, n_kv_ctx=ROW_TILE // tk, dv=dv)
    return pl.pallas_call(
        kern,
        grid=(nh, r // tq),
        in_specs=[
            pl.BlockSpec((None, dq, tq), lambda h, i: (h, 0, i)),
            pl.BlockSpec((None, r, dq), lambda h, i: (h // grp, 0, 0)),
            pl.BlockSpec((None, V_ROWS, r), lambda h, i: (h // grp, 0, 0)),
        ],
        out_specs=pl.BlockSpec((None, dv, tq), lambda h, i: (h, 0, i)),
        out_shape=jax.ShapeDtypeStruct((nh, dv, r), F32),
        compiler_params=_cparams("parallel", "arbitrary"),
        name="attention",
    )(qt, k, vt)


def _to_vt(v_rows, nheads):
    r = v_rows.shape[0]
    vt = jnp.transpose(v_rows.reshape(r, nheads, 64), (1, 2, 0))
    pad = jnp.zeros((nheads, V_ROWS - 64, r), v_rows.dtype).at[:, 0, :].set(1.0)
    return jnp.concatenate([vt, pad], axis=1)


def _s5_kernel(u_ref, toe_ref, bre_ref, bim_ref, cre_ref, cim_ref, apr_ref, api_ref, y_ref,
               xcr_s, xci_s, xir_s, xii_s, *, gb, sp):
    nb = SSM_SUPER
    for g in range(gb):
        u = u_ref[g]
        xcr_s[...] = _dot(u, bre_ref[g])
        xci_s[...] = _dot(u, bim_ref[g])
        apr = apr_ref[g]
        api = api_ref[g]
        a1r, a1i = apr[1:2], api[1:2]
        lr = jnp.zeros((sp, SSM_STATE), F32)
        li = jnp.zeros((sp, SSM_STATE), F32)
        for b in range(nb):
            xir_s[b * sp:(b + 1) * sp, :] = lr
            xii_s[b * sp:(b + 1) * sp, :] = li
            cr = xcr_s[b * sp:(b + 1) * sp, :]
            ci = xci_s[b * sp:(b + 1) * sp, :]
            lr, li = a1r * lr - a1i * li + cr, a1r * li + a1i * lr + ci
        anr, ani = apr[nb:nb + 1], api[nb:nb + 1]
        sr = jnp.zeros((1, SSM_STATE), F32)
        si = jnp.zeros((1, SSM_STATE), F32)
        rows_r, rows_i = [], []
        for s in range(sp):
            rows_r.append(sr)
            rows_i.append(si)
            sr, si = anr * sr - ani * si + lr[s:s + 1], anr * si + ani * sr + li[s:s + 1]
        sin_r = jnp.concatenate(rows_r, axis=0)
        sin_i = jnp.concatenate(rows_i, axis=0)
        for b in range(nb):
            pr, pi = apr[b:b + 1], api[b:b + 1]
            xir_s[b * sp:(b + 1) * sp, :] += pr * sin_r - pi * sin_i
            xii_s[b * sp:(b + 1) * sp, :] += pr * sin_i + pi * sin_r
        y = _dot(u, toe_ref[g])
        y += _dot(xir_s[...].astype(BF16), cre_ref[g]) + _dot(xii_s[...].astype(BF16), cim_ref[g])
        y_ref[g] = y


def _s5_scan(ug, mats, *, gb=4):
    nd, ng, rows, cw = ug.shape
    sp = rows // SSM_SUPER
    p = SSM_STATE
    blk = lambda *tail: pl.BlockSpec((None, gb) + tail, lambda d, g: (d, g) + (0,) * len(tail))
    kern = functools.partial(_s5_kernel, gb=gb, sp=sp)
    return pl.pallas_call(
        kern,
        grid=(nd, ng // gb),
        in_specs=[blk(rows, cw), blk(cw, cw), blk(cw, p), blk(cw, p), blk(p, cw), blk(p, cw),
                  blk(24, p), blk(24, p)],
        out_specs=blk(rows, cw),
        out_shape=jax.ShapeDtypeStruct((nd, ng, rows, cw), F32),
        scratch_shapes=[pltpu.VMEM((rows, p), F32)] * 4,
        compiler_params=_cparams("parallel", "arbitrary"),
        name="s5_scan",
    )(ug, mats["toe"], mats["bre"], mats["bim"], mats["cre"], mats["cim"], mats["apr"], mats["api"])


def _s5_matrices(a_re, a_im, log_dt, b_re, b_im, c_re, c_im, d_skip):
    t = SSM_CHUNK
    a_re = a_re.astype(F32)
    a_im = a_im.astype(F32)
    dt = jnp.exp(log_dt.astype(F32))[..., None]
    den = a_re * a_re + a_im * a_im

    def lpow(k):
        kf = jnp.asarray(k, F32)
        mag = jnp.exp(a_re * dt * kf)
        return mag * jnp.cos(a_im * dt * kf), mag * jnp.sin(a_im * dt * kf)

    lr, li = lpow(1.0)
    cr = ((lr - 1.0) * a_re + li * a_im) / den
    ci = (li * a_re - (lr - 1.0) * a_im) / den
    bbr = cr[..., None] * b_re - ci[..., None] * b_im
    bbi = cr[..., None] * b_im + ci[..., None] * b_re
    c_re = c_re.astype(F32)
    c_im = c_im.astype(F32)

    ks = jnp.arange(t + 1, dtype=F32)[:, None, None, None]
    pr, pi = lpow(ks)
    lbr = pr[:t, ..., None] * bbr - pi[:t, ..., None] * bbi
    lbi = pr[:t, ..., None] * bbi + pi[:t, ..., None] * bbr
    kern = (jnp.einsum('dgop,kdgpc->dgkoc', c_re, lbr, precision='highest')
            - jnp.einsum('dgop,kdgpc->dgkoc', c_im, lbi, precision='highest'))
    skip = d_skip.astype(F32).reshape(SSM_GROUPS, SSM_GROUP)
    eye = jnp.eye(SSM_GROUP, dtype=F32)
    kern = kern.at[0, :, 0].add(skip[:, :, None] * eye)
    lag = jnp.arange(t)[None, :] - jnp.arange(t)[:, None]
    blocks = jnp.take(kern, jnp.clip(lag, 0, t - 1).reshape(-1), axis=2)
    blocks = blocks.reshape(SSM_DIRS, SSM_GROUPS, t, t, SSM_GROUP, SSM_GROUP)
    blocks = jnp.where((lag >= 0)[None, None, :, :, None, None], blocks, 0.0)
    toe = jnp.transpose(blocks, (0, 1, 2, 5, 3, 4)).reshape(SSM_DIRS, SSM_GROUPS, SSM_CW, SSM_CW)
    bpr = jnp.transpose(lbr[::-1], (1, 2, 0, 4, 3)).reshape(SSM_DIRS, SSM_GROUPS, SSM_CW, SSM_STATE)
    bpi = jnp.transpose(lbi[::-1], (1, 2, 0, 4, 3)).reshape(SSM_DIRS, SSM_GROUPS, SSM_CW, SSM_STATE)
    p1r = pr[1:, :, :, None, :]
    p1i = pi[1:, :, :, None, :]
    rd_r = c_re * p1r - c_im * p1i
    rd_i = -(c_re * p1i + c_im * p1r)
    cpr = jnp.transpose(rd_r, (1, 2, 4, 0, 3)).reshape(SSM_DIRS, SSM_GROUPS, SSM_STATE, SSM_CW)
    cpi = jnp.transpose(rd_i, (1, 2, 4, 0, 3)).reshape(SSM_DIRS, SSM_GROUPS, SSM_STATE, SSM_CW)
    kk = (jnp.arange(24, dtype=F32) * t)[:, None, None, None]
    kk = jnp.where(kk > SSM_SUPER * t, 0.0, kk)
    qr, qi = lpow(kk)
    apr = jnp.transpose(qr, (1, 2, 0, 3))
    api = jnp.transpose(qi, (1, 2, 0, 3))
    return dict(toe=toe.astype(BF16), bre=bpr.astype(BF16), bim=bpi.astype(BF16),
                cre=cpr.astype(BF16), cim=cpi.astype(BF16), apr=apr, api=api)


def _s5_regroup(u_rows, n_ctx):
    r = u_rows.shape[0]
    t, nb = SSM_CHUNK, SSM_SUPER
    ns = r // (t * nb)
    sp = -(-ns // 8) * 8
    rev = jnp.concatenate([u_rows[:n_ctx][::-1], u_rows[n_ctx:][::-1]], axis=0)
    both = jnp.stack([u_rows, rev])
    x = both.reshape(SSM_DIRS, ns, nb, t, SSM_GROUPS, SSM_GROUP)
    x = jnp.transpose(x, (0, 4, 2, 1, 3, 5))
    x = jnp.pad(x, ((0, 0), (0, 0), (0, 0), (0, sp - ns), (0, 0), (0, 0)))
    return x.reshape(SSM_DIRS, SSM_GROUPS, nb * sp, SSM_CW), ns, sp


def _s5_ungroup(y, ns, sp, n_ctx):
    t, nb = SSM_CHUNK, SSM_SUPER
    y = y.reshape(SSM_DIRS, SSM_GROUPS, nb, sp, t, SSM_GROUP)[:, :, :, :ns]
    y = jnp.transpose(y, (0, 3, 2, 4, 1, 5)).reshape(SSM_DIRS, ns * nb * t, SSM_W)
    yr = jnp.concatenate([y[1, :n_ctx][::-1], y[1, n_ctx:][::-1]], axis=0)
    return y[0], yr


def _merge_kernel(x_ref, mod_ref, om_ref, og_ref, yf_ref, yr_ref, gate_ref,
                  wglu_ref, bglu_ref, wbm_ref, wbs_ref, wbg_ref, wout_ref,
                  gpost_ref, gffn_ref, wrt_hi_ref, wrt_lo_ref, brt_ref,
                  xo_ref, h_ref, cw_ref):
    d = x_ref.shape[1]
    y = yf_ref[...] + yr_ref[...]
    y1 = 0.5 * y * (1.0 + jnp.tanh(math.sqrt(2.0 / math.pi) * (y + 0.044715 * (y * y * y))))
    ssm = y1 * _sigmoid(_dot(y1.astype(BF16), wglu_ref[...]) + bglu_ref[...])
    gate = gate_ref[...]
    m = (gate[:, 0:d].astype(F32) * _dot(om_ref[...], wbm_ref[...])
         + gate[:, d:2 * d].astype(F32) * _dot(ssm.astype(BF16), wbs_ref[...])
         + gate[:, 2 * d:3 * d].astype(F32) * _dot(og_ref[...], wbg_ref[...]))
    ymix = _dot(m.astype(BF16), wout_ref[...])
    x = x_ref[...] + _mod_row(mod_ref, 2) * _rms(ymix, gpost_ref[...])
    xo_ref[...] = x
    h = _rms(x, gffn_ref[...]) * (1.0 + _mod_row(mod_ref, 4)) + _mod_row(mod_ref, 3)
    h_ref[...] = h.astype(BF16)

    h_hi = h.astype(BF16)
    h_lo = (h - h_hi.astype(F32)).astype(BF16)
    lg = (_dot(h_hi, wrt_hi_ref[...]) + _dot(h_lo, wrt_hi_ref[...]) + _dot(h_hi, wrt_lo_ref[...])
          + brt_ref[...])
    lane = lax.broadcasted_iota(jnp.int32, lg.shape, 1)
    neg = jnp.float32(-1e30)
    is_g = lane < N_GROUPS
    gl = jnp.where(is_g, lg, neg)
    gmax = jnp.max(gl, axis=-1, keepdims=True)
    gsel = jnp.min(jnp.where(is_g & (gl == gmax), lane, LANES), axis=-1, keepdims=True)
    pg = 1.0 / jnp.sum(jnp.where(is_g, jnp.exp(gl - gmax), 0.0), axis=-1, keepdims=True)
    lo = N_GROUPS + gsel * EXPERTS_PER_GROUP
    in_grp = (lane >= lo) & (lane < lo + EXPERTS_PER_GROUP)
    el = jnp.where(in_grp, lg, neg)
    m1 = jnp.max(el, axis=-1, keepdims=True)
    i1 = jnp.min(jnp.where(in_grp & (el == m1), lane, LANES), axis=-1, keepdims=True)
    el2 = jnp.where(lane == i1, neg, el)
    m2 = jnp.max(el2, axis=-1, keepdims=True)
    i2 = jnp.min(jnp.where(in_grp & (el2 == m2), lane, LANES), axis=-1, keepdims=True)
    e2 = jnp.exp(m2 - m1)
    t1 = pg / (1.0 + e2)
    cw_ref[...] = jnp.where(lane == i1, t1, 0.0) + jnp.where(lane == i2, t1 * e2, 0.0)


def _merge(rows, mod_l, o_mla, o_gqa, y_f, y_r, gates, wts):
    r, d = rows.shape
    nt = r // ROW_TILE
    row_spec = lambda w: pl.BlockSpec((ROW_TILE, w), lambda i: (i, 0))
    full = lambda a: pl.BlockSpec(a.shape, lambda i: (0,) * a.ndim)
    consts = [wts["w_glu"], wts["b_glu"], wts["w_br_mla"], wts["w_br_ssm"], wts["w_br_gqa"], wts["w_out"],
              wts["g_post_mix"], wts["g_pre_ffn"], wts["w_rt_hi"], wts["w_rt_lo"], wts["b_rt"]]
    return pl.pallas_call(
        _merge_kernel,
        grid=(nt,),
        in_specs=[row_spec(d), full(mod_l), row_spec(512), row_spec(512), row_spec(512), row_spec(512),
                  row_spec(3 * d)] + [full(a) for a in consts],
        out_specs=[row_spec(d), row_spec(d), row_spec(LANES)],
        out_shape=[jax.ShapeDtypeStruct((r, d), F32), jax.ShapeDtypeStruct((r, d), BF16),
                   jax.ShapeDtypeStruct((r, LANES), F32)],
        compiler_params=_cparams("parallel"),
        name="merge",
    )(rows, mod_l, o_mla, o_gqa, y_f, y_r, gates, *consts)


def _moe_kernel(h_ref, cw_ref, x_ref, mod_ref, wg_ref, wu_ref, wd_ref, gpost_ref, o_ref, acc_ref, *, n_ctx):
    e = pl.program_id(1)

    @pl.when(e == 0)
    def _():
        acc_ref[...] = jnp.zeros_like(acc_ref)

    h = h_ref[...]
    a = _dot(h, wg_ref[...])
    hid = (a * _sigmoid(a)) * _dot(h, wu_ref[...])
    lane = lax.broadcasted_iota(jnp.int32, cw_ref.shape, 1)
    w_e = jnp.sum(jnp.where(lane == e + N_GROUPS, cw_ref[...], 0.0), axis=-1, keepdims=True)
    acc_ref[...] += w_e * _dot(hid.astype(BF16), wd_ref[...])

    @pl.when(e == pl.num_programs(1) - 1)
    def _():
        m = mod_ref[...]
        d = m.shape[1] // N_MOD
        tm = o_ref.shape[0]
        row = pl.program_id(0) * tm + lax.broadcasted_iota(jnp.int32, (tm, 1), 0)
        ga = jnp.where(row < n_ctx, m[1:2, 5 * d:6 * d], m[0:1, 5 * d:6 * d])
        o_ref[...] = x_ref[...] + ga * _rms(acc_ref[...], gpost_ref[...])


def _moe(h, cw, rows, mod_l, wg, wu, wd, g_post, n_ctx):
    r, d = rows.shape
    tm = 1280 if r % 1280 == 0 else ROW_TILE
    ne, _, de = wg.shape
    kern = functools.partial(_moe_kernel, n_ctx=n_ctx)
    return pl.pallas_call(
        kern,
        grid=(r // tm, ne),
        in_specs=[
            pl.BlockSpec((tm, d), lambda i, e: (i, 0)),
            pl.BlockSpec((tm, LANES), lambda i, e: (i, 0)),
            pl.BlockSpec((tm, d), lambda i, e: (i, 0)),
            pl.BlockSpec(mod_l.shape, lambda i, e: (0, 0)),
            pl.BlockSpec((None, d, de), lambda i, e: (e, 0, 0)),
            pl.BlockSpec((None, d, de), lambda i, e: (e, 0, 0)),
            pl.BlockSpec((None, de, d), lambda i, e: (e, 0, 0)),
            pl.BlockSpec(g_post.shape, lambda i, e: (0, 0)),
        ],
        out_specs=pl.BlockSpec((tm, d), lambda i, e: (i, 0)),
        out_shape=jax.ShapeDtypeStruct((r, d), F32),
        scratch_shapes=[pltpu.VMEM((tm, d), F32)],
        compiler_params=_cparams("parallel", "arbitrary"),
        name="moe",
    )(h, cw, rows, mod_l, wg, wu, wd, g_post)


def _rope_tables(seq, n_ctx):
    tpos = jnp.arange(seq, dtype=jnp.int32)
    rows = (tpos // GRID_W).astype(F32)
    cols = (tpos % GRID_W).astype(F32)

    def pattern(width):
        half = width // 4
        freqs = ROPE_THETA ** (-jnp.arange(half, dtype=F32) / half)
        ar = rows[:, None] * freqs[None, :]
        ac = cols[:, None] * freqs[None, :]
        c = jnp.concatenate([jnp.cos(ar), jnp.cos(ar), jnp.cos(ac), jnp.cos(ac)], axis=1)
        s = jnp.concatenate([-jnp.sin(ar), jnp.sin(ar), -jnp.sin(ac), jnp.sin(ac)], axis=1)
        return c, s

    c32, s32 = pattern(MLA_ROPE)
    c64, s64 = pattern(GQA_HD)
    one = jnp.ones((seq, 1), F32)
    cm = jnp.concatenate([one * jnp.ones((1, MLA_NOPE)), c32, one * jnp.ones((1, 32))], axis=1)
    sm = jnp.concatenate([jnp.zeros((seq, MLA_NOPE)), s32, jnp.zeros((seq, 32))], axis=1)
    cg = jnp.concatenate([c64, c64], axis=1)
    sg = jnp.concatenate([s64, s64], axis=1)
    ctx_c = jnp.ones((n_ctx, LANES), F32)
    ctx_s = jnp.zeros((n_ctx, LANES), F32)
    cat = lambda a, b: jnp.concatenate([a, b], axis=0)
    return dict(cm=cat(ctx_c, cm), sm=cat(ctx_s, sm), cg=cat(ctx_c, cg), sg=cat(ctx_s, sg))


def _swap_perm(width):
    q = width // 4
    return np.concatenate([np.arange(q, 2 * q), np.arange(0, q), np.arange(3 * q, 4 * q), np.arange(2 * q, 3 * q)])


def _layer_weights(l, p):
    d = p["w_in"].shape[1]
    w_in = p["w_in"][l]
    o = 0
    cq = w_in[:, o:o + MLA_Q_RANK]; o += MLA_Q_RANK
    ckv = w_in[:, o:o + MLA_KV_RANK]; o += MLA_KV_RANK
    kr = w_in[:, o:o + MLA_ROPE]; o += MLA_ROPE
    ssm = w_in[:, o:o + SSM_W]; o += SSM_W
    gq = w_in[:, o:o + GQA_W]; o += GQA_W
    gk = w_in[:, o:o + GQA_KV_HEADS * GQA_HD]; o += GQA_KV_HEADS * GQA_HD
    gv = w_in[:, o:o + GQA_KV_HEADS * GQA_HD]; o += GQA_KV_HEADS * GQA_HD
    gate = w_in[:, o:]
    p32 = _swap_perm(MLA_ROPE)
    p64 = _swap_perm(GQA_HD)
    perm_q = np.concatenate([p64 + GQA_HD * h for h in range(GQA_HEADS)])
    perm_k = np.concatenate([p64 + GQA_HD * h for h in range(GQA_KV_HEADS)])
    z = lambda n: jnp.zeros((d, n), F32)
    kr128 = jnp.concatenate([z(MLA_NOPE), kr, z(32)], axis=1)
    krs128 = jnp.concatenate([z(MLA_NOPE), kr[:, p32], z(32)], axis=1)
    w_cat = jnp.concatenate([cq, ckv, kr128, krs128, ssm, gq, gq[:, perm_q], gk, gk[:, perm_k], gv, gate],
                            axis=1).astype(BF16)

    w_uq = p["w_uq"][l].reshape(MLA_Q_RANK, MLA_HEADS, MLA_NOPE + MLA_ROPE)
    zq = jnp.zeros((MLA_Q_RANK, MLA_HEADS, 32), F32)
    wuq = jnp.concatenate([w_uq, zq], axis=2).reshape(MLA_Q_RANK, -1)
    wuqs = jnp.concatenate([jnp.zeros((MLA_Q_RANK, MLA_HEADS, MLA_NOPE), F32),
                            w_uq[:, :, MLA_NOPE:][:, :, p32], zq], axis=2).reshape(MLA_Q_RANK, -1)
    w_ukv = p["w_ukv"][l].reshape(MLA_KV_RANK, MLA_HEADS, MLA_NOPE + MLA_V)
    wuk = jnp.concatenate([w_ukv[:, :, :MLA_NOPE], jnp.zeros((MLA_KV_RANK, MLA_HEADS, 64), F32)],
                          axis=2).reshape(MLA_KV_RANK, -1)
    wuv = w_ukv[:, :, MLA_NOPE:].reshape(MLA_KV_RANK, -1)

    g_qn = p["g_qn"][l]
    g_kn = p["g_kn"][l]
    blk = np.arange(GQA_W) // GQA_HD
    ones = jnp.asarray((blk[:, None] == blk[None, :]).astype(np.float32), BF16)

    w_rt = jnp.zeros((d, LANES), F32)
    w_rt = w_rt.at[:, :N_GROUPS].set(p["w_group"][l]).at[:, N_GROUPS:N_GROUPS + N_EXPERTS].set(p["w_router"][l])
    w_rt_hi = w_rt.astype(BF16)
    w_rt_lo = (w_rt - w_rt_hi.astype(F32)).astype(BF16)
    b_rt = jnp.zeros((1, LANES), F32)
    b_rt = b_rt.at[0, :N_GROUPS].set(p["b_group"][l]).at[0, N_GROUPS:N_GROUPS + N_EXPERTS].set(p["b_router"][l])

    row = lambda v: v.reshape(1, -1).astype(F32)
    return dict(
        w_cat=w_cat, wuq=wuq.astype(BF16), wuqs=wuqs.astype(BF16), wuk=wuk.astype(BF16), wuv=wuv.astype(BF16),
        g_cq=row(p["g_cq"][l]), g_ckv=row(p["g_ckv"][l]),
        g_q=row(jnp.tile(g_qn, GQA_HEADS)), g_qs=row(jnp.tile(g_qn[p64], GQA_HEADS)),
        g_k=row(jnp.tile(g_kn, GQA_KV_HEADS)), g_ks=row(jnp.tile(g_kn[p64], GQA_KV_HEADS)),
        ones=ones,
        w_glu=p["w_glu"][l].astype(BF16), b_glu=row(p["b_glu"][l]),
        w_br_mla=p["w_br_mla"][l].astype(BF16), w_br_ssm=p["w_br_ssm"][l].astype(BF16),
        w_br_gqa=p["w_br_gqa"][l].astype(BF16), w_out=p["w_out"][l].astype(BF16),
        g_post_mix=row(p["g_post_mix"][l]), g_pre_ffn=row(p["g_pre_ffn"][l]),
        w_rt_hi=w_rt_hi, w_rt_lo=w_rt_lo, b_rt=b_rt,
    )


def kernel(x, c, ctx, c_ctx, w_mod, b_mod, g_pre_mix, g_post_mix, g_pre_ffn, g_post_ffn, w_in, g_cq, g_ckv, w_uq, w_ukv, g_qn, g_kn, ssm_a_re, ssm_a_im, ssm_log_dt, ssm_b_re, ssm_b_im, ssm_c_re, ssm_c_im, ssm_d, w_glu, b_glu, w_br_mla, w_br_ssm, w_br_gqa, w_out, w_group, b_group, w_router, b_router, w_exp_gate, w_exp_up, w_exp_down):
    assert x.shape[0] == 1 and ctx.shape[0] == 1
    seq, d = x.shape[1], x.shape[2]
    n_ctx = ctx.shape[1]
    assert n_ctx == ROW_TILE and seq % (SSM_CHUNK * SSM_SUPER) == 0
    depth = w_in.shape[0]
    params = dict(w_in=w_in, g_cq=g_cq, g_ckv=g_ckv, w_uq=w_uq, w_ukv=w_ukv, g_qn=g_qn, g_kn=g_kn,
                  w_glu=w_glu, b_glu=b_glu, w_br_mla=w_br_mla, w_br_ssm=w_br_ssm, w_br_gqa=w_br_gqa,
                  w_out=w_out, g_post_mix=g_post_mix, g_pre_ffn=g_pre_ffn,
                  w_group=w_group, b_group=b_group, w_router=w_router, b_router=b_router)

    cond8 = jnp.zeros((8, d), F32).at[0].set(c[0]).at[1].set(c_ctx)
    mods = _modulation(cond8, w_mod, b_mod)
    tabs = _rope_tables(seq, n_ctx)
    rows = jnp.concatenate([ctx[0], x[0]], axis=0)
    r = rows.shape[0]

    for l in range(depth):
        wts = _layer_weights(l, params)
        mod_l = mods[l]
        qm, km, vm, qg, kg, vg, u, gates = _premix(rows, mod_l, g_pre_mix[l].reshape(1, d), wts, tabs)

        o_mla = _attention(jnp.transpose(qm.reshape(r, MLA_HEADS, MLA_HEAD_PAD), (1, 2, 0)),
                           jnp.transpose(km.reshape(r, MLA_HEADS, MLA_HEAD_PAD), (1, 0, 2)),
                           _to_vt(vm, MLA_HEADS))
        o_gqa = _attention(jnp.transpose(qg.reshape(r, GQA_HEADS, GQA_HD), (1, 2, 0)),
                           jnp.transpose(kg.reshape(r, GQA_KV_HEADS, GQA_HD), (1, 0, 2)),
                           _to_vt(vg, GQA_KV_HEADS))
        o_mla = jnp.transpose(o_mla, (2, 0, 1)).reshape(r, MLA_HEADS * MLA_V).astype(BF16)
        o_gqa = jnp.transpose(o_gqa, (2, 0, 1)).reshape(r, GQA_W).astype(BF16)

        mats = _s5_matrices(ssm_a_re[l], ssm_a_im[l], ssm_log_dt[l], ssm_b_re[l], ssm_b_im[l],
                            ssm_c_re[l], ssm_c_im[l], ssm_d[l])
        ug, ns, sp = _s5_regroup(u, n_ctx)
        y_f, y_r = _s5_ungroup(_s5_scan(ug, mats), ns, sp, n_ctx)

        rows, h, cw = _merge(rows, mod_l, o_mla, o_gqa, y_f, y_r, gates, wts)
        rows = _moe(h, cw, rows, mod_l, w_exp_gate[l].astype(BF16), w_exp_up[l].astype(BF16),
                    w_exp_down[l].astype(BF16), g_post_ffn[l].reshape(1, d), n_ctx)

    return rows[n_ctx:].reshape(1, seq, d)
```

```python
import functools
import math

import jax
import jax.numpy as jnp
import numpy as np
from jax import lax
from jax.experimental import pallas as pl
from jax.experimental.pallas import tpu as pltpu

F32 = jnp.float32
BF16 = jnp.bfloat16

GRID_W = 64
ROPE_THETA = 10000.0
NORM_EPS = 1e-6
N_MOD = 6

MLA_HEADS = 8
MLA_NOPE = 64
MLA_ROPE = 32
MLA_V = 64
MLA_Q_RANK = 256
MLA_KV_RANK = 128
MLA_SCALE = (MLA_NOPE + MLA_ROPE) ** -0.5
MLA_HEAD_PAD = 128

SSM_GROUP = 16
SSM_GROUPS = 32
SSM_STATE = 64
SSM_W = SSM_GROUP * SSM_GROUPS
SSM_DIRS = 2
SSM_CHUNK = 16
SSM_SUPER = 16
SSM_CW = SSM_CHUNK * SSM_GROUP

GQA_HEADS = 8
GQA_KV_HEADS = 2
GQA_HD = 64
GQA_W = GQA_HEADS * GQA_HD
GQA_SCALE = GQA_HD ** -0.5
LOG2E = math.log2(math.e)

N_GROUPS = 4
EXPERTS_PER_GROUP = 4
N_EXPERTS = N_GROUPS * EXPERTS_PER_GROUP
D_EXPERT = 512

ROW_TILE = 256
LANES = 128
V_ROWS = 80
VMEM_LIMIT = 56 * 1024 * 1024


def _cparams(*sem):
    return pltpu.CompilerParams(dimension_semantics=sem, vmem_limit_bytes=VMEM_LIMIT)


def _dot(a, b):
    return jnp.dot(a, b, preferred_element_type=F32)


def _dot_f32(a, b):
    a_hi = a.astype(BF16)
    a_lo = (a - a_hi.astype(F32)).astype(BF16)
    return _dot(a_hi, b) + _dot(a_lo, b)


def _rms(x, g):
    return x * lax.rsqrt(jnp.mean(x * x, axis=-1, keepdims=True) + NORM_EPS) * g


def _sigmoid(x):
    return 1.0 / (1.0 + jnp.exp(-x))


def _mod_kernel(cond_ref, w_ref, b_ref, o_ref):
    a = cond_ref[...]
    s = a * _sigmoid(a)
    w = w_ref[...]
    s_hi = s.astype(BF16)
    s_lo = (s - s_hi.astype(F32)).astype(BF16)
    w_hi = w.astype(BF16)
    w_lo = (w - w_hi.astype(F32)).astype(BF16)
    acc = _dot(s_hi, w_hi) + _dot(s_lo, w_hi) + _dot(s_hi, w_lo)
    o_ref[...] = acc + b_ref[...]


def _modulation(cond8, w_mod, b_mod):
    nl, d, n = w_mod.shape
    tn = 1536
    return pl.pallas_call(
        _mod_kernel,
        grid=(nl, n // tn),
        in_specs=[
            pl.BlockSpec((8, d), lambda l, j: (0, 0)),
            pl.BlockSpec((None, d, tn), lambda l, j: (l, 0, j)),
            pl.BlockSpec((None, 1, tn), lambda l, j: (l, 0, j)),
        ],
        out_specs=pl.BlockSpec((None, 8, tn), lambda l, j: (l, 0, j)),
        out_shape=jax.ShapeDtypeStruct((nl, 8, n), F32),
        compiler_params=_cparams("parallel", "parallel"),
        name="modulation",
    )(cond8, w_mod, b_mod.reshape(nl, 1, n))


def _mod_row(mod_ref, idx):
    m = mod_ref[...]
    d = m.shape[1] // N_MOD
    lat = m[0:1, idx * d:(idx + 1) * d]
    ctx = m[1:2, idx * d:(idx + 1) * d]
    return jnp.where(pl.program_id(0) == 0, ctx, lat)


_SEG = {}
_off = 0
for _name, _w in (("cq", 256), ("ckv", 128), ("kr", 128), ("krs", 128), ("ssm", 512), ("gq", 512),
                  ("gqs", 512), ("gk", 128), ("gks", 128), ("gv", 128), ("gate", 3072)):
    _SEG[_name] = (_off, _off + _w)
    _off += _w
W_CAT = _off


def _premix_kernel(x_ref, mod_ref, gpre_ref, w_ref, wuq_ref, wuqs_ref, wuk_ref, wuv_ref,
                   gcq_ref, gckv_ref, gq_ref, gqs_ref, gk_ref, gks_ref, ones_ref,
                   cm_ref, sm_ref, cg_ref, sg_ref,
                   qm_ref, km_ref, vm_ref, qg_ref, kg_ref, vg_ref, u_ref, gate_ref):
    x = x_ref[...]
    h = _rms(x, gpre_ref[...]) * (1.0 + _mod_row(mod_ref, 1)) + _mod_row(mod_ref, 0)
    hb = h.astype(BF16)

    def proj(name):
        a, b = _SEG[name]
        return _dot(hb, w_ref[:, a:b])

    cm = cm_ref[...]
    sm = sm_ref[...]
    cg = cg_ref[...]
    sg = sg_ref[...]

    cqn = _rms(proj("cq"), gcq_ref[...]).astype(BF16)
    ckvn = _rms(proj("ckv"), gckv_ref[...]).astype(BF16)
    q = _dot(cqn, wuq_ref[...])
    qs = _dot(cqn, wuqs_ref[...])
    kn = _dot(ckvn, wuk_ref[...])
    kr = proj("kr") * cm + proj("krs") * sm
    for hd in range(MLA_HEADS):
        sl = slice(hd * MLA_HEAD_PAD, (hd + 1) * MLA_HEAD_PAD)
        qm_ref[:, sl] = ((q[:, sl] * cm + qs[:, sl] * sm) * (MLA_SCALE * LOG2E)).astype(BF16)
        km_ref[:, sl] = (kn[:, sl] + kr).astype(BF16)
    vm_ref[...] = _dot(ckvn, wuv_ref[...]).astype(BF16)

    ones = ones_ref[...]

    def head_rms_scale(v, width):
        ms = _dot_f32(v * v, ones[:width, :width]) * (1.0 / GQA_HD)
        return lax.rsqrt(ms + NORM_EPS)

    gq = proj("gq")
    gqs = proj("gqs")
    rq = head_rms_scale(gq, GQA_W)
    cg4 = jnp.concatenate([cg] * (GQA_W // LANES), axis=1)
    sg4 = jnp.concatenate([sg] * (GQA_W // LANES), axis=1)
    qg = rq * (gq * gq_ref[...] * cg4 + gqs * gqs_ref[...] * sg4)
    qg_ref[...] = (qg * (GQA_SCALE * LOG2E)).astype(BF16)
    gk = proj("gk")
    gks = proj("gks")
    rk = head_rms_scale(gk, GQA_KV_HEADS * GQA_HD)
    kg_ref[...] = (rk * (gk * gk_ref[...] * cg + gks * gks_ref[...] * sg)).astype(BF16)
    vg_ref[...] = proj("gv").astype(BF16)

    u_ref[...] = proj("ssm").astype(BF16)
    gate_ref[...] = _sigmoid(proj("gate")).astype(BF16)


def _premix(rows, mod_l, g_pre, wts, tabs):
    r, d = rows.shape
    nt = r // ROW_TILE
    row_spec = lambda w: pl.BlockSpec((ROW_TILE, w), lambda i: (i, 0))
    full = lambda a: pl.BlockSpec(a.shape, lambda i: (0,) * a.ndim)
    consts = [g_pre, wts["w_cat"], wts["wuq"], wts["wuqs"], wts["wuk"], wts["wuv"],
              wts["g_cq"], wts["g_ckv"], wts["g_q"], wts["g_qs"], wts["g_k"], wts["g_ks"], wts["ones"]]
    out_w = [MLA_HEADS * MLA_HEAD_PAD, MLA_HEADS * MLA_HEAD_PAD, MLA_HEADS * MLA_V,
             GQA_W, GQA_KV_HEADS * GQA_HD, GQA_KV_HEADS * GQA_HD, SSM_W, 3 * d]
    return pl.pallas_call(
        _premix_kernel,
        grid=(nt,),
        in_specs=[row_spec(d), full(mod_l)] + [full(a) for a in consts] + [row_spec(LANES)] * 4,
        out_specs=[row_spec(w) for w in out_w],
        out_shape=[jax.ShapeDtypeStruct((r, w), BF16) for w in out_w],
        compiler_params=_cparams("parallel"),
        name="premix",
    )(rows, mod_l, *consts, tabs["cm"], tabs["sm"], tabs["cg"], tabs["sg"])


def _attn_kernel(q_ref, k_ref, v_ref, o_ref, s_ref, *, tk, nsub, n_ctx, n_iter, dv, hb, kv_shared, ahead):
    n_items = hb * nsub
    tq = q_ref.shape[2]

    def scores(h, off, size):
        return _dot(k_ref[0 if kv_shared else h, pl.ds(off, size), :], q_ref[h])

    def update(h, m, acc, s, off, size):
        m_new = jnp.maximum(m, jnp.max(s, axis=0, keepdims=True))
        p = jnp.exp2(s - m_new).astype(BF16)
        pv = _dot(v_ref[0 if kv_shared else h, :, pl.ds(off, size)], p)
        return m_new, jnp.exp2(m - m_new) * acc + pv

    def issue(item, base):
        h, sub = item % hb, item // hb
        s_ref[item] = scores(h, pl.multiple_of(base + sub * tk, LANES), tk)

    m0 = jnp.full((1, tq), -1e30, F32)
    acc0 = jnp.zeros((V_ROWS, tq), F32)
    s_ctx = [scores(h, 0, n_ctx) for h in range(hb)]
    for item in range(ahead):
        issue(item, n_ctx)
    carry = tuple(update(h, m0, acc0, s_ctx[h], 0, n_ctx) for h in range(hb))

    def body(j, carry):
        carry = list(carry)
        base = n_ctx + j * (nsub * tk)
        base_next = n_ctx + jnp.minimum(j + 1, n_iter - 1) * (nsub * tk)
        for item in range(n_items):
            nxt = item + ahead
            if nxt < n_items:
                issue(nxt, base)
            else:
                issue(nxt - n_items, base_next)
            h, sub = item % hb, item // hb
            off = pl.multiple_of(base + sub * tk, LANES)
            carry[h] = update(h, carry[h][0], carry[h][1], s_ref[item], off, tk)
        return tuple(carry)

    carry = lax.fori_loop(0, jnp.where(pl.program_id(1) == 0, 0, n_iter), body, carry)
    for h in range(hb):
        acc = carry[h][1]
        o_ref[h] = acc[:dv] / acc[dv:dv + 1]


def _attention(qt, k, vt, *, hb, nsub, tq=256, tk=512, ahead=2):
    nh, dq, r = qt.shape
    nk = k.shape[0]
    kv_shared = nk != nh
    assert (nh // nk == hb) if kv_shared else (nh % hb == 0)
    kb = 1 if kv_shared else hb
    dv = V_ROWS - 16
    assert (r - ROW_TILE) % (tk * nsub) == 0 and ahead < hb * nsub
    kern = functools.partial(_attn_kernel, tk=tk, nsub=nsub, n_ctx=ROW_TILE, n_iter=(r - ROW_TILE) // (tk * nsub),
                             dv=dv, hb=hb, kv_shared=kv_shared, ahead=ahead)
    return pl.pallas_call(
        kern,
        grid=(nh // hb, r // tq),
        in_specs=[
            pl.BlockSpec((hb, dq, tq), lambda g, i: (g, 0, i)),
            pl.BlockSpec((kb, r, dq), lambda g, i: (g, 0, 0)),
            pl.BlockSpec((kb, V_ROWS, r), lambda g, i: (g, 0, 0)),
        ],
        out_specs=pl.BlockSpec((hb, dv, tq), lambda g, i: (g, 0, i)),
        out_shape=jax.ShapeDtypeStruct((nh, dv, r), F32),
        scratch_shapes=[pltpu.VMEM((hb * nsub, tk, tq), F32)],
        compiler_params=_cparams("parallel", "arbitrary"),
        name="attention",
    )(qt, k, vt)


def _to_vt(v_rows, nheads):
    r = v_rows.shape[0]
    vt = jnp.transpose(v_rows.reshape(r, nheads, 64), (1, 2, 0))
    pad = jnp.zeros((nheads, V_ROWS - 64, r), v_rows.dtype).at[:, 0, :].set(1.0)
    return jnp.concatenate([vt, pad], axis=1)


def _s5_kernel(u_ref, toe_ref, bre_ref, bim_ref, cre_ref, cim_ref, apr_ref, api_ref, y_ref,
               xcr_s, xci_s, xir_s, xii_s, *, gb, sp):
    nb = SSM_SUPER
    for g in range(gb):
        u = u_ref[g]
        xcr_s[...] = _dot(u, bre_ref[g])
        xci_s[...] = _dot(u, bim_ref[g])
        apr = apr_ref[g]
        api = api_ref[g]
        a1r, a1i = apr[1:2], api[1:2]
        lr = jnp.zeros((sp, SSM_STATE), F32)
        li = jnp.zeros((sp, SSM_STATE), F32)
        for b in range(nb):
            xir_s[b * sp:(b + 1) * sp, :] = lr
            xii_s[b * sp:(b + 1) * sp, :] = li
            cr = xcr_s[b * sp:(b + 1) * sp, :]
            ci = xci_s[b * sp:(b + 1) * sp, :]
            lr, li = a1r * lr - a1i * li + cr, a1r * li + a1i * lr + ci
        anr, ani = apr[nb:nb + 1], api[nb:nb + 1]
        sr = jnp.zeros((1, SSM_STATE), F32)
        si = jnp.zeros((1, SSM_STATE), F32)
        rows_r, rows_i = [], []
        for s in range(sp):
            rows_r.append(sr)
            rows_i.append(si)
            sr, si = anr * sr - ani * si + lr[s:s + 1], anr * si + ani * sr + li[s:s + 1]
        sin_r = jnp.concatenate(rows_r, axis=0)
        sin_i = jnp.concatenate(rows_i, axis=0)
        for b in range(nb):
            pr, pi = apr[b:b + 1], api[b:b + 1]
            xir_s[b * sp:(b + 1) * sp, :] += pr * sin_r - pi * sin_i
            xii_s[b * sp:(b + 1) * sp, :] += pr * sin_i + pi * sin_r
        y = _dot(u, toe_ref[g])
        y += _dot(xir_s[...].astype(BF16), cre_ref[g]) + _dot(xii_s[...].astype(BF16), cim_ref[g])
        y_ref[g] = y


def _s5_scan(ug, mats, *, gb=4):
    nd, ng, rows, cw = ug.shape
    sp = rows // SSM_SUPER
    p = SSM_STATE
    blk = lambda *tail: pl.BlockSpec((None, gb) + tail, lambda d, g: (d, g) + (0,) * len(tail))
    kern = functools.partial(_s5_kernel, gb=gb, sp=sp)
    return pl.pallas_call(
        kern,
        grid=(nd, ng // gb),
        in_specs=[blk(rows, cw), blk(cw, cw), blk(cw, p), blk(cw, p), blk(p, cw), blk(p, cw),
                  blk(24, p), blk(24, p)],
        out_specs=blk(rows, cw),
        out_shape=jax.ShapeDtypeStruct((nd, ng, rows, cw), F32),
        scratch_shapes=[pltpu.VMEM((rows, p), F32)] * 4,
        compiler_params=_cparams("parallel", "arbitrary"),
        name="s5_scan",
    )(ug, mats["toe"], mats["bre"], mats["bim"], mats["cre"], mats["cim"], mats["apr"], mats["api"])


def _s5_matrices(a_re, a_im, log_dt, b_re, b_im, c_re, c_im, d_skip):
    t = SSM_CHUNK
    a_re = a_re.astype(F32)
    a_im = a_im.astype(F32)
    dt = jnp.exp(log_dt.astype(F32))[..., None]
    den = a_re * a_re + a_im * a_im

    def lpow(k):
        kf = jnp.asarray(k, F32)
        mag = jnp.exp(a_re * dt * kf)
        return mag * jnp.cos(a_im * dt * kf), mag * jnp.sin(a_im * dt * kf)

    lr, li = lpow(1.0)
    cr = ((lr - 1.0) * a_re + li * a_im) / den
    ci = (li * a_re - (lr - 1.0) * a_im) / den
    bbr = cr[..., None] * b_re - ci[..., None] * b_im
    bbi = cr[..., None] * b_im + ci[..., None] * b_re
    c_re = c_re.astype(F32)
    c_im = c_im.astype(F32)

    ks = jnp.arange(t + 1, dtype=F32)[:, None, None, None]
    pr, pi = lpow(ks)
    lbr = pr[:t, ..., None] * bbr - pi[:t, ..., None] * bbi
    lbi = pr[:t, ..., None] * bbi + pi[:t, ..., None] * bbr
    kern = (jnp.einsum('dgop,kdgpc->dgkoc', c_re, lbr, precision='highest')
            - jnp.einsum('dgop,kdgpc->dgkoc', c_im, lbi, precision='highest'))
    skip = d_skip.astype(F32).reshape(SSM_GROUPS, SSM_GROUP)
    eye = jnp.eye(SSM_GROUP, dtype=F32)
    kern = kern.at[0, :, 0].add(skip[:, :, None] * eye)
    lag = jnp.arange(t)[None, :] - jnp.arange(t)[:, None]
    blocks = jnp.take(kern, jnp.clip(lag, 0, t - 1).reshape(-1), axis=2)
    blocks = blocks.reshape(SSM_DIRS, SSM_GROUPS, t, t, SSM_GROUP, SSM_GROUP)
    blocks = jnp.where((lag >= 0)[None, None, :, :, None, None], blocks, 0.0)
    toe = jnp.transpose(blocks, (0, 1, 2, 5, 3, 4)).reshape(SSM_DIRS, SSM_GROUPS, SSM_CW, SSM_CW)
    bpr = jnp.transpose(lbr[::-1], (1, 2, 0, 4, 3)).reshape(SSM_DIRS, SSM_GROUPS, SSM_CW, SSM_STATE)
    bpi = jnp.transpose(lbi[::-1], (1, 2, 0, 4, 3)).reshape(SSM_DIRS, SSM_GROUPS, SSM_CW, SSM_STATE)
    p1r = pr[1:, :, :, None, :]
    p1i = pi[1:, :, :, None, :]
    rd_r = c_re * p1r - c_im * p1i
    rd_i = -(c_re * p1i + c_im * p1r)
    cpr = jnp.transpose(rd_r, (1, 2, 4, 0, 3)).reshape(SSM_DIRS, SSM_GROUPS, SSM_STATE, SSM_CW)
    cpi = jnp.transpose(rd_i, (1, 2, 4, 0, 3)).reshape(SSM_DIRS, SSM_GROUPS, SSM_STATE, SSM_CW)
    kk = (jnp.arange(24, dtype=F32) * t)[:, None, None, None]
    kk = jnp.where(kk > SSM_SUPER * t, 0.0, kk)
    qr, qi = lpow(kk)
    apr = jnp.transpose(qr, (1, 2, 0, 3))
    api = jnp.transpose(qi, (1, 2, 0, 3))
    return dict(toe=toe.astype(BF16), bre=bpr.astype(BF16), bim=bpi.astype(BF16),
                cre=cpr.astype(BF16), cim=cpi.astype(BF16), apr=apr, api=api)


def _s5_regroup(u_rows, n_ctx):
    r = u_rows.shape[0]
    t, nb = SSM_CHUNK, SSM_SUPER
    ns = r // (t * nb)
    sp = -(-ns // 8) * 8
    rev = jnp.concatenate([u_rows[:n_ctx][::-1], u_rows[n_ctx:][::-1]], axis=0)
    both = jnp.stack([u_rows, rev])
    x = both.reshape(SSM_DIRS, ns, nb, t, SSM_GROUPS, SSM_GROUP)
    x = jnp.transpose(x, (0, 4, 2, 1, 3, 5))
    x = jnp.pad(x, ((0, 0), (0, 0), (0, 0), (0, sp - ns), (0, 0), (0, 0)))
    return x.reshape(SSM_DIRS, SSM_GROUPS, nb * sp, SSM_CW), ns, sp


def _s5_ungroup(y, ns, sp, n_ctx):
    t, nb = SSM_CHUNK, SSM_SUPER
    y = y.reshape(SSM_DIRS, SSM_GROUPS, nb, sp, t, SSM_GROUP)[:, :, :, :ns]
    y = jnp.transpose(y, (0, 3, 2, 4, 1, 5)).reshape(SSM_DIRS, ns * nb * t, SSM_W)
    yr = jnp.concatenate([y[1, :n_ctx][::-1], y[1, n_ctx:][::-1]], axis=0)
    return y[0], yr


def _merge_kernel(x_ref, mod_ref, om_ref, og_ref, yf_ref, yr_ref, gate_ref,
                  wglu_ref, bglu_ref, wbm_ref, wbs_ref, wbg_ref, wout_ref,
                  gpost_ref, gffn_ref, wrt_hi_ref, wrt_lo_ref, brt_ref,
                  xo_ref, h_ref, cw_ref):
    d = x_ref.shape[1]
    y = yf_ref[...] + yr_ref[...]
    y1 = 0.5 * y * (1.0 + jnp.tanh(math.sqrt(2.0 / math.pi) * (y + 0.044715 * (y * y * y))))
    ssm = y1 * _sigmoid(_dot(y1.astype(BF16), wglu_ref[...]) + bglu_ref[...])
    gate = gate_ref[...]
    m = (gate[:, 0:d].astype(F32) * _dot(om_ref[...], wbm_ref[...])
         + gate[:, d:2 * d].astype(F32) * _dot(ssm.astype(BF16), wbs_ref[...])
         + gate[:, 2 * d:3 * d].astype(F32) * _dot(og_ref[...], wbg_ref[...]))
    ymix = _dot(m.astype(BF16), wout_ref[...])
    x = x_ref[...] + _mod_row(mod_ref, 2) * _rms(ymix, gpost_ref[...])
    xo_ref[...] = x
    h = _rms(x, gffn_ref[...]) * (1.0 + _mod_row(mod_ref, 4)) + _mod_row(mod_ref, 3)
    h_ref[...] = h.astype(BF16)

    h_hi = h.astype(BF16)
    h_lo = (h - h_hi.astype(F32)).astype(BF16)
    lg = (_dot(h_hi, wrt_hi_ref[...]) + _dot(h_lo, wrt_hi_ref[...]) + _dot(h_hi, wrt_lo_ref[...])
          + brt_ref[...])
    lane = lax.broadcasted_iota(jnp.int32, lg.shape, 1)
    neg = jnp.float32(-1e30)
    is_g = lane < N_GROUPS
    gl = jnp.where(is_g, lg, neg)
    gmax = jnp.max(gl, axis=-1, keepdims=True)
    gsel = jnp.min(jnp.where(is_g & (gl == gmax), lane, LANES), axis=-1, keepdims=True)
    pg = 1.0 / jnp.sum(jnp.where(is_g, jnp.exp(gl - gmax), 0.0), axis=-1, keepdims=True)
    lo = N_GROUPS + gsel * EXPERTS_PER_GROUP
    in_grp = (lane >= lo) & (lane < lo + EXPERTS_PER_GROUP)
    el = jnp.where(in_grp, lg, neg)
    m1 = jnp.max(el, axis=-1, keepdims=True)
    i1 = jnp.min(jnp.where(in_grp & (el == m1), lane, LANES), axis=-1, keepdims=True)
    el2 = jnp.where(lane == i1, neg, el)
    m2 = jnp.max(el2, axis=-1, keepdims=True)
    i2 = jnp.min(jnp.where(in_grp & (el2 == m2), lane, LANES), axis=-1, keepdims=True)
    e2 = jnp.exp(m2 - m1)
    t1 = pg / (1.0 + e2)
    cw_ref[...] = jnp.where(lane == i1, t1, 0.0) + jnp.where(lane == i2, t1 * e2, 0.0)


def _merge(rows, mod_l, o_mla, o_gqa, y_f, y_r, gates, wts):
    r, d = rows.shape
    nt = r // ROW_TILE
    row_spec = lambda w: pl.BlockSpec((ROW_TILE, w), lambda i: (i, 0))
    full = lambda a: pl.BlockSpec(a.shape, lambda i: (0,) * a.ndim)
    consts = [wts["w_glu"], wts["b_glu"], wts["w_br_mla"], wts["w_br_ssm"], wts["w_br_gqa"], wts["w_out"],
              wts["g_post_mix"], wts["g_pre_ffn"], wts["w_rt_hi"], wts["w_rt_lo"], wts["b_rt"]]
    return pl.pallas_call(
        _merge_kernel,
        grid=(nt,),
        in_specs=[row_spec(d), full(mod_l), row_spec(512), row_spec(512), row_spec(512), row_spec(512),
                  row_spec(3 * d)] + [full(a) for a in consts],
        out_specs=[row_spec(d), row_spec(d), row_spec(LANES)],
        out_shape=[jax.ShapeDtypeStruct((r, d), F32), jax.ShapeDtypeStruct((r, d), BF16),
                   jax.ShapeDtypeStruct((r, LANES), F32)],
        compiler_params=_cparams("parallel"),
        name="merge",
    )(rows, mod_l, o_mla, o_gqa, y_f, y_r, gates, *consts)


def _moe_kernel(h_ref, cw_ref, x_ref, mod_ref, wg_ref, wu_ref, wd_ref, gpost_ref, o_ref, acc_ref, *, n_ctx):
    e = pl.program_id(1)

    @pl.when(e == 0)
    def _():
        acc_ref[...] = jnp.zeros_like(acc_ref)

    h = h_ref[...]
    a = _dot(h, wg_ref[...])
    hid = (a * _sigmoid(a)) * _dot(h, wu_ref[...])
    lane = lax.broadcasted_iota(jnp.int32, cw_ref.shape, 1)
    w_e = jnp.sum(jnp.where(lane == e + N_GROUPS, cw_ref[...], 0.0), axis=-1, keepdims=True)
    acc_ref[...] += w_e * _dot(hid.astype(BF16), wd_ref[...])

    @pl.when(e == pl.num_programs(1) - 1)
    def _():
        m = mod_ref[...]
        d = m.shape[1] // N_MOD
        tm = o_ref.shape[0]
        row = pl.program_id(0) * tm + lax.broadcasted_iota(jnp.int32, (tm, 1), 0)
        ga = jnp.where(row < n_ctx, m[1:2, 5 * d:6 * d], m[0:1, 5 * d:6 * d])
        o_ref[...] = x_ref[...] + ga * _rms(acc_ref[...], gpost_ref[...])


def _moe(h, cw, rows, mod_l, wg, wu, wd, g_post, n_ctx):
    r, d = rows.shape
    tm = 1280 if r % 1280 == 0 else ROW_TILE
    ne, _, de = wg.shape
    kern = functools.partial(_moe_kernel, n_ctx=n_ctx)
    return pl.pallas_call(
        kern,
        grid=(r // tm, ne),
        in_specs=[
            pl.BlockSpec((tm, d), lambda i, e: (i, 0)),
            pl.BlockSpec((tm, LANES), lambda i, e: (i, 0)),
            pl.BlockSpec((tm, d), lambda i, e: (i, 0)),
            pl.BlockSpec(mod_l.shape, lambda i, e: (0, 0)),
            pl.BlockSpec((None, d, de), lambda i, e: (e, 0, 0)),
            pl.BlockSpec((None, d, de), lambda i, e: (e, 0, 0)),
            pl.BlockSpec((None, de, d), lambda i, e: (e, 0, 0)),
            pl.BlockSpec(g_post.shape, lambda i, e: (0, 0)),
        ],
        out_specs=pl.BlockSpec((tm, d), lambda i, e: (i, 0)),
        out_shape=jax.ShapeDtypeStruct((r, d), F32),
        scratch_shapes=[pltpu.VMEM((tm, d), F32)],
        compiler_params=_cparams("parallel", "arbitrary"),
        name="moe",
    )(h, cw, rows, mod_l, wg, wu, wd, g_post)


def _rope_tables(seq, n_ctx):
    tpos = jnp.arange(seq, dtype=jnp.int32)
    rows = (tpos // GRID_W).astype(F32)
    cols = (tpos % GRID_W).astype(F32)

    def pattern(width):
        half = width // 4
        freqs = ROPE_THETA ** (-jnp.arange(half, dtype=F32) / half)
        ar = rows[:, None] * freqs[None, :]
        ac = cols[:, None] * freqs[None, :]
        c = jnp.concatenate([jnp.cos(ar), jnp.cos(ar), jnp.cos(ac), jnp.cos(ac)], axis=1)
        s = jnp.concatenate([-jnp.sin(ar), jnp.sin(ar), -jnp.sin(ac), jnp.sin(ac)], axis=1)
        return c, s

    c32, s32 = pattern(MLA_ROPE)
    c64, s64 = pattern(GQA_HD)
    one = jnp.ones((seq, 1), F32)
    cm = jnp.concatenate([one * jnp.ones((1, MLA_NOPE)), c32, one * jnp.ones((1, 32))], axis=1)
    sm = jnp.concatenate([jnp.zeros((seq, MLA_NOPE)), s32, jnp.zeros((seq, 32))], axis=1)
    cg = jnp.concatenate([c64, c64], axis=1)
    sg = jnp.concatenate([s64, s64], axis=1)
    ctx_c = jnp.ones((n_ctx, LANES), F32)
    ctx_s = jnp.zeros((n_ctx, LANES), F32)
    cat = lambda a, b: jnp.concatenate([a, b], axis=0)
    return dict(cm=cat(ctx_c, cm), sm=cat(ctx_s, sm), cg=cat(ctx_c, cg), sg=cat(ctx_s, sg))


def _swap_perm(width):
    q = width // 4
    return np.concatenate([np.arange(q, 2 * q), np.arange(0, q), np.arange(3 * q, 4 * q), np.arange(2 * q, 3 * q)])


def _layer_weights(l, p):
    d = p["w_in"].shape[1]
    w_in = p["w_in"][l]
    o = 0
    cq = w_in[:, o:o + MLA_Q_RANK]; o += MLA_Q_RANK
    ckv = w_in[:, o:o + MLA_KV_RANK]; o += MLA_KV_RANK
    kr = w_in[:, o:o + MLA_ROPE]; o += MLA_ROPE
    ssm = w_in[:, o:o + SSM_W]; o += SSM_W
    gq = w_in[:, o:o + GQA_W]; o += GQA_W
    gk = w_in[:, o:o + GQA_KV_HEADS * GQA_HD]; o += GQA_KV_HEADS * GQA_HD
    gv = w_in[:, o:o + GQA_KV_HEADS * GQA_HD]; o += GQA_KV_HEADS * GQA_HD
    gate = w_in[:, o:]
    p32 = _swap_perm(MLA_ROPE)
    p64 = _swap_perm(GQA_HD)
    perm_q = np.concatenate([p64 + GQA_HD * h for h in range(GQA_HEADS)])
    perm_k = np.concatenate([p64 + GQA_HD * h for h in range(GQA_KV_HEADS)])
    z = lambda n: jnp.zeros((d, n), F32)
    kr128 = jnp.concatenate([z(MLA_NOPE), kr, z(32)], axis=1)
    krs128 = jnp.concatenate([z(MLA_NOPE), kr[:, p32], z(32)], axis=1)
    w_cat = jnp.concatenate([cq, ckv, kr128, krs128, ssm, gq, gq[:, perm_q], gk, gk[:, perm_k], gv, gate],
                            axis=1).astype(BF16)

    w_uq = p["w_uq"][l].reshape(MLA_Q_RANK, MLA_HEADS, MLA_NOPE + MLA_ROPE)
    zq = jnp.zeros((MLA_Q_RANK, MLA_HEADS, 32), F32)
    wuq = jnp.concatenate([w_uq, zq], axis=2).reshape(MLA_Q_RANK, -1)
    wuqs = jnp.concatenate([jnp.zeros((MLA_Q_RANK, MLA_HEADS, MLA_NOPE), F32),
                            w_uq[:, :, MLA_NOPE:][:, :, p32], zq], axis=2).reshape(MLA_Q_RANK, -1)
    w_ukv = p["w_ukv"][l].reshape(MLA_KV_RANK, MLA_HEADS, MLA_NOPE + MLA_V)
    wuk = jnp.concatenate([w_ukv[:, :, :MLA_NOPE], jnp.zeros((MLA_KV_RANK, MLA_HEADS, 64), F32)],
                          axis=2).reshape(MLA_KV_RANK, -1)
    wuv = w_ukv[:, :, MLA_NOPE:].reshape(MLA_KV_RANK, -1)

    g_qn = p["g_qn"][l]
    g_kn = p["g_kn"][l]
    blk = np.arange(GQA_W) // GQA_HD
    ones = jnp.asarray((blk[:, None] == blk[None, :]).astype(np.float32), BF16)

    w_rt = jnp.zeros((d, LANES), F32)
    w_rt = w_rt.at[:, :N_GROUPS].set(p["w_group"][l]).at[:, N_GROUPS:N_GROUPS + N_EXPERTS].set(p["w_router"][l])
    w_rt_hi = w_rt.astype(BF16)
    w_rt_lo = (w_rt - w_rt_hi.astype(F32)).astype(BF16)
    b_rt = jnp.zeros((1, LANES), F32)
    b_rt = b_rt.at[0, :N_GROUPS].set(p["b_group"][l]).at[0, N_GROUPS:N_GROUPS + N_EXPERTS].set(p["b_router"][l])

    row = lambda v: v.reshape(1, -1).astype(F32)
    return dict(
        w_cat=w_cat, wuq=wuq.astype(BF16), wuqs=wuqs.astype(BF16), wuk=wuk.astype(BF16), wuv=wuv.astype(BF16),
        g_cq=row(p["g_cq"][l]), g_ckv=row(p["g_ckv"][l]),
        g_q=row(jnp.tile(g_qn, GQA_HEADS)), g_qs=row(jnp.tile(g_qn[p64], GQA_HEADS)),
        g_k=row(jnp.tile(g_kn, GQA_KV_HEADS)), g_ks=row(jnp.tile(g_kn[p64], GQA_KV_HEADS)),
        ones=ones,
        w_glu=p["w_glu"][l].astype(BF16), b_glu=row(p["b_glu"][l]),
        w_br_mla=p["w_br_mla"][l].astype(BF16), w_br_ssm=p["w_br_ssm"][l].astype(BF16),
        w_br_gqa=p["w_br_gqa"][l].astype(BF16), w_out=p["w_out"][l].astype(BF16),
        g_post_mix=row(p["g_post_mix"][l]), g_pre_ffn=row(p["g_pre_ffn"][l]),
        w_rt_hi=w_rt_hi, w_rt_lo=w_rt_lo, b_rt=b_rt,
    )


def kernel(x, c, ctx, c_ctx, w_mod, b_mod, g_pre_mix, g_post_mix, g_pre_ffn, g_post_ffn, w_in, g_cq, g_ckv, w_uq, w_ukv, g_qn, g_kn, ssm_a_re, ssm_a_im, ssm_log_dt, ssm_b_re, ssm_b_im, ssm_c_re, ssm_c_im, ssm_d, w_glu, b_glu, w_br_mla, w_br_ssm, w_br_gqa, w_out, w_group, b_group, w_router, b_router, w_exp_gate, w_exp_up, w_exp_down):
    assert x.shape[0] == 1 and ctx.shape[0] == 1
    seq, d = x.shape[1], x.shape[2]
    n_ctx = ctx.shape[1]
    assert n_ctx == ROW_TILE and seq % (SSM_CHUNK * SSM_SUPER) == 0
    depth = w_in.shape[0]
    params = dict(w_in=w_in, g_cq=g_cq, g_ckv=g_ckv, w_uq=w_uq, w_ukv=w_ukv, g_qn=g_qn, g_kn=g_kn,
                  w_glu=w_glu, b_glu=b_glu, w_br_mla=w_br_mla, w_br_ssm=w_br_ssm, w_br_gqa=w_br_gqa,
                  w_out=w_out, g_post_mix=g_post_mix, g_pre_ffn=g_pre_ffn,
                  w_group=w_group, b_group=b_group, w_router=w_router, b_router=b_router)

    cond8 = jnp.zeros((8, d), F32).at[0].set(c[0]).at[1].set(c_ctx)
    mods = _modulation(cond8, w_mod, b_mod)
    tabs = _rope_tables(seq, n_ctx)
    rows = jnp.concatenate([ctx[0], x[0]], axis=0)
    r = rows.shape[0]

    for l in range(depth):
        wts = _layer_weights(l, params)
        mod_l = mods[l]
        qm, km, vm, qg, kg, vg, u, gates = _premix(rows, mod_l, g_pre_mix[l].reshape(1, d), wts, tabs)

        o_mla = _attention(jnp.transpose(qm.reshape(r, MLA_HEADS, MLA_HEAD_PAD), (1, 2, 0)),
                           jnp.transpose(km.reshape(r, MLA_HEADS, MLA_HEAD_PAD), (1, 0, 2)),
                           _to_vt(vm, MLA_HEADS), hb=2, nsub=4)
        o_gqa = _attention(jnp.transpose(qg.reshape(r, GQA_HEADS, GQA_HD), (1, 2, 0)),
                           jnp.transpose(kg.reshape(r, GQA_KV_HEADS, GQA_HD), (1, 0, 2)),
                           _to_vt(vg, GQA_KV_HEADS), hb=GQA_HEADS // GQA_KV_HEADS, nsub=2)
        o_mla = jnp.transpose(o_mla, (2, 0, 1)).reshape(r, MLA_HEADS * MLA_V).astype(BF16)
        o_gqa = jnp.transpose(o_gqa, (2, 0, 1)).reshape(r, GQA_W).astype(BF16)

        mats = _s5_matrices(ssm_a_re[l], ssm_a_im[l], ssm_log_dt[l], ssm_b_re[l], ssm_b_im[l],
                            ssm_c_re[l], ssm_c_im[l], ssm_d[l])
        ug, ns, sp = _s5_regroup(u, n_ctx)
        y_f, y_r = _s5_ungroup(_s5_scan(ug, mats), ns, sp, n_ctx)

        rows, h, cw = _merge(rows, mod_l, o_mla, o_gqa, y_f, y_r, gates, wts)
        rows = _moe(h, cw, rows, mod_l, w_exp_gate[l].astype(BF16), w_exp_up[l].astype(BF16),
                    w_exp_down[l].astype(BF16), g_post_ffn[l].reshape(1, d), n_ctx)

    return rows[n_ctx:].reshape(1, seq, d)
```

```python
import functools
import math

import jax
import jax.numpy as jnp
import numpy as np
from jax import lax
from jax.experimental import pallas as pl
from jax.experimental.pallas import tpu as pltpu

F32 = jnp.float32
BF16 = jnp.bfloat16

GRID_W = 64
ROPE_THETA = 10000.0
NORM_EPS = 1e-6
N_MOD = 6

MLA_HEADS = 8
MLA_NOPE = 64
MLA_ROPE = 32
MLA_V = 64
MLA_Q_RANK = 256
MLA_KV_RANK = 128
MLA_SCALE = (MLA_NOPE + MLA_ROPE) ** -0.5
MLA_HEAD_PAD = 128

SSM_GROUP = 16
SSM_GROUPS = 32
SSM_STATE = 64
SSM_W = SSM_GROUP * SSM_GROUPS
SSM_DIRS = 2
SSM_CHUNK = 16
SSM_SUPER = 16
SSM_CW = SSM_CHUNK * SSM_GROUP

GQA_HEADS = 8
GQA_KV_HEADS = 2
GQA_HD = 64
GQA_W = GQA_HEADS * GQA_HD
GQA_SCALE = GQA_HD ** -0.5
LOG2E = math.log2(math.e)

N_GROUPS = 4
EXPERTS_PER_GROUP = 4
N_EXPERTS = N_GROUPS * EXPERTS_PER_GROUP
D_EXPERT = 512

ROW_TILE = 256
LANES = 128
V_ROWS = 80
VMEM_LIMIT = 56 * 1024 * 1024


def _cparams(*sem):
    return pltpu.CompilerParams(dimension_semantics=sem, vmem_limit_bytes=VMEM_LIMIT)


def _dot(a, b):
    return jnp.dot(a, b, preferred_element_type=F32)


def _dot_f32(a, b):
    a_hi = a.astype(BF16)
    a_lo = (a - a_hi.astype(F32)).astype(BF16)
    return _dot(a_hi, b) + _dot(a_lo, b)


def _rms(x, g):
    return x * lax.rsqrt(jnp.mean(x * x, axis=-1, keepdims=True) + NORM_EPS) * g


def _sigmoid(x):
    return 1.0 / (1.0 + jnp.exp(-x))


def _mod_kernel(cond_ref, w_ref, b_ref, o_ref):
    a = cond_ref[...]
    s = a * _sigmoid(a)
    w = w_ref[...]
    s_hi = s.astype(BF16)
    s_lo = (s - s_hi.astype(F32)).astype(BF16)
    w_hi = w.astype(BF16)
    w_lo = (w - w_hi.astype(F32)).astype(BF16)
    acc = _dot(s_hi, w_hi) + _dot(s_lo, w_hi) + _dot(s_hi, w_lo)
    o_ref[...] = acc + b_ref[...]


def _modulation(cond8, w_mod, b_mod):
    nl, d, n = w_mod.shape
    tn = 1536
    return pl.pallas_call(
        _mod_kernel,
        grid=(nl, n // tn),
        in_specs=[
            pl.BlockSpec((8, d), lambda l, j: (0, 0)),
            pl.BlockSpec((None, d, tn), lambda l, j: (l, 0, j)),
            pl.BlockSpec((None, 1, tn), lambda l, j: (l, 0, j)),
        ],
        out_specs=pl.BlockSpec((None, 8, tn), lambda l, j: (l, 0, j)),
        out_shape=jax.ShapeDtypeStruct((nl, 8, n), F32),
        compiler_params=_cparams("parallel", "parallel"),
        name="modulation",
    )(cond8, w_mod, b_mod.reshape(nl, 1, n))


def _mod_row(mod_ref, idx):
    m = mod_ref[...]
    d = m.shape[1] // N_MOD
    lat = m[0:1, idx * d:(idx + 1) * d]
    ctx = m[1:2, idx * d:(idx + 1) * d]
    return jnp.where(pl.program_id(0) == 0, ctx, lat)


_SEG = {}
_off = 0
for _name, _w in (("cq", 256), ("ckv", 128), ("kr", 128), ("krs", 128), ("ssm", 512), ("gq", 512),
                  ("gqs", 512), ("gk", 128), ("gks", 128), ("gv", 128), ("gate", 3072)):
    _SEG[_name] = (_off, _off + _w)
    _off += _w
W_CAT = _off


def _premix_kernel(x_ref, mod_ref, gpre_ref, w_ref, wuq_ref, wuqs_ref, wuk_ref, wuv_ref,
                   gcq_ref, gckv_ref, gq_ref, gqs_ref, gk_ref, gks_ref, ones_ref,
                   cm_ref, sm_ref, cg_ref, sg_ref,
                   qm_ref, km_ref, vm_ref, qg_ref, kg_ref, vg_ref, u_ref, gate_ref):
    x = x_ref[...]
    h = _rms(x, gpre_ref[...]) * (1.0 + _mod_row(mod_ref, 1)) + _mod_row(mod_ref, 0)
    hb = h.astype(BF16)

    def proj(name):
        a, b = _SEG[name]
        return _dot(hb, w_ref[:, a:b])

    cm = cm_ref[...]
    sm = sm_ref[...]
    cg = cg_ref[...]
    sg = sg_ref[...]

    cqn = _rms(proj("cq"), gcq_ref[...]).astype(BF16)
    ckvn = _rms(proj("ckv"), gckv_ref[...]).astype(BF16)
    q = _dot(cqn, wuq_ref[...])
    qs = _dot(cqn, wuqs_ref[...])
    kn = _dot(ckvn, wuk_ref[...])
    kr = proj("kr") * cm + proj("krs") * sm
    tr = x.shape[0]
    vrow = lax.broadcasted_iota(jnp.int32, (V_ROWS - MLA_V, tr), 0)
    v_tail = jnp.where(vrow == 0, 1.0, 0.0).astype(BF16)
    for hd in range(MLA_HEADS):
        sl = slice(hd * MLA_HEAD_PAD, (hd + 1) * MLA_HEAD_PAD)
        qh = (q[:, sl] * cm + qs[:, sl] * sm) * (MLA_SCALE * LOG2E)
        qm_ref[hd] = qh.T.astype(BF16)
        km_ref[hd] = (kn[:, sl] + kr).astype(BF16)
    vmt = _dot(ckvn, wuv_ref[...]).T
    for hd in range(MLA_HEADS):
        vm_ref[hd, 0:MLA_V, :] = vmt[hd * MLA_V:(hd + 1) * MLA_V, :].astype(BF16)
        vm_ref[hd, MLA_V:V_ROWS, :] = v_tail

    ones = ones_ref[...]

    def head_rms_scale(v, width):
        ms = _dot_f32(v * v, ones[:width, :width]) * (1.0 / GQA_HD)
        return lax.rsqrt(ms + NORM_EPS)

    gq = proj("gq")
    gqs = proj("gqs")
    rq = head_rms_scale(gq, GQA_W)
    cg4 = jnp.concatenate([cg] * (GQA_W // LANES), axis=1)
    sg4 = jnp.concatenate([sg] * (GQA_W // LANES), axis=1)
    qg = rq * (gq * gq_ref[...] * cg4 + gqs * gqs_ref[...] * sg4)
    qgt = (qg * (GQA_SCALE * LOG2E)).T.astype(BF16)
    zeros_q = jnp.zeros((GQA_HD, tr), BF16)
    for hd in range(GQA_HEADS):
        kvh = hd // (GQA_HEADS // GQA_KV_HEADS)
        qh = qgt[hd * GQA_HD:(hd + 1) * GQA_HD, :]
        qg_ref[hd] = jnp.concatenate([qh, zeros_q] if kvh == 0 else [zeros_q, qh], axis=0)
    gk = proj("gk")
    gks = proj("gks")
    rk = head_rms_scale(gk, GQA_KV_HEADS * GQA_HD)
    kg_ref[0] = (rk * (gk * gk_ref[...] * cg + gks * gks_ref[...] * sg)).astype(BF16)
    vgt = proj("gv").T
    for kvh in range(GQA_KV_HEADS):
        vg_ref[kvh, 0:GQA_HD, :] = vgt[kvh * GQA_HD:(kvh + 1) * GQA_HD, :].astype(BF16)
        vg_ref[kvh, GQA_HD:V_ROWS, :] = v_tail

    u_ref[...] = proj("ssm").astype(BF16)
    gate_ref[...] = _sigmoid(proj("gate")).astype(BF16)


def _premix(rows, mod_l, g_pre, wts, tabs):
    r, d = rows.shape
    nt = r // ROW_TILE
    row_spec = lambda w: pl.BlockSpec((ROW_TILE, w), lambda i: (i, 0))
    full = lambda a: pl.BlockSpec(a.shape, lambda i: (0,) * a.ndim)
    consts = [g_pre, wts["w_cat"], wts["wuq"], wts["wuqs"], wts["wuk"], wts["wuv"],
              wts["g_cq"], wts["g_ckv"], wts["g_q"], wts["g_qs"], wts["g_k"], wts["g_ks"], wts["ones"]]
    t_spec = lambda nh, f: pl.BlockSpec((nh, f, ROW_TILE), lambda i: (0, 0, i))
    k_spec = lambda nh, f: pl.BlockSpec((nh, ROW_TILE, f), lambda i: (0, i, 0))
    outs = [(t_spec(MLA_HEADS, MLA_HEAD_PAD), (MLA_HEADS, MLA_HEAD_PAD, r)),
            (k_spec(MLA_HEADS, MLA_HEAD_PAD), (MLA_HEADS, r, MLA_HEAD_PAD)),
            (t_spec(MLA_HEADS, V_ROWS), (MLA_HEADS, V_ROWS, r)),
            (t_spec(GQA_HEADS, GQA_KV_HEADS * GQA_HD), (GQA_HEADS, GQA_KV_HEADS * GQA_HD, r)),
            (k_spec(1, GQA_KV_HEADS * GQA_HD), (1, r, GQA_KV_HEADS * GQA_HD)),
            (t_spec(GQA_KV_HEADS, V_ROWS), (GQA_KV_HEADS, V_ROWS, r)),
            (row_spec(SSM_W), (r, SSM_W)),
            (row_spec(3 * d), (r, 3 * d))]
    return pl.pallas_call(
        _premix_kernel,
        grid=(nt,),
        in_specs=[row_spec(d), full(mod_l)] + [full(a) for a in consts] + [row_spec(LANES)] * 4,
        out_specs=[s for s, _ in outs],
        out_shape=[jax.ShapeDtypeStruct(shp, BF16) for _, shp in outs],
        compiler_params=_cparams("parallel"),
        name="premix",
    )(rows, mod_l, *consts, tabs["cm"], tabs["sm"], tabs["cg"], tabs["sg"])


def _attn_kernel(q_ref, k_ref, v_ref, o_ref, s_ref, *, tk, nsub, n_ctx, n_iter, dv, hb, k_shared, v_shared,
                 ahead):
    n_items = hb * nsub
    tq = q_ref.shape[2]

    def scores(h, off, size):
        return _dot(k_ref[0 if k_shared else h, pl.ds(off, size), :], q_ref[h])

    def update(h, m, acc, s, off, size):
        m_new = jnp.maximum(m, jnp.max(s, axis=0, keepdims=True))
        p = jnp.exp2(s - m_new).astype(BF16)
        pv = _dot(v_ref[0 if v_shared else h, :, pl.ds(off, size)], p)
        return m_new, jnp.exp2(m - m_new) * acc + pv

    def issue(item, base):
        h, sub = item % hb, item // hb
        s_ref[item] = scores(h, pl.multiple_of(base + sub * tk, LANES), tk)

    m0 = jnp.full((1, tq), -1e30, F32)
    acc0 = jnp.zeros((V_ROWS, tq), F32)
    s_ctx = [scores(h, 0, n_ctx) for h in range(hb)]
    for item in range(ahead):
        issue(item, n_ctx)
    carry = tuple(update(h, m0, acc0, s_ctx[h], 0, n_ctx) for h in range(hb))

    def body(j, carry):
        carry = list(carry)
        base = n_ctx + j * (nsub * tk)
        base_next = n_ctx + jnp.minimum(j + 1, n_iter - 1) * (nsub * tk)
        for item in range(n_items):
            nxt = item + ahead
            if nxt < n_items:
                issue(nxt, base)
            else:
                issue(nxt - n_items, base_next)
            h, sub = item % hb, item // hb
            off = pl.multiple_of(base + sub * tk, LANES)
            carry[h] = update(h, carry[h][0], carry[h][1], s_ref[item], off, tk)
        return tuple(carry)

    carry = lax.fori_loop(0, jnp.where(pl.program_id(1) == 0, 0, n_iter), body, carry)
    for h in range(hb):
        acc = carry[h][1]
        o_ref[h] = acc[:dv] / acc[dv:dv + 1]


def _attention(qt, k, vt, *, hb, nsub, tq=256, tk=512, ahead=2):
    nh, dq, r = qt.shape
    k_shared = k.shape[0] != nh
    v_shared = vt.shape[0] != nh
    assert nh % hb == 0 and k.shape[0] in (nh, 1) and vt.shape[0] in (nh, nh // hb)
    dv = V_ROWS - 16
    assert (r - ROW_TILE) % (tk * nsub) == 0 and ahead < hb * nsub
    kern = functools.partial(_attn_kernel, tk=tk, nsub=nsub, n_ctx=ROW_TILE, n_iter=(r - ROW_TILE) // (tk * nsub),
                             dv=dv, hb=hb, k_shared=k_shared, v_shared=v_shared, ahead=ahead)
    k_spec = (pl.BlockSpec((1, r, dq), lambda g, i: (0, 0, 0)) if k_shared
              else pl.BlockSpec((hb, r, dq), lambda g, i: (g, 0, 0)))
    v_spec = pl.BlockSpec((1 if v_shared else hb, V_ROWS, r), lambda g, i: (g, 0, 0))
    return pl.pallas_call(
        kern,
        grid=(nh // hb, r // tq),
        in_specs=[pl.BlockSpec((hb, dq, tq), lambda g, i: (g, 0, i)), k_spec, v_spec],
        out_specs=pl.BlockSpec((hb, dv, tq), lambda g, i: (g, 0, i)),
        out_shape=jax.ShapeDtypeStruct((nh, dv, r), F32),
        scratch_shapes=[pltpu.VMEM((hb * nsub, tk, tq), F32)],
        compiler_params=_cparams("parallel", "arbitrary"),
        name="attention",
    )(qt, k, vt)


def _s5_kernel(u_ref, blk_ref, bp_ref, cp_ref, ar_ref, ai_ref, y_ref, xr_ref, xi_ref, *, n_ctx_chunks):
    t, j, _ = u_ref.shape
    npair = t // 2
    ns = xr_ref.shape[1]
    row = lax.broadcasted_iota(jnp.int32, (j, ns), 0)

    def run(reverse):
        ucat = jnp.concatenate([u_ref[k] for k in range(t)], axis=1)
        xc = _dot(ucat, bp_ref[...])
        if reverse:
            shift = j - n_ctx_chunks - 1
            keep = row < j - 1
        else:
            shift = 1
            keep = row >= 1
        xr_ref[...] = jnp.where(keep, pltpu.roll(xc[:, :ns], shift, 0), 0.0)
        xi_ref[...] = jnp.where(keep, pltpu.roll(xc[:, ns:], shift, 0), 0.0)
        step = 1
        k = 0
        while step < j:
            ar = ar_ref[k:k + 1, :]
            ai = ai_ref[k:k + 1, :]
            xr = xr_ref[...]
            xi = xi_ref[...]
            if reverse:
                keep = row < j - step
                sr = jnp.where(keep, pltpu.roll(xr, j - step, 0), 0.0)
                si = jnp.where(keep, pltpu.roll(xi, j - step, 0), 0.0)
            else:
                keep = row >= step
                sr = jnp.where(keep, pltpu.roll(xr, step, 0), 0.0)
                si = jnp.where(keep, pltpu.roll(xi, step, 0), 0.0)
            xr_ref[...] = xr + ar * sr - ai * si
            xi_ref[...] = xi + ar * si + ai * sr
            step *= 2
            k += 1
        xin = jnp.concatenate([xr_ref[...], xi_ref[...]], axis=1)
        if reverse:
            xin = pltpu.roll(xin, n_ctx_chunks, 0)
        xin = xin.astype(BF16)
        for po in range(npair):
            y = _dot(xin, cp_ref[:, po * 256:(po + 1) * 256])
            pins = range(po, npair) if reverse else range(0, po + 1)
            for pi_ in pins:
                upair = jnp.concatenate([u_ref[2 * pi_], u_ref[2 * pi_ + 1]], axis=1)
                y += _dot(upair, blk_ref[abs(po - pi_)])
            if reverse:
                y_ref[2 * po] += y[:, :128]
                y_ref[2 * po + 1] += y[:, 128:]
            else:
                y_ref[2 * po] = y[:, :128]
                y_ref[2 * po + 1] = y[:, 128:]

    @pl.when(pl.program_id(1) == 0)
    def _():
        run(False)

    @pl.when(pl.program_id(1) == 1)
    def _():
        run(True)


def _s5_scan(ut, mats, n_ctx_chunks):
    t, j, w = ut.shape
    no = w // LANES
    ns = (LANES // SSM_GROUP) * SSM_STATE
    wspec = lambda *tail: pl.BlockSpec((None, None) + tail, lambda o, d: (d, o) + (0,) * len(tail))
    kern = functools.partial(_s5_kernel, n_ctx_chunks=n_ctx_chunks)
    return pl.pallas_call(
        kern,
        grid=(no, SSM_DIRS),
        in_specs=[pl.BlockSpec((t, j, LANES), lambda o, d: (0, 0, o)),
                  wspec(t // 2, 256, 256), wspec(t * LANES, 2 * ns), wspec(2 * ns, t * LANES),
                  wspec(16, ns), wspec(16, ns)],
        out_specs=pl.BlockSpec((t, j, LANES), lambda o, d: (0, 0, o)),
        out_shape=jax.ShapeDtypeStruct((t, j, w), F32),
        scratch_shapes=[pltpu.VMEM((j, ns), F32)] * 2,
        compiler_params=_cparams("parallel", "arbitrary"),
        name="s5_scan",
    )(ut, mats["blk"], mats["bp"], mats["cp"], mats["ar"], mats["ai"])


def _s5_matrices(a_re, a_im, log_dt, b_re, b_im, c_re, c_im, d_skip):
    t = SSM_CHUNK
    a_re = a_re.astype(F32)
    a_im = a_im.astype(F32)
    dt = jnp.exp(log_dt.astype(F32))[..., None]
    den = a_re * a_re + a_im * a_im

    def lpow(k):
        kf = jnp.asarray(k, F32)
        mag = jnp.exp(a_re * dt * kf)
        return mag * jnp.cos(a_im * dt * kf), mag * jnp.sin(a_im * dt * kf)

    lr, li = lpow(1.0)
    cr = ((lr - 1.0) * a_re + li * a_im) / den
    ci = (li * a_re - (lr - 1.0) * a_im) / den
    bbr = cr[..., None] * b_re - ci[..., None] * b_im
    bbi = cr[..., None] * b_im + ci[..., None] * b_re
    c_re = c_re.astype(F32)
    c_im = c_im.astype(F32)

    ks = jnp.arange(t + 1, dtype=F32)[:, None, None, None]
    pr, pi = lpow(ks)
    lbr = pr[:t, ..., None] * bbr - pi[:t, ..., None] * bbi
    lbi = pr[:t, ..., None] * bbi + pi[:t, ..., None] * bbr
    kern = (jnp.einsum('dgop,kdgpc->dgkoc', c_re, lbr, precision='highest')
            - jnp.einsum('dgop,kdgpc->dgkoc', c_im, lbi, precision='highest'))
    skip = d_skip.astype(F32).reshape(SSM_GROUPS, SSM_GROUP)
    eye = jnp.eye(SSM_GROUP, dtype=F32)
    kern = kern.at[0, :, 0].add(skip[:, :, None] * eye)
    gl = LANES // SSM_GROUP
    no = SSM_GROUPS // gl
    eye_g = jnp.eye(gl, dtype=F32)
    dd = jnp.arange(t // 2)[:, None, None]
    ti = jnp.arange(2)[None, :, None]
    to = jnp.arange(2)[None, None, :]
    lag = jnp.stack([2 * dd + to - ti, 2 * dd + ti - to])
    pick = jax.vmap(lambda kd, ld: jnp.take(kd, jnp.clip(ld, 0, t - 1).reshape(-1), axis=1))(kern, lag)
    pick = pick.reshape(SSM_DIRS, no, gl, t // 2, 2, 2, SSM_GROUP, SSM_GROUP)
    pick = jnp.where((lag >= 0)[:, None, None, :, :, :, None, None], pick, 0.0)
    blk = jnp.einsum('dogDabxy,gh->doDagybhx', pick, eye_g).reshape(SSM_DIRS, no, t // 2, 256, 256)
    def contrib(lb):
        both = jnp.stack([lb[::-1, 0], lb[:, 1]])
        both = both.reshape(SSM_DIRS, t, no, gl, SSM_STATE, SSM_GROUP)
        return jnp.einsum('dtogpc,gh->dotgchp', both, eye_g).reshape(SSM_DIRS, no, t * LANES, gl * SSM_STATE)
    bp = jnp.concatenate([contrib(lbr), contrib(lbi)], axis=-1)
    p1r = pr[1:, :, :, None, :]
    p1i = pi[1:, :, :, None, :]
    rd_r = c_re * p1r - c_im * p1i
    rd_i = -(c_re * p1i + c_im * p1r)
    def readout(rd):
        both = jnp.stack([rd[:, 0], rd[::-1, 1]])
        both = both.reshape(SSM_DIRS, t, no, gl, SSM_GROUP, SSM_STATE)
        return jnp.einsum('dtogcp,gh->dogpthc', both, eye_g).reshape(SSM_DIRS, no, gl * SSM_STATE, t * LANES)
    cp = jnp.concatenate([readout(rd_r), readout(rd_i)], axis=2)
    kk = (t * 2.0 ** jnp.minimum(jnp.arange(16), 10)).astype(F32)[:, None, None, None]
    qr, qi = lpow(kk)
    lanes = lambda q: jnp.transpose(q, (1, 2, 0, 3)).reshape(SSM_DIRS, no, gl, 16, SSM_STATE)
    ar = jnp.transpose(lanes(qr), (0, 1, 3, 2, 4)).reshape(SSM_DIRS, no, 16, gl * SSM_STATE)
    ai = jnp.transpose(lanes(qi), (0, 1, 3, 2, 4)).reshape(SSM_DIRS, no, 16, gl * SSM_STATE)
    return dict(blk=blk.astype(BF16), bp=bp.astype(BF16), cp=cp.astype(BF16), ar=ar, ai=ai)


def _merge_kernel(x_ref, mod_ref, om_ref, og_ref, y_ref, gate_ref,
                  wglu_ref, bglu_ref, wbm_ref, wbs_ref, wbg_ref, wout_ref,
                  gpost_ref, gffn_ref, wrt_hi_ref, wrt_lo_ref, brt_ref,
                  xo_ref, h_ref, cw_ref):
    d = x_ref.shape[1]
    y = y_ref[...]
    y1 = 0.5 * y * (1.0 + jnp.tanh(math.sqrt(2.0 / math.pi) * (y + 0.044715 * (y * y * y))))
    ssm = y1 * _sigmoid(_dot(y1.astype(BF16), wglu_ref[...]) + bglu_ref[...])
    gate = gate_ref[...]
    o_mla = om_ref[...].reshape(-1, om_ref.shape[2]).T.astype(BF16)
    o_gqa = og_ref[...].reshape(-1, og_ref.shape[2]).T.astype(BF16)
    m = (gate[:, 0:d].astype(F32) * _dot(o_mla, wbm_ref[...])
         + gate[:, d:2 * d].astype(F32) * _dot(ssm.astype(BF16), wbs_ref[...])
         + gate[:, 2 * d:3 * d].astype(F32) * _dot(o_gqa, wbg_ref[...]))
    ymix = _dot(m.astype(BF16), wout_ref[...])
    x = x_ref[...] + _mod_row(mod_ref, 2) * _rms(ymix, gpost_ref[...])
    xo_ref[...] = x
    h = _rms(x, gffn_ref[...]) * (1.0 + _mod_row(mod_ref, 4)) + _mod_row(mod_ref, 3)
    h_ref[...] = h.astype(BF16)

    h_hi = h.astype(BF16)
    h_lo = (h - h_hi.astype(F32)).astype(BF16)
    lg = (_dot(h_hi, wrt_hi_ref[...]) + _dot(h_lo, wrt_hi_ref[...]) + _dot(h_hi, wrt_lo_ref[...])
          + brt_ref[...])
    lane = lax.broadcasted_iota(jnp.int32, lg.shape, 1)
    neg = jnp.float32(-1e30)
    is_g = lane < N_GROUPS
    gl = jnp.where(is_g, lg, neg)
    gmax = jnp.max(gl, axis=-1, keepdims=True)
    gsel = jnp.min(jnp.where(is_g & (gl == gmax), lane, LANES), axis=-1, keepdims=True)
    pg = 1.0 / jnp.sum(jnp.where(is_g, jnp.exp(gl - gmax), 0.0), axis=-1, keepdims=True)
    lo = N_GROUPS + gsel * EXPERTS_PER_GROUP
    in_grp = (lane >= lo) & (lane < lo + EXPERTS_PER_GROUP)
    el = jnp.where(in_grp, lg, neg)
    m1 = jnp.max(el, axis=-1, keepdims=True)
    i1 = jnp.min(jnp.where(in_grp & (el == m1), lane, LANES), axis=-1, keepdims=True)
    el2 = jnp.where(lane == i1, neg, el)
    m2 = jnp.max(el2, axis=-1, keepdims=True)
    i2 = jnp.min(jnp.where(in_grp & (el2 == m2), lane, LANES), axis=-1, keepdims=True)
    e2 = jnp.exp(m2 - m1)
    t1 = pg / (1.0 + e2)
    cw_ref[...] = jnp.where(lane == i1, t1, 0.0) + jnp.where(lane == i2, t1 * e2, 0.0)


def _merge(rows, mod_l, o_mla, o_gqa, y_ssm, gates, wts):
    r, d = rows.shape
    nt = r // ROW_TILE
    row_spec = lambda w: pl.BlockSpec((ROW_TILE, w), lambda i: (i, 0))
    full = lambda a: pl.BlockSpec(a.shape, lambda i: (0,) * a.ndim)
    consts = [wts["w_glu"], wts["b_glu"], wts["w_br_mla"], wts["w_br_ssm"], wts["w_br_gqa"], wts["w_out"],
              wts["g_post_mix"], wts["g_pre_ffn"], wts["w_rt_hi"], wts["w_rt_lo"], wts["b_rt"]]
    return pl.pallas_call(
        _merge_kernel,
        grid=(nt,),
        in_specs=[row_spec(d), full(mod_l),
                  pl.BlockSpec(o_mla.shape[:2] + (ROW_TILE,), lambda i: (0, 0, i)),
                  pl.BlockSpec(o_gqa.shape[:2] + (ROW_TILE,), lambda i: (0, 0, i)),
                  row_spec(SSM_W), row_spec(3 * d)] + [full(a) for a in consts],
        out_specs=[row_spec(d), row_spec(d), row_spec(LANES)],
        out_shape=[jax.ShapeDtypeStruct((r, d), F32), jax.ShapeDtypeStruct((r, d), BF16),
                   jax.ShapeDtypeStruct((r, LANES), F32)],
        compiler_params=_cparams("parallel"),
        name="merge",
    )(rows, mod_l, o_mla, o_gqa, y_ssm, gates, *consts)


def _moe_kernel(h_ref, cw_ref, x_ref, mod_ref, wg_ref, wu_ref, wd_ref, gpost_ref, o_ref, acc_ref, *, n_ctx):
    e = pl.program_id(1)

    @pl.when(e == 0)
    def _():
        acc_ref[...] = jnp.zeros_like(acc_ref)

    h = h_ref[...]
    a = _dot(h, wg_ref[...])
    hid = (a * _sigmoid(a)) * _dot(h, wu_ref[...])
    lane = lax.broadcasted_iota(jnp.int32, cw_ref.shape, 1)
    w_e = jnp.sum(jnp.where(lane == e + N_GROUPS, cw_ref[...], 0.0), axis=-1, keepdims=True)
    acc_ref[...] += w_e * _dot(hid.astype(BF16), wd_ref[...])

    @pl.when(e == pl.num_programs(1) - 1)
    def _():
        m = mod_ref[...]
        d = m.shape[1] // N_MOD
        tm = o_ref.shape[0]
        row = pl.program_id(0) * tm + lax.broadcasted_iota(jnp.int32, (tm, 1), 0)
        ga = jnp.where(row < n_ctx, m[1:2, 5 * d:6 * d], m[0:1, 5 * d:6 * d])
        o_ref[...] = x_ref[...] + ga * _rms(acc_ref[...], gpost_ref[...])


def _moe(h, cw, rows, mod_l, wg, wu, wd, g_post, n_ctx):
    r, d = rows.shape
    tm = 1280 if r % 1280 == 0 else ROW_TILE
    ne, _, de = wg.shape
    kern = functools.partial(_moe_kernel, n_ctx=n_ctx)
    return pl.pallas_call(
        kern,
        grid=(r // tm, ne),
        in_specs=[
            pl.BlockSpec((tm, d), lambda i, e: (i, 0)),
            pl.BlockSpec((tm, LANES), lambda i, e: (i, 0)),
            pl.BlockSpec((tm, d), lambda i, e: (i, 0)),
            pl.BlockSpec(mod_l.shape, lambda i, e: (0, 0)),
            pl.BlockSpec((None, d, de), lambda i, e: (e, 0, 0)),
            pl.BlockSpec((None, d, de), lambda i, e: (e, 0, 0)),
            pl.BlockSpec((None, de, d), lambda i, e: (e, 0, 0)),
            pl.BlockSpec(g_post.shape, lambda i, e: (0, 0)),
        ],
        out_specs=pl.BlockSpec((tm, d), lambda i, e: (i, 0)),
        out_shape=jax.ShapeDtypeStruct((r, d), F32),
        scratch_shapes=[pltpu.VMEM((tm, d), F32)],
        compiler_params=_cparams("parallel", "arbitrary"),
        name="moe",
    )(h, cw, rows, mod_l, wg, wu, wd, g_post)


def _rope_tables(seq, n_ctx):
    tpos = jnp.arange(seq, dtype=jnp.int32)
    rows = (tpos // GRID_W).astype(F32)
    cols = (tpos % GRID_W).astype(F32)

    def pattern(width):
        half = width // 4
        freqs = ROPE_THETA ** (-jnp.arange(half, dtype=F32) / half)
        ar = rows[:, None] * freqs[None, :]
        ac = cols[:, None] * freqs[None, :]
        c = jnp.concatenate([jnp.cos(ar), jnp.cos(ar), jnp.cos(ac), jnp.cos(ac)], axis=1)
        s = jnp.concatenate([-jnp.sin(ar), jnp.sin(ar), -jnp.sin(ac), jnp.sin(ac)], axis=1)
        return c, s

    c32, s32 = pattern(MLA_ROPE)
    c64, s64 = pattern(GQA_HD)
    one = jnp.ones((seq, 1), F32)
    cm = jnp.concatenate([one * jnp.ones((1, MLA_NOPE)), c32, one * jnp.ones((1, 32))], axis=1)
    sm = jnp.concatenate([jnp.zeros((seq, MLA_NOPE)), s32, jnp.zeros((seq, 32))], axis=1)
    cg = jnp.concatenate([c64, c64], axis=1)
    sg = jnp.concatenate([s64, s64], axis=1)
    ctx_c = jnp.ones((n_ctx, LANES), F32)
    ctx_s = jnp.zeros((n_ctx, LANES), F32)
    cat = lambda a, b: jnp.concatenate([a, b], axis=0)
    return dict(cm=cat(ctx_c, cm), sm=cat(ctx_s, sm), cg=cat(ctx_c, cg), sg=cat(ctx_s, sg))


def _swap_perm(width):
    q = width // 4
    return np.concatenate([np.arange(q, 2 * q), np.arange(0, q), np.arange(3 * q, 4 * q), np.arange(2 * q, 3 * q)])


def _layer_weights(l, p):
    d = p["w_in"].shape[1]
    w_in = p["w_in"][l]
    o = 0
    cq = w_in[:, o:o + MLA_Q_RANK]; o += MLA_Q_RANK
    ckv = w_in[:, o:o + MLA_KV_RANK]; o += MLA_KV_RANK
    kr = w_in[:, o:o + MLA_ROPE]; o += MLA_ROPE
    ssm = w_in[:, o:o + SSM_W]; o += SSM_W
    gq = w_in[:, o:o + GQA_W]; o += GQA_W
    gk = w_in[:, o:o + GQA_KV_HEADS * GQA_HD]; o += GQA_KV_HEADS * GQA_HD
    gv = w_in[:, o:o + GQA_KV_HEADS * GQA_HD]; o += GQA_KV_HEADS * GQA_HD
    gate = w_in[:, o:]
    p32 = _swap_perm(MLA_ROPE)
    p64 = _swap_perm(GQA_HD)
    perm_q = np.concatenate([p64 + GQA_HD * h for h in range(GQA_HEADS)])
    perm_k = np.concatenate([p64 + GQA_HD * h for h in range(GQA_KV_HEADS)])
    z = lambda n: jnp.zeros((d, n), F32)
    kr128 = jnp.concatenate([z(MLA_NOPE), kr, z(32)], axis=1)
    krs128 = jnp.concatenate([z(MLA_NOPE), kr[:, p32], z(32)], axis=1)
    w_cat = jnp.concatenate([cq, ckv, kr128, krs128, ssm, gq, gq[:, perm_q], gk, gk[:, perm_k], gv, gate],
                            axis=1).astype(BF16)

    w_uq = p["w_uq"][l].reshape(MLA_Q_RANK, MLA_HEADS, MLA_NOPE + MLA_ROPE)
    zq = jnp.zeros((MLA_Q_RANK, MLA_HEADS, 32), F32)
    wuq = jnp.concatenate([w_uq, zq], axis=2).reshape(MLA_Q_RANK, -1)
    wuqs = jnp.concatenate([jnp.zeros((MLA_Q_RANK, MLA_HEADS, MLA_NOPE), F32),
                            w_uq[:, :, MLA_NOPE:][:, :, p32], zq], axis=2).reshape(MLA_Q_RANK, -1)
    w_ukv = p["w_ukv"][l].reshape(MLA_KV_RANK, MLA_HEADS, MLA_NOPE + MLA_V)
    wuk = jnp.concatenate([w_ukv[:, :, :MLA_NOPE], jnp.zeros((MLA_KV_RANK, MLA_HEADS, 64), F32)],
                          axis=2).reshape(MLA_KV_RANK, -1)
    wuv = w_ukv[:, :, MLA_NOPE:].reshape(MLA_KV_RANK, -1)

    g_qn = p["g_qn"][l]
    g_kn = p["g_kn"][l]
    blk = np.arange(GQA_W) // GQA_HD
    ones = jnp.asarray((blk[:, None] == blk[None, :]).astype(np.float32), BF16)

    w_rt = jnp.zeros((d, LANES), F32)
    w_rt = w_rt.at[:, :N_GROUPS].set(p["w_group"][l]).at[:, N_GROUPS:N_GROUPS + N_EXPERTS].set(p["w_router"][l])
    w_rt_hi = w_rt.astype(BF16)
    w_rt_lo = (w_rt - w_rt_hi.astype(F32)).astype(BF16)
    b_rt = jnp.zeros((1, LANES), F32)
    b_rt = b_rt.at[0, :N_GROUPS].set(p["b_group"][l]).at[0, N_GROUPS:N_GROUPS + N_EXPERTS].set(p["b_router"][l])

    row = lambda v: v.reshape(1, -1).astype(F32)
    return dict(
        w_cat=w_cat, wuq=wuq.astype(BF16), wuqs=wuqs.astype(BF16), wuk=wuk.astype(BF16), wuv=wuv.astype(BF16),
        g_cq=row(p["g_cq"][l]), g_ckv=row(p["g_ckv"][l]),
        g_q=row(jnp.tile(g_qn, GQA_HEADS)), g_qs=row(jnp.tile(g_qn[p64], GQA_HEADS)),
        g_k=row(jnp.tile(g_kn, GQA_KV_HEADS)), g_ks=row(jnp.tile(g_kn[p64], GQA_KV_HEADS)),
        ones=ones,
        w_glu=p["w_glu"][l].astype(BF16), b_glu=row(p["b_glu"][l]),
        w_br_mla=p["w_br_mla"][l].astype(BF16), w_br_ssm=p["w_br_ssm"][l].astype(BF16),
        w_br_gqa=p["w_br_gqa"][l].astype(BF16), w_out=p["w_out"][l].astype(BF16),
        g_post_mix=row(p["g_post_mix"][l]), g_pre_ffn=row(p["g_pre_ffn"][l]),
        w_rt_hi=w_rt_hi, w_rt_lo=w_rt_lo, b_rt=b_rt,
    )


def kernel(x, c, ctx, c_ctx, w_mod, b_mod, g_pre_mix, g_post_mix, g_pre_ffn, g_post_ffn, w_in, g_cq, g_ckv, w_uq, w_ukv, g_qn, g_kn, ssm_a_re, ssm_a_im, ssm_log_dt, ssm_b_re, ssm_b_im, ssm_c_re, ssm_c_im, ssm_d, w_glu, b_glu, w_br_mla, w_br_ssm, w_br_gqa, w_out, w_group, b_group, w_router, b_router, w_exp_gate, w_exp_up, w_exp_down):
    assert x.shape[0] == 1 and ctx.shape[0] == 1
    seq, d = x.shape[1], x.shape[2]
    n_ctx = ctx.shape[1]
    assert n_ctx == ROW_TILE and seq % (SSM_CHUNK * SSM_SUPER) == 0
    depth = w_in.shape[0]
    params = dict(w_in=w_in, g_cq=g_cq, g_ckv=g_ckv, w_uq=w_uq, w_ukv=w_ukv, g_qn=g_qn, g_kn=g_kn,
                  w_glu=w_glu, b_glu=b_glu, w_br_mla=w_br_mla, w_br_ssm=w_br_ssm, w_br_gqa=w_br_gqa,
                  w_out=w_out, g_post_mix=g_post_mix, g_pre_ffn=g_pre_ffn,
                  w_group=w_group, b_group=b_group, w_router=w_router, b_router=b_router)

    cond8 = jnp.zeros((8, d), F32).at[0].set(c[0]).at[1].set(c_ctx)
    mods = _modulation(cond8, w_mod, b_mod)
    tabs = _rope_tables(seq, n_ctx)
    rows = jnp.concatenate([ctx[0], x[0]], axis=0)
    r = rows.shape[0]

    for l in range(depth):
        wts = _layer_weights(l, params)
        mod_l = mods[l]
        qm, km, vm, qg, kg, vg, u, gates = _premix(rows, mod_l, g_pre_mix[l].reshape(1, d), wts, tabs)

        o_mla = _attention(qm, km, vm, hb=2, nsub=4)
        o_gqa = _attention(qg, kg, vg, hb=GQA_HEADS // GQA_KV_HEADS, nsub=2)

        mats = _s5_matrices(ssm_a_re[l], ssm_a_im[l], ssm_log_dt[l], ssm_b_re[l], ssm_b_im[l],
                            ssm_c_re[l], ssm_c_im[l], ssm_d[l])
        ut = jnp.transpose(u.reshape(r // SSM_CHUNK, SSM_CHUNK, SSM_W), (1, 0, 2))
        yt = _s5_scan(ut, mats, n_ctx // SSM_CHUNK)
        y_ssm = jnp.transpose(yt, (1, 0, 2)).reshape(r, SSM_W)

        rows, h, cw = _merge(rows, mod_l, o_mla, o_gqa, y_ssm, gates, wts)
        rows = _moe(h, cw, rows, mod_l, w_exp_gate[l].astype(BF16), w_exp_up[l].astype(BF16),
                    w_exp_down[l].astype(BF16), g_post_ffn[l].reshape(1, d), n_ctx)

    return rows[n_ctx:].reshape(1, seq, d)
```

```python
import functools
import math

import jax
import jax.numpy as jnp
import numpy as np
from jax import lax
from jax.experimental import pallas as pl
from jax.experimental.pallas import tpu as pltpu

F32 = jnp.float32
BF16 = jnp.bfloat16

GRID_W = 64
ROPE_THETA = 10000.0
NORM_EPS = 1e-6
N_MOD = 6

MLA_HEADS = 8
MLA_NOPE = 64
MLA_ROPE = 32
MLA_V = 64
MLA_Q_RANK = 256
MLA_KV_RANK = 128
MLA_SCALE = (MLA_NOPE + MLA_ROPE) ** -0.5
MLA_HEAD_PAD = 128

SSM_GROUP = 16
SSM_GROUPS = 32
SSM_STATE = 64
SSM_W = SSM_GROUP * SSM_GROUPS
SSM_DIRS = 2
SSM_CHUNK = 16
SSM_SUPER = 16
SSM_CW = SSM_CHUNK * SSM_GROUP

GQA_HEADS = 8
GQA_KV_HEADS = 2
GQA_HD = 64
GQA_W = GQA_HEADS * GQA_HD
GQA_SCALE = GQA_HD ** -0.5
LOG2E = math.log2(math.e)

N_GROUPS = 4
EXPERTS_PER_GROUP = 4
N_EXPERTS = N_GROUPS * EXPERTS_PER_GROUP
D_EXPERT = 512

ROW_TILE = 256
LANES = 128
V_ROWS = 80
VMEM_LIMIT = 56 * 1024 * 1024


def _cparams(*sem):
    return pltpu.CompilerParams(dimension_semantics=sem, vmem_limit_bytes=VMEM_LIMIT)


def _dot(a, b):
    return jnp.dot(a, b, preferred_element_type=F32)


def _dot_f32(a, b):
    a_hi = a.astype(BF16)
    a_lo = (a - a_hi.astype(F32)).astype(BF16)
    return _dot(a_hi, b) + _dot(a_lo, b)


def _rms(x, g):
    return x * lax.rsqrt(jnp.mean(x * x, axis=-1, keepdims=True) + NORM_EPS) * g


def _sigmoid(x):
    return 1.0 / (1.0 + jnp.exp(-x))


def _mod_kernel(cond_ref, w_ref, b_ref, o_ref):
    a = cond_ref[...]
    s = a * _sigmoid(a)
    w = w_ref[...]
    s_hi = s.astype(BF16)
    s_lo = (s - s_hi.astype(F32)).astype(BF16)
    w_hi = w.astype(BF16)
    w_lo = (w - w_hi.astype(F32)).astype(BF16)
    acc = _dot(s_hi, w_hi) + _dot(s_lo, w_hi) + _dot(s_hi, w_lo)
    o_ref[...] = acc + b_ref[...]


def _modulation(cond8, w_mod, b_mod):
    nl, d, n = w_mod.shape
    tn = 1536
    return pl.pallas_call(
        _mod_kernel,
        grid=(nl, n // tn),
        in_specs=[
            pl.BlockSpec((8, d), lambda l, j: (0, 0)),
            pl.BlockSpec((None, d, tn), lambda l, j: (l, 0, j)),
            pl.BlockSpec((None, 1, tn), lambda l, j: (l, 0, j)),
        ],
        out_specs=pl.BlockSpec((None, 8, tn), lambda l, j: (l, 0, j)),
        out_shape=jax.ShapeDtypeStruct((nl, 8, n), F32),
        compiler_params=_cparams("parallel", "parallel"),
        name="modulation",
    )(cond8, w_mod, b_mod.reshape(nl, 1, n))


def _mod_row(mod_ref, idx):
    m = mod_ref[...]
    d = m.shape[1] // N_MOD
    lat = m[0:1, idx * d:(idx + 1) * d]
    ctx = m[1:2, idx * d:(idx + 1) * d]
    return jnp.where(pl.program_id(0) == 0, ctx, lat)


_SEG = {}
_off = 0
for _name, _w in (("cq", 256), ("ckv", 128), ("kr", 128), ("krs", 128), ("ssm", 512), ("gq", 512),
                  ("gqs", 512), ("gk", 128), ("gks", 128), ("gv", 128), ("gate", 3072)):
    _SEG[_name] = (_off, _off + _w)
    _off += _w
W_CAT = _off


def _premix_kernel(x_ref, mod_ref, gpre_ref, w_ref, wuq_ref, wuqs_ref, wuk_ref, wuv_ref,
                   gcq_ref, gckv_ref, gq_ref, gqs_ref, gk_ref, gks_ref, ones_ref,
                   cm_ref, sm_ref, cg_ref, sg_ref,
                   qm_ref, km_ref, vm_ref, qg_ref, kg_ref, vg_ref, u_ref, gate_ref):
    x = x_ref[...]
    h = _rms(x, gpre_ref[...]) * (1.0 + _mod_row(mod_ref, 1)) + _mod_row(mod_ref, 0)
    hb = h.astype(BF16)

    def proj(name):
        a, b = _SEG[name]
        return _dot(hb, w_ref[:, a:b])

    cm = cm_ref[...]
    sm = sm_ref[...]
    cg = cg_ref[...]
    sg = sg_ref[...]

    cqn = _rms(proj("cq"), gcq_ref[...]).astype(BF16)
    ckvn = _rms(proj("ckv"), gckv_ref[...]).astype(BF16)
    q = _dot(cqn, wuq_ref[...])
    qs = _dot(cqn, wuqs_ref[...])
    kn = _dot(ckvn, wuk_ref[...])
    kr = proj("kr") * cm + proj("krs") * sm
    tr = x.shape[0]
    vrow = lax.broadcasted_iota(jnp.int32, (V_ROWS - MLA_V, tr), 0)
    v_tail = jnp.where(vrow == 0, 1.0, 0.0).astype(BF16)
    for hd in range(MLA_HEADS):
        sl = slice(hd * MLA_HEAD_PAD, (hd + 1) * MLA_HEAD_PAD)
        qh = (q[:, sl] * cm + qs[:, sl] * sm) * (MLA_SCALE * LOG2E)
        qm_ref[hd] = qh.T.astype(BF16)
        km_ref[hd] = (kn[:, sl] + kr).astype(BF16)
    vmt = _dot(ckvn, wuv_ref[...]).T
    for hd in range(MLA_HEADS):
        vm_ref[hd, 0:MLA_V, :] = vmt[hd * MLA_V:(hd + 1) * MLA_V, :].astype(BF16)
        vm_ref[hd, MLA_V:V_ROWS, :] = v_tail

    ones = ones_ref[...]

    def head_rms_scale(v, width):
        ms = _dot_f32(v * v, ones[:width, :width]) * (1.0 / GQA_HD)
        return lax.rsqrt(ms + NORM_EPS)

    gq = proj("gq")
    gqs = proj("gqs")
    rq = head_rms_scale(gq, GQA_W)
    cg4 = jnp.concatenate([cg] * (GQA_W // LANES), axis=1)
    sg4 = jnp.concatenate([sg] * (GQA_W // LANES), axis=1)
    qg = rq * (gq * gq_ref[...] * cg4 + gqs * gqs_ref[...] * sg4)
    qgt = (qg * (GQA_SCALE * LOG2E)).T.astype(BF16)
    zeros_q = jnp.zeros((GQA_HD, tr), BF16)
    for hd in range(GQA_HEADS):
        kvh = hd // (GQA_HEADS // GQA_KV_HEADS)
        qh = qgt[hd * GQA_HD:(hd + 1) * GQA_HD, :]
        qg_ref[hd] = jnp.concatenate([qh, zeros_q] if kvh == 0 else [zeros_q, qh], axis=0)
    gk = proj("gk")
    gks = proj("gks")
    rk = head_rms_scale(gk, GQA_KV_HEADS * GQA_HD)
    kg_ref[0] = (rk * (gk * gk_ref[...] * cg + gks * gks_ref[...] * sg)).astype(BF16)
    vgt = proj("gv").T
    for kvh in range(GQA_KV_HEADS):
        vg_ref[kvh, 0:GQA_HD, :] = vgt[kvh * GQA_HD:(kvh + 1) * GQA_HD, :].astype(BF16)
        vg_ref[kvh, GQA_HD:V_ROWS, :] = v_tail

    u_ref[...] = proj("ssm").astype(BF16)
    gate_ref[...] = _sigmoid(proj("gate")).astype(BF16)


def _premix(rows, mod_l, g_pre, wts, tabs):
    r, d = rows.shape
    nt = r // ROW_TILE
    row_spec = lambda w: pl.BlockSpec((ROW_TILE, w), lambda i: (i, 0))
    full = lambda a: pl.BlockSpec(a.shape, lambda i: (0,) * a.ndim)
    consts = [g_pre, wts["w_cat"], wts["wuq"], wts["wuqs"], wts["wuk"], wts["wuv"],
              wts["g_cq"], wts["g_ckv"], wts["g_q"], wts["g_qs"], wts["g_k"], wts["g_ks"], wts["ones"]]
    t_spec = lambda nh, f: pl.BlockSpec((nh, f, ROW_TILE), lambda i: (0, 0, i))
    k_spec = lambda nh, f: pl.BlockSpec((nh, ROW_TILE, f), lambda i: (0, i, 0))
    outs = [(t_spec(MLA_HEADS, MLA_HEAD_PAD), (MLA_HEADS, MLA_HEAD_PAD, r)),
            (k_spec(MLA_HEADS, MLA_HEAD_PAD), (MLA_HEADS, r, MLA_HEAD_PAD)),
            (t_spec(MLA_HEADS, V_ROWS), (MLA_HEADS, V_ROWS, r)),
            (t_spec(GQA_HEADS, GQA_KV_HEADS * GQA_HD), (GQA_HEADS, GQA_KV_HEADS * GQA_HD, r)),
            (k_spec(1, GQA_KV_HEADS * GQA_HD), (1, r, GQA_KV_HEADS * GQA_HD)),
            (t_spec(GQA_KV_HEADS, V_ROWS), (GQA_KV_HEADS, V_ROWS, r)),
            (row_spec(SSM_W), (r, SSM_W)),
            (row_spec(3 * d), (r, 3 * d))]
    return pl.pallas_call(
        _premix_kernel,
        grid=(nt,),
        in_specs=[row_spec(d), full(mod_l)] + [full(a) for a in consts] + [row_spec(LANES)] * 4,
        out_specs=[s for s, _ in outs],
        out_shape=[jax.ShapeDtypeStruct(shp, BF16) for _, shp in outs],
        compiler_params=_cparams("parallel"),
        name="premix",
    )(rows, mod_l, *consts, tabs["cm"], tabs["sm"], tabs["cg"], tabs["sg"])


def _attn_kernel(q_ref, k_ref, v_ref, o_ref, s_ref, *, tk, nsub, n_ctx, n_iter, dv, hb, k_shared, v_shared,
                 ahead):
    n_items = hb * nsub
    tq = q_ref.shape[2]

    def scores(h, off, size):
        return _dot(k_ref[0 if k_shared else h, pl.ds(off, size), :], q_ref[h])

    def update(h, m, acc, s, off, size):
        m_new = jnp.maximum(m, jnp.max(s, axis=0, keepdims=True))
        p = jnp.exp2(s - m_new).astype(BF16)
        pv = _dot(v_ref[0 if v_shared else h, :, pl.ds(off, size)], p)
        return m_new, jnp.exp2(m - m_new) * acc + pv

    def issue(item, base):
        h, sub = item % hb, item // hb
        s_ref[item] = scores(h, pl.multiple_of(base + sub * tk, LANES), tk)

    m0 = jnp.full((1, tq), -1e30, F32)
    acc0 = jnp.zeros((V_ROWS, tq), F32)
    s_ctx = [scores(h, 0, n_ctx) for h in range(hb)]
    for item in range(ahead):
        issue(item, n_ctx)
    carry = tuple(update(h, m0, acc0, s_ctx[h], 0, n_ctx) for h in range(hb))

    def body(j, carry):
        carry = list(carry)
        base = n_ctx + j * (nsub * tk)
        base_next = n_ctx + jnp.minimum(j + 1, n_iter - 1) * (nsub * tk)
        for item in range(n_items):
            nxt = item + ahead
            if nxt < n_items:
                issue(nxt, base)
            else:
                issue(nxt - n_items, base_next)
            h, sub = item % hb, item // hb
            off = pl.multiple_of(base + sub * tk, LANES)
            carry[h] = update(h, carry[h][0], carry[h][1], s_ref[item], off, tk)
        return tuple(carry)

    carry = lax.fori_loop(0, jnp.where(pl.program_id(1) == 0, 0, n_iter), body, carry)
    for h in range(hb):
        acc = carry[h][1]
        o_ref[h] = acc[:dv] / acc[dv:dv + 1]


def _attention(qt, k, vt, *, hb, nsub, tq=256, tk=512, ahead=2):
    nh, dq, r = qt.shape
    k_shared = k.shape[0] != nh
    v_shared = vt.shape[0] != nh
    assert nh % hb == 0 and k.shape[0] in (nh, 1) and vt.shape[0] in (nh, nh // hb)
    dv = V_ROWS - 16
    assert (r - ROW_TILE) % (tk * nsub) == 0 and ahead < hb * nsub
    kern = functools.partial(_attn_kernel, tk=tk, nsub=nsub, n_ctx=ROW_TILE, n_iter=(r - ROW_TILE) // (tk * nsub),
                             dv=dv, hb=hb, k_shared=k_shared, v_shared=v_shared, ahead=ahead)
    k_spec = (pl.BlockSpec((1, r, dq), lambda g, i: (0, 0, 0)) if k_shared
              else pl.BlockSpec((hb, r, dq), lambda g, i: (g, 0, 0)))
    v_spec = pl.BlockSpec((1 if v_shared else hb, V_ROWS, r), lambda g, i: (g, 0, 0))
    return pl.pallas_call(
        kern,
        grid=(nh // hb, r // tq),
        in_specs=[pl.BlockSpec((hb, dq, tq), lambda g, i: (g, 0, i)), k_spec, v_spec],
        out_specs=pl.BlockSpec((hb, dv, tq), lambda g, i: (g, 0, i)),
        out_shape=jax.ShapeDtypeStruct((nh, dv, r), F32),
        scratch_shapes=[pltpu.VMEM((hb * nsub, tk, tq), F32)],
        compiler_params=_cparams("parallel", "arbitrary"),
        name="attention",
    )(qt, k, vt)


def _s5_kernel(u_ref, blk_ref, bp_ref, cp_ref, ar_ref, ai_ref, y_ref, xr_ref, xi_ref, *, n_ctx_chunks):
    t, j, _ = u_ref.shape
    npair = t // 2
    ns = xr_ref.shape[1]
    row = lax.broadcasted_iota(jnp.int32, (j, ns), 0)

    def run(reverse):
        ucat = jnp.concatenate([u_ref[k] for k in range(t)], axis=1)
        xc = _dot(ucat, bp_ref[...])
        if reverse:
            shift = j - n_ctx_chunks - 1
            keep = row < j - 1
        else:
            shift = 1
            keep = row >= 1
        xr_ref[...] = jnp.where(keep, pltpu.roll(xc[:, :ns], shift, 0), 0.0)
        xi_ref[...] = jnp.where(keep, pltpu.roll(xc[:, ns:], shift, 0), 0.0)
        step = 1
        k = 0
        while step < j:
            ar = ar_ref[k:k + 1, :]
            ai = ai_ref[k:k + 1, :]
            xr = xr_ref[...]
            xi = xi_ref[...]
            if reverse:
                keep = row < j - step
                sr = jnp.where(keep, pltpu.roll(xr, j - step, 0), 0.0)
                si = jnp.where(keep, pltpu.roll(xi, j - step, 0), 0.0)
            else:
                keep = row >= step
                sr = jnp.where(keep, pltpu.roll(xr, step, 0), 0.0)
                si = jnp.where(keep, pltpu.roll(xi, step, 0), 0.0)
            xr_ref[...] = xr + ar * sr - ai * si
            xi_ref[...] = xi + ar * si + ai * sr
            step *= 2
            k += 1
        xin = jnp.concatenate([xr_ref[...], xi_ref[...]], axis=1)
        if reverse:
            xin = pltpu.roll(xin, n_ctx_chunks, 0)
        xin = xin.astype(BF16)
        for po in range(npair):
            y = _dot(xin, cp_ref[:, po * 256:(po + 1) * 256])
            pins = range(po, npair) if reverse else range(0, po + 1)
            for pi_ in pins:
                upair = jnp.concatenate([u_ref[2 * pi_], u_ref[2 * pi_ + 1]], axis=1)
                y += _dot(upair, blk_ref[abs(po - pi_)])
            if reverse:
                y_ref[2 * po] += y[:, :128]
                y_ref[2 * po + 1] += y[:, 128:]
            else:
                y_ref[2 * po] = y[:, :128]
                y_ref[2 * po + 1] = y[:, 128:]

    @pl.when(pl.program_id(1) == 0)
    def _():
        run(False)

    @pl.when(pl.program_id(1) == 1)
    def _():
        run(True)


def _s5_scan(ut, mats, n_ctx_chunks):
    t, j, w = ut.shape
    no = w // LANES
    ns = (LANES // SSM_GROUP) * SSM_STATE
    wspec = lambda *tail: pl.BlockSpec((None, None) + tail, lambda o, d: (d, o) + (0,) * len(tail))
    kern = functools.partial(_s5_kernel, n_ctx_chunks=n_ctx_chunks)
    return pl.pallas_call(
        kern,
        grid=(no, SSM_DIRS),
        in_specs=[pl.BlockSpec((t, j, LANES), lambda o, d: (0, 0, o)),
                  wspec(t // 2, 256, 256), wspec(t * LANES, 2 * ns), wspec(2 * ns, t * LANES),
                  wspec(16, ns), wspec(16, ns)],
        out_specs=pl.BlockSpec((t, j, LANES), lambda o, d: (0, 0, o)),
        out_shape=jax.ShapeDtypeStruct((t, j, w), F32),
        scratch_shapes=[pltpu.VMEM((j, ns), F32)] * 2,
        compiler_params=_cparams("parallel", "arbitrary"),
        name="s5_scan",
    )(ut, mats["blk"], mats["bp"], mats["cp"], mats["ar"], mats["ai"])


def _s5_matrices(a_re, a_im, log_dt, b_re, b_im, c_re, c_im, d_skip):
    t = SSM_CHUNK
    a_re = a_re.astype(F32)
    a_im = a_im.astype(F32)
    dt = jnp.exp(log_dt.astype(F32))[..., None]
    den = a_re * a_re + a_im * a_im

    def lpow(k):
        kf = jnp.asarray(k, F32)
        mag = jnp.exp(a_re * dt * kf)
        return mag * jnp.cos(a_im * dt * kf), mag * jnp.sin(a_im * dt * kf)

    lr, li = lpow(1.0)
    cr = ((lr - 1.0) * a_re + li * a_im) / den
    ci = (li * a_re - (lr - 1.0) * a_im) / den
    bbr = cr[..., None] * b_re - ci[..., None] * b_im
    bbi = cr[..., None] * b_im + ci[..., None] * b_re
    c_re = c_re.astype(F32)
    c_im = c_im.astype(F32)

    ks = jnp.arange(t + 1, dtype=F32)[:, None, None, None]
    pr, pi = lpow(ks)
    lbr = pr[:t, ..., None] * bbr - pi[:t, ..., None] * bbi
    lbi = pr[:t, ..., None] * bbi + pi[:t, ..., None] * bbr
    kern = (jnp.einsum('dgop,kdgpc->dgkoc', c_re, lbr, precision='highest')
            - jnp.einsum('dgop,kdgpc->dgkoc', c_im, lbi, precision='highest'))
    skip = d_skip.astype(F32).reshape(SSM_GROUPS, SSM_GROUP)
    eye = jnp.eye(SSM_GROUP, dtype=F32)
    kern = kern.at[0, :, 0].add(skip[:, :, None] * eye)
    gl = LANES // SSM_GROUP
    no = SSM_GROUPS // gl
    eye_g = jnp.eye(gl, dtype=F32)
    dd = jnp.arange(t // 2)[:, None, None]
    ti = jnp.arange(2)[None, :, None]
    to = jnp.arange(2)[None, None, :]
    lag = jnp.stack([2 * dd + to - ti, 2 * dd + ti - to])
    pick = jax.vmap(lambda kd, ld: jnp.take(kd, jnp.clip(ld, 0, t - 1).reshape(-1), axis=1))(kern, lag)
    pick = pick.reshape(SSM_DIRS, no, gl, t // 2, 2, 2, SSM_GROUP, SSM_GROUP)
    pick = jnp.where((lag >= 0)[:, None, None, :, :, :, None, None], pick, 0.0)
    blk = jnp.einsum('dogDabxy,gh->doDagybhx', pick, eye_g).reshape(SSM_DIRS, no, t // 2, 256, 256)
    def contrib(lb):
        both = jnp.stack([lb[::-1, 0], lb[:, 1]])
        both = both.reshape(SSM_DIRS, t, no, gl, SSM_STATE, SSM_GROUP)
        return jnp.einsum('dtogpc,gh->dotgchp', both, eye_g).reshape(SSM_DIRS, no, t * LANES, gl * SSM_STATE)
    bp = jnp.concatenate([contrib(lbr), contrib(lbi)], axis=-1)
    p1r = pr[1:, :, :, None, :]
    p1i = pi[1:, :, :, None, :]
    rd_r = c_re * p1r - c_im * p1i
    rd_i = -(c_re * p1i + c_im * p1r)
    def readout(rd):
        both = jnp.stack([rd[:, 0], rd[::-1, 1]])
        both = both.reshape(SSM_DIRS, t, no, gl, SSM_GROUP, SSM_STATE)
        return jnp.einsum('dtogcp,gh->dogpthc', both, eye_g).reshape(SSM_DIRS, no, gl * SSM_STATE, t * LANES)
    cp = jnp.concatenate([readout(rd_r), readout(rd_i)], axis=2)
    kk = (t * 2.0 ** jnp.minimum(jnp.arange(16), 10)).astype(F32)[:, None, None, None]
    qr, qi = lpow(kk)
    lanes = lambda q: jnp.transpose(q, (1, 2, 0, 3)).reshape(SSM_DIRS, no, gl, 16, SSM_STATE)
    ar = jnp.transpose(lanes(qr), (0, 1, 3, 2, 4)).reshape(SSM_DIRS, no, 16, gl * SSM_STATE)
    ai = jnp.transpose(lanes(qi), (0, 1, 3, 2, 4)).reshape(SSM_DIRS, no, 16, gl * SSM_STATE)
    return dict(blk=blk.astype(BF16), bp=bp.astype(BF16), cp=cp.astype(BF16), ar=ar, ai=ai)


def _merge_kernel(x_ref, mod_ref, om_ref, og_ref, y_ref, gate_ref,
                  wglu_ref, bglu_ref, wbm_ref, wbs_ref, wbg_ref, wout_ref,
                  gpost_ref, gffn_ref, wrt_hi_ref, wrt_lo_ref, brt_ref,
                  xo_ref, h_ref, cw_ref):
    d = x_ref.shape[1]
    y = y_ref[...]
    y1 = 0.5 * y * (1.0 + jnp.tanh(math.sqrt(2.0 / math.pi) * (y + 0.044715 * (y * y * y))))
    ssm = y1 * _sigmoid(_dot(y1.astype(BF16), wglu_ref[...]) + bglu_ref[...])
    gate = gate_ref[...]
    o_mla = om_ref[...].reshape(-1, om_ref.shape[2]).T.astype(BF16)
    o_gqa = og_ref[...].reshape(-1, og_ref.shape[2]).T.astype(BF16)
    m = (gate[:, 0:d].astype(F32) * _dot(o_mla, wbm_ref[...])
         + gate[:, d:2 * d].astype(F32) * _dot(ssm.astype(BF16), wbs_ref[...])
         + gate[:, 2 * d:3 * d].astype(F32) * _dot(o_gqa, wbg_ref[...]))
    ymix = _dot(m.astype(BF16), wout_ref[...])
    x = x_ref[...] + _mod_row(mod_ref, 2) * _rms(ymix, gpost_ref[...])
    xo_ref[...] = x
    h = _rms(x, gffn_ref[...]) * (1.0 + _mod_row(mod_ref, 4)) + _mod_row(mod_ref, 3)
    h_ref[...] = h.astype(BF16)

    h_hi = h.astype(BF16)
    h_lo = (h - h_hi.astype(F32)).astype(BF16)
    lg = (_dot(h_hi, wrt_hi_ref[...]) + _dot(h_lo, wrt_hi_ref[...]) + _dot(h_hi, wrt_lo_ref[...])
          + brt_ref[...])
    lane = lax.broadcasted_iota(jnp.int32, lg.shape, 1)
    neg = jnp.float32(-1e30)
    is_g = lane < N_GROUPS
    gl = jnp.where(is_g, lg, neg)
    gmax = jnp.max(gl, axis=-1, keepdims=True)
    gsel = jnp.min(jnp.where(is_g & (gl == gmax), lane, LANES), axis=-1, keepdims=True)
    pg = 1.0 / jnp.sum(jnp.where(is_g, jnp.exp(gl - gmax), 0.0), axis=-1, keepdims=True)
    lo = N_GROUPS + gsel * EXPERTS_PER_GROUP
    in_grp = (lane >= lo) & (lane < lo + EXPERTS_PER_GROUP)
    el = jnp.where(in_grp, lg, neg)
    m1 = jnp.max(el, axis=-1, keepdims=True)
    i1 = jnp.min(jnp.where(in_grp & (el == m1), lane, LANES), axis=-1, keepdims=True)
    el2 = jnp.where(lane == i1, neg, el)
    m2 = jnp.max(el2, axis=-1, keepdims=True)
    i2 = jnp.min(jnp.where(in_grp & (el2 == m2), lane, LANES), axis=-1, keepdims=True)
    e2 = jnp.exp(m2 - m1)
    t1 = pg / (1.0 + e2)
    cw_ref[...] = jnp.where(lane == i1, t1, 0.0) + jnp.where(lane == i2, t1 * e2, 0.0)


def _merge(rows, mod_l, o_mla, o_gqa, y_ssm, gates, wts):
    r, d = rows.shape
    nt = r // ROW_TILE
    row_spec = lambda w: pl.BlockSpec((ROW_TILE, w), lambda i: (i, 0))
    full = lambda a: pl.BlockSpec(a.shape, lambda i: (0,) * a.ndim)
    consts = [wts["w_glu"], wts["b_glu"], wts["w_br_mla"], wts["w_br_ssm"], wts["w_br_gqa"], wts["w_out"],
              wts["g_post_mix"], wts["g_pre_ffn"], wts["w_rt_hi"], wts["w_rt_lo"], wts["b_rt"]]
    return pl.pallas_call(
        _merge_kernel,
        grid=(nt,),
        in_specs=[row_spec(d), full(mod_l),
                  pl.BlockSpec(o_mla.shape[:2] + (ROW_TILE,), lambda i: (0, 0, i)),
                  pl.BlockSpec(o_gqa.shape[:2] + (ROW_TILE,), lambda i: (0, 0, i)),
                  row_spec(SSM_W), row_spec(3 * d)] + [full(a) for a in consts],
        out_specs=[row_spec(d), row_spec(d), row_spec(LANES)],
        out_shape=[jax.ShapeDtypeStruct((r, d), F32), jax.ShapeDtypeStruct((r, d), BF16),
                   jax.ShapeDtypeStruct((r, LANES), F32)],
        compiler_params=_cparams("parallel"),
        name="merge",
    )(rows, mod_l, o_mla, o_gqa, y_ssm, gates, *consts)


def _moe_kernel(h_ref, cw_ref, x_ref, mod_ref, wg_ref, wu_ref, wd_ref, gpost_ref, o_ref, acc_ref, *, n_ctx):
    e = pl.program_id(1)

    @pl.when(e == 0)
    def _():
        acc_ref[...] = jnp.zeros_like(acc_ref)

    h = h_ref[...]
    a = _dot(h, wg_ref[...])
    hid = (a * _sigmoid(a)) * _dot(h, wu_ref[...])
    lane = lax.broadcasted_iota(jnp.int32, cw_ref.shape, 1)
    w_e = jnp.sum(jnp.where(lane == e + N_GROUPS, cw_ref[...], 0.0), axis=-1, keepdims=True)
    acc_ref[...] += w_e * _dot(hid.astype(BF16), wd_ref[...])

    @pl.when(e == pl.num_programs(1) - 1)
    def _():
        m = mod_ref[...]
        d = m.shape[1] // N_MOD
        tm = o_ref.shape[0]
        row = pl.program_id(0) * tm + lax.broadcasted_iota(jnp.int32, (tm, 1), 0)
        ga = jnp.where(row < n_ctx, m[1:2, 5 * d:6 * d], m[0:1, 5 * d:6 * d])
        o_ref[...] = x_ref[...] + ga * _rms(acc_ref[...], gpost_ref[...])


def _moe(h, cw, rows, mod_l, wg, wu, wd, g_post, n_ctx):
    r, d = rows.shape
    tm = 1280 if r % 1280 == 0 else ROW_TILE
    ne, _, de = wg.shape
    kern = functools.partial(_moe_kernel, n_ctx=n_ctx)
    return pl.pallas_call(
        kern,
        grid=(r // tm, ne),
        in_specs=[
            pl.BlockSpec((tm, d), lambda i, e: (i, 0)),
            pl.BlockSpec((tm, LANES), lambda i, e: (i, 0)),
            pl.BlockSpec((tm, d), lambda i, e: (i, 0)),
            pl.BlockSpec(mod_l.shape, lambda i, e: (0, 0)),
            pl.BlockSpec((None, d, de), lambda i, e: (e, 0, 0)),
            pl.BlockSpec((None, d, de), lambda i, e: (e, 0, 0)),
            pl.BlockSpec((None, de, d), lambda i, e: (e, 0, 0)),
            pl.BlockSpec(g_post.shape, lambda i, e: (0, 0)),
        ],
        out_specs=pl.BlockSpec((tm, d), lambda i, e: (i, 0)),
        out_shape=jax.ShapeDtypeStruct((r, d), F32),
        scratch_shapes=[pltpu.VMEM((tm, d), F32)],
        compiler_params=_cparams("parallel", "arbitrary"),
        name="moe",
    )(h, cw, rows, mod_l, wg, wu, wd, g_post)


def _rope_tables(seq, n_ctx):
    n_rows = seq // GRID_W

    def pattern(width):
        half = width // 4
        freqs = (np.float32(ROPE_THETA) ** (-np.arange(half, dtype=np.float32) / np.float32(half))).astype(np.float32)
        ar = np.arange(n_rows, dtype=np.float32)[:, None] * freqs[None, :]
        ac = np.arange(GRID_W, dtype=np.float32)[:, None] * freqs[None, :]
        by_row = lambda tab: jnp.broadcast_to(jnp.asarray(tab, F32)[:, None, :], (n_rows, GRID_W, half)).reshape(seq, half)
        by_col = lambda tab: jnp.broadcast_to(jnp.asarray(tab, F32)[None, :, :], (n_rows, GRID_W, half)).reshape(seq, half)
        cr, sr, cc, sc = by_row(np.cos(ar)), by_row(np.sin(ar)), by_col(np.cos(ac)), by_col(np.sin(ac))
        c = jnp.concatenate([cr, cr, cc, cc], axis=1)
        s = jnp.concatenate([-sr, sr, -sc, sc], axis=1)
        return c, s

    c32, s32 = pattern(MLA_ROPE)
    c64, s64 = pattern(GQA_HD)
    one = jnp.ones((seq, 1), F32)
    cm = jnp.concatenate([one * jnp.ones((1, MLA_NOPE)), c32, one * jnp.ones((1, 32))], axis=1)
    sm = jnp.concatenate([jnp.zeros((seq, MLA_NOPE)), s32, jnp.zeros((seq, 32))], axis=1)
    cg = jnp.concatenate([c64, c64], axis=1)
    sg = jnp.concatenate([s64, s64], axis=1)
    ctx_c = jnp.ones((n_ctx, LANES), F32)
    ctx_s = jnp.zeros((n_ctx, LANES), F32)
    cat = lambda a, b: jnp.concatenate([a, b], axis=0)
    return dict(cm=cat(ctx_c, cm), sm=cat(ctx_s, sm), cg=cat(ctx_c, cg), sg=cat(ctx_s, sg))


def _swap_perm(width):
    q = width // 4
    return np.concatenate([np.arange(q, 2 * q), np.arange(0, q), np.arange(3 * q, 4 * q), np.arange(2 * q, 3 * q)])


def _layer_weights(l, p):
    d = p["w_in"].shape[1]
    w_in = p["w_in"][l]
    o = 0
    cq = w_in[:, o:o + MLA_Q_RANK]; o += MLA_Q_RANK
    ckv = w_in[:, o:o + MLA_KV_RANK]; o += MLA_KV_RANK
    kr = w_in[:, o:o + MLA_ROPE]; o += MLA_ROPE
    ssm = w_in[:, o:o + SSM_W]; o += SSM_W
    gq = w_in[:, o:o + GQA_W]; o += GQA_W
    gk = w_in[:, o:o + GQA_KV_HEADS * GQA_HD]; o += GQA_KV_HEADS * GQA_HD
    gv = w_in[:, o:o + GQA_KV_HEADS * GQA_HD]; o += GQA_KV_HEADS * GQA_HD
    gate = w_in[:, o:]
    p32 = _swap_perm(MLA_ROPE)
    p64 = _swap_perm(GQA_HD)
    perm_q = np.concatenate([p64 + GQA_HD * h for h in range(GQA_HEADS)])
    perm_k = np.concatenate([p64 + GQA_HD * h for h in range(GQA_KV_HEADS)])
    z = lambda n: jnp.zeros((d, n), F32)
    kr128 = jnp.concatenate([z(MLA_NOPE), kr, z(32)], axis=1)
    krs128 = jnp.concatenate([z(MLA_NOPE), kr[:, p32], z(32)], axis=1)
    w_cat = jnp.concatenate([cq, ckv, kr128, krs128, ssm, gq, gq[:, perm_q], gk, gk[:, perm_k], gv, gate],
                            axis=1).astype(BF16)

    w_uq = p["w_uq"][l].reshape(MLA_Q_RANK, MLA_HEADS, MLA_NOPE + MLA_ROPE)
    zq = jnp.zeros((MLA_Q_RANK, MLA_HEADS, 32), F32)
    wuq = jnp.concatenate([w_uq, zq], axis=2).reshape(MLA_Q_RANK, -1)
    wuqs = jnp.concatenate([jnp.zeros((MLA_Q_RANK, MLA_HEADS, MLA_NOPE), F32),
                            w_uq[:, :, MLA_NOPE:][:, :, p32], zq], axis=2).reshape(MLA_Q_RANK, -1)
    w_ukv = p["w_ukv"][l].reshape(MLA_KV_RANK, MLA_HEADS, MLA_NOPE + MLA_V)
    wuk = jnp.concatenate([w_ukv[:, :, :MLA_NOPE], jnp.zeros((MLA_KV_RANK, MLA_HEADS, 64), F32)],
                          axis=2).reshape(MLA_KV_RANK, -1)
    wuv = w_ukv[:, :, MLA_NOPE:].reshape(MLA_KV_RANK, -1)

    g_qn = p["g_qn"][l]
    g_kn = p["g_kn"][l]
    blk = np.arange(GQA_W) // GQA_HD
    ones = jnp.asarray((blk[:, None] == blk[None, :]).astype(np.float32), BF16)

    w_rt = jnp.zeros((d, LANES), F32)
    w_rt = w_rt.at[:, :N_GROUPS].set(p["w_group"][l]).at[:, N_GROUPS:N_GROUPS + N_EXPERTS].set(p["w_router"][l])
    w_rt_hi = w_rt.astype(BF16)
    w_rt_lo = (w_rt - w_rt_hi.astype(F32)).astype(BF16)
    b_rt = jnp.zeros((1, LANES), F32)
    b_rt = b_rt.at[0, :N_GROUPS].set(p["b_group"][l]).at[0, N_GROUPS:N_GROUPS + N_EXPERTS].set(p["b_router"][l])

    row = lambda v: v.reshape(1, -1).astype(F32)
    return dict(
        w_cat=w_cat, wuq=wuq.astype(BF16), wuqs=wuqs.astype(BF16), wuk=wuk.astype(BF16), wuv=wuv.astype(BF16),
        g_cq=row(p["g_cq"][l]), g_ckv=row(p["g_ckv"][l]),
        g_q=row(jnp.tile(g_qn, GQA_HEADS)), g_qs=row(jnp.tile(g_qn[p64], GQA_HEADS)),
        g_k=row(jnp.tile(g_kn, GQA_KV_HEADS)), g_ks=row(jnp.tile(g_kn[p64], GQA_KV_HEADS)),
        ones=ones,
        w_glu=p["w_glu"][l].astype(BF16), b_glu=row(p["b_glu"][l]),
        w_br_mla=p["w_br_mla"][l].astype(BF16), w_br_ssm=p["w_br_ssm"][l].astype(BF16),
        w_br_gqa=p["w_br_gqa"][l].astype(BF16), w_out=p["w_out"][l].astype(BF16),
        g_post_mix=row(p["g_post_mix"][l]), g_pre_ffn=row(p["g_pre_ffn"][l]),
        w_rt_hi=w_rt_hi, w_rt_lo=w_rt_lo, b_rt=b_rt,
    )


def kernel(x, c, ctx, c_ctx, w_mod, b_mod, g_pre_mix, g_post_mix, g_pre_ffn, g_post_ffn, w_in, g_cq, g_ckv, w_uq, w_ukv, g_qn, g_kn, ssm_a_re, ssm_a_im, ssm_log_dt, ssm_b_re, ssm_b_im, ssm_c_re, ssm_c_im, ssm_d, w_glu, b_glu, w_br_mla, w_br_ssm, w_br_gqa, w_out, w_group, b_group, w_router, b_router, w_exp_gate, w_exp_up, w_exp_down):
    assert x.shape[0] == 1 and ctx.shape[0] == 1
    seq, d = x.shape[1], x.shape[2]
    n_ctx = ctx.shape[1]
    assert n_ctx == ROW_TILE and seq % (SSM_CHUNK * SSM_SUPER) == 0
    depth = w_in.shape[0]
    params = dict(w_in=w_in, g_cq=g_cq, g_ckv=g_ckv, w_uq=w_uq, w_ukv=w_ukv, g_qn=g_qn, g_kn=g_kn,
                  w_glu=w_glu, b_glu=b_glu, w_br_mla=w_br_mla, w_br_ssm=w_br_ssm, w_br_gqa=w_br_gqa,
                  w_out=w_out, g_post_mix=g_post_mix, g_pre_ffn=g_pre_ffn,
                  w_group=w_group, b_group=b_group, w_router=w_router, b_router=b_router)

    cond8 = jnp.zeros((8, d), F32).at[0].set(c[0]).at[1].set(c_ctx)
    mods = _modulation(cond8, w_mod, b_mod)
    tabs = _rope_tables(seq, n_ctx)
    rows = jnp.concatenate([ctx[0], x[0]], axis=0)
    r = rows.shape[0]

    for l in range(depth):
        wts = _layer_weights(l, params)
        mod_l = mods[l]
        qm, km, vm, qg, kg, vg, u, gates = _premix(rows, mod_l, g_pre_mix[l].reshape(1, d), wts, tabs)

        o_mla = _attention(qm, km, vm, hb=2, nsub=8, ahead=6)
        o_gqa = _attention(qg, kg, vg, hb=GQA_HEADS // GQA_KV_HEADS, nsub=4, ahead=4)

        mats = _s5_matrices(ssm_a_re[l], ssm_a_im[l], ssm_log_dt[l], ssm_b_re[l], ssm_b_im[l],
                            ssm_c_re[l], ssm_c_im[l], ssm_d[l])
        ut = jnp.transpose(u.reshape(r // SSM_CHUNK, SSM_CHUNK, SSM_W), (1, 0, 2))
        yt = _s5_scan(ut, mats, n_ctx // SSM_CHUNK)
        y_ssm = jnp.transpose(yt, (1, 0, 2)).reshape(r, SSM_W)

        rows, h, cw = _merge(rows, mod_l, o_mla, o_gqa, y_ssm, gates, wts)
        rows = _moe(h, cw, rows, mod_l, w_exp_gate[l].astype(BF16), w_exp_up[l].astype(BF16),
                    w_exp_down[l].astype(BF16), g_post_ffn[l].reshape(1, d), n_ctx)

    return rows[n_ctx:].reshape(1, seq, d)
```

```python
import functools
import math

import jax
import jax.numpy as jnp
import numpy as np
from jax import lax
from jax.experimental import pallas as pl
from jax.experimental.pallas import tpu as pltpu

F32 = jnp.float32
BF16 = jnp.bfloat16

GRID_W = 64
ROPE_THETA = 10000.0
NORM_EPS = 1e-6
N_MOD = 6

MLA_HEADS = 8
MLA_NOPE = 64
MLA_ROPE = 32
MLA_V = 64
MLA_Q_RANK = 256
MLA_KV_RANK = 128
MLA_SCALE = (MLA_NOPE + MLA_ROPE) ** -0.5
MLA_HEAD_PAD = 128

SSM_GROUP = 16
SSM_GROUPS = 32
SSM_STATE = 64
SSM_W = SSM_GROUP * SSM_GROUPS
SSM_DIRS = 2
SSM_CHUNK = 16
SSM_SUPER = 16
SSM_CW = SSM_CHUNK * SSM_GROUP

GQA_HEADS = 8
GQA_KV_HEADS = 2
GQA_HD = 64
GQA_W = GQA_HEADS * GQA_HD
GQA_SCALE = GQA_HD ** -0.5
LOG2E = math.log2(math.e)

N_GROUPS = 4
EXPERTS_PER_GROUP = 4
N_EXPERTS = N_GROUPS * EXPERTS_PER_GROUP
D_EXPERT = 512
PAIRS_PER_GROUP = EXPERTS_PER_GROUP * (EXPERTS_PER_GROUP - 1) // 2
MOE_TILE = 256

ROW_TILE = 256
LANES = 128
V_ROWS = 80
VMEM_LIMIT = 56 * 1024 * 1024


def _cparams(*sem):
    return pltpu.CompilerParams(dimension_semantics=sem, vmem_limit_bytes=VMEM_LIMIT)


def _dot(a, b):
    return jnp.dot(a, b, preferred_element_type=F32)


def _dot_f32(a, b):
    a_hi = a.astype(BF16)
    a_lo = (a - a_hi.astype(F32)).astype(BF16)
    return _dot(a_hi, b) + _dot(a_lo, b)


def _rms(x, g):
    return x * lax.rsqrt(jnp.mean(x * x, axis=-1, keepdims=True) + NORM_EPS) * g


def _sigmoid(x):
    return 1.0 / (1.0 + jnp.exp(-x))


def _mod_kernel(cond_ref, w_ref, b_ref, o_ref):
    a = cond_ref[...]
    s = a * _sigmoid(a)
    w = w_ref[...]
    s_hi = s.astype(BF16)
    s_lo = (s - s_hi.astype(F32)).astype(BF16)
    w_hi = w.astype(BF16)
    w_lo = (w - w_hi.astype(F32)).astype(BF16)
    acc = _dot(s_hi, w_hi) + _dot(s_lo, w_hi) + _dot(s_hi, w_lo)
    o_ref[...] = acc + b_ref[...]


def _modulation(cond8, w_mod, b_mod):
    nl, d, n = w_mod.shape
    tn = 1536
    return pl.pallas_call(
        _mod_kernel,
        grid=(nl, n // tn),
        in_specs=[
            pl.BlockSpec((8, d), lambda l, j: (0, 0)),
            pl.BlockSpec((None, d, tn), lambda l, j: (l, 0, j)),
            pl.BlockSpec((None, 1, tn), lambda l, j: (l, 0, j)),
        ],
        out_specs=pl.BlockSpec((None, 8, tn), lambda l, j: (l, 0, j)),
        out_shape=jax.ShapeDtypeStruct((nl, 8, n), F32),
        compiler_params=_cparams("parallel", "parallel"),
        name="modulation",
    )(cond8, w_mod, b_mod.reshape(nl, 1, n))


def _mod_row(mod_ref, idx):
    m = mod_ref[...]
    d = m.shape[1] // N_MOD
    lat = m[0:1, idx * d:(idx + 1) * d]
    ctx = m[1:2, idx * d:(idx + 1) * d]
    return jnp.where(pl.program_id(0) == 0, ctx, lat)


_SEG = {}
_off = 0
for _name, _w in (("cq", 256), ("ckv", 128), ("kr", 128), ("krs", 128), ("ssm", 512), ("gq", 512),
                  ("gqs", 512), ("gk", 128), ("gks", 128), ("gv", 128), ("gate", 3072)):
    _SEG[_name] = (_off, _off + _w)
    _off += _w
W_CAT = _off


def _premix_kernel(x_ref, mod_ref, gpre_ref, w_ref, wuq_ref, wuqs_ref, wuk_ref, wuv_ref,
                   gcq_ref, gckv_ref, gq_ref, gqs_ref, gk_ref, gks_ref, ones_ref,
                   cm_ref, sm_ref, cg_ref, sg_ref,
                   qm_ref, km_ref, vm_ref, qg_ref, kg_ref, vg_ref, u_ref, gate_ref):
    x = x_ref[...]
    h = _rms(x, gpre_ref[...]) * (1.0 + _mod_row(mod_ref, 1)) + _mod_row(mod_ref, 0)
    hb = h.astype(BF16)

    def proj(name):
        a, b = _SEG[name]
        return _dot(hb, w_ref[:, a:b])

    cm = cm_ref[...]
    sm = sm_ref[...]
    cg = cg_ref[...]
    sg = sg_ref[...]

    cqn = _rms(proj("cq"), gcq_ref[...]).astype(BF16)
    ckvn = _rms(proj("ckv"), gckv_ref[...]).astype(BF16)
    q = _dot(cqn, wuq_ref[...])
    qs = _dot(cqn, wuqs_ref[...])
    kn = _dot(ckvn, wuk_ref[...])
    kr = proj("kr") * cm + proj("krs") * sm
    tr = x.shape[0]
    vrow = lax.broadcasted_iota(jnp.int32, (V_ROWS - MLA_V, tr), 0)
    v_tail = jnp.where(vrow == 0, 1.0, 0.0).astype(BF16)
    for hd in range(MLA_HEADS):
        sl = slice(hd * MLA_HEAD_PAD, (hd + 1) * MLA_HEAD_PAD)
        qh = (q[:, sl] * cm + qs[:, sl] * sm) * (MLA_SCALE * LOG2E)
        qm_ref[hd] = qh.T.astype(BF16)
        km_ref[hd] = (kn[:, sl] + kr).astype(BF16)
    vmt = _dot(ckvn, wuv_ref[...]).T
    for hd in range(MLA_HEADS):
        vm_ref[hd, 0:MLA_V, :] = vmt[hd * MLA_V:(hd + 1) * MLA_V, :].astype(BF16)
        vm_ref[hd, MLA_V:V_ROWS, :] = v_tail

    ones = ones_ref[...]

    def head_rms_scale(v, width):
        ms = _dot_f32(v * v, ones[:width, :width]) * (1.0 / GQA_HD)
        return lax.rsqrt(ms + NORM_EPS)

    gq = proj("gq")
    gqs = proj("gqs")
    rq = head_rms_scale(gq, GQA_W)
    cg4 = jnp.concatenate([cg] * (GQA_W // LANES), axis=1)
    sg4 = jnp.concatenate([sg] * (GQA_W // LANES), axis=1)
    qg = rq * (gq * gq_ref[...] * cg4 + gqs * gqs_ref[...] * sg4)
    qgt = (qg * (GQA_SCALE * LOG2E)).T.astype(BF16)
    zeros_q = jnp.zeros((GQA_HD, tr), BF16)
    for hd in range(GQA_HEADS):
        kvh = hd // (GQA_HEADS // GQA_KV_HEADS)
        qh = qgt[hd * GQA_HD:(hd + 1) * GQA_HD, :]
        qg_ref[hd] = jnp.concatenate([qh, zeros_q] if kvh == 0 else [zeros_q, qh], axis=0)
    gk = proj("gk")
    gks = proj("gks")
    rk = head_rms_scale(gk, GQA_KV_HEADS * GQA_HD)
    kg_ref[0] = (rk * (gk * gk_ref[...] * cg + gks * gks_ref[...] * sg)).astype(BF16)
    vgt = proj("gv").T
    for kvh in range(GQA_KV_HEADS):
        vg_ref[kvh, 0:GQA_HD, :] = vgt[kvh * GQA_HD:(kvh + 1) * GQA_HD, :].astype(BF16)
        vg_ref[kvh, GQA_HD:V_ROWS, :] = v_tail

    u_ref[...] = proj("ssm").astype(BF16)
    gate_ref[...] = _sigmoid(proj("gate")).astype(BF16)


def _premix(rows, mod_l, g_pre, wts, tabs):
    r, d = rows.shape
    nt = r // ROW_TILE
    row_spec = lambda w: pl.BlockSpec((ROW_TILE, w), lambda i: (i, 0))
    full = lambda a: pl.BlockSpec(a.shape, lambda i: (0,) * a.ndim)
    consts = [g_pre, wts["w_cat"], wts["wuq"], wts["wuqs"], wts["wuk"], wts["wuv"],
              wts["g_cq"], wts["g_ckv"], wts["g_q"], wts["g_qs"], wts["g_k"], wts["g_ks"], wts["ones"]]
    t_spec = lambda nh, f: pl.BlockSpec((nh, f, ROW_TILE), lambda i: (0, 0, i))
    k_spec = lambda nh, f: pl.BlockSpec((nh, ROW_TILE, f), lambda i: (0, i, 0))
    outs = [(t_spec(MLA_HEADS, MLA_HEAD_PAD), (MLA_HEADS, MLA_HEAD_PAD, r)),
            (k_spec(MLA_HEADS, MLA_HEAD_PAD), (MLA_HEADS, r, MLA_HEAD_PAD)),
            (t_spec(MLA_HEADS, V_ROWS), (MLA_HEADS, V_ROWS, r)),
            (t_spec(GQA_HEADS, GQA_KV_HEADS * GQA_HD), (GQA_HEADS, GQA_KV_HEADS * GQA_HD, r)),
            (k_spec(1, GQA_KV_HEADS * GQA_HD), (1, r, GQA_KV_HEADS * GQA_HD)),
            (t_spec(GQA_KV_HEADS, V_ROWS), (GQA_KV_HEADS, V_ROWS, r)),
            (row_spec(SSM_W), (r, SSM_W)),
            (row_spec(3 * d), (r, 3 * d))]
    return pl.pallas_call(
        _premix_kernel,
        grid=(nt,),
        in_specs=[row_spec(d), full(mod_l)] + [full(a) for a in consts] + [row_spec(LANES)] * 4,
        out_specs=[s for s, _ in outs],
        out_shape=[jax.ShapeDtypeStruct(shp, BF16) for _, shp in outs],
        compiler_params=_cparams("parallel"),
        name="premix",
    )(rows, mod_l, *consts, tabs["cm"], tabs["sm"], tabs["cg"], tabs["sg"])


def _attn_kernel(q_ref, k_ref, v_ref, o_ref, s_ref, *, tk, nsub, n_ctx, n_iter, dv, hb, k_shared, v_shared,
                 ahead):
    n_items = hb * nsub
    tq = q_ref.shape[2]

    def scores(h, off, size):
        return _dot(k_ref[0 if k_shared else h, pl.ds(off, size), :], q_ref[h])

    def update(h, m, acc, s, off, size):
        m_new = jnp.maximum(m, jnp.max(s, axis=0, keepdims=True))
        p = jnp.exp2(s - m_new).astype(BF16)
        pv = _dot(v_ref[0 if v_shared else h, :, pl.ds(off, size)], p)
        return m_new, jnp.exp2(m - m_new) * acc + pv

    def issue(item, base):
        h, sub = item % hb, item // hb
        s_ref[item] = scores(h, pl.multiple_of(base + sub * tk, LANES), tk)

    m0 = jnp.full((1, tq), -1e30, F32)
    acc0 = jnp.zeros((V_ROWS, tq), F32)
    s_ctx = [scores(h, 0, n_ctx) for h in range(hb)]
    for item in range(ahead):
        issue(item, n_ctx)
    carry = tuple(update(h, m0, acc0, s_ctx[h], 0, n_ctx) for h in range(hb))

    def body(j, carry):
        carry = list(carry)
        base = n_ctx + j * (nsub * tk)
        base_next = n_ctx + jnp.minimum(j + 1, n_iter - 1) * (nsub * tk)
        for item in range(n_items):
            nxt = item + ahead
            if nxt < n_items:
                issue(nxt, base)
            else:
                issue(nxt - n_items, base_next)
            h, sub = item % hb, item // hb
            off = pl.multiple_of(base + sub * tk, LANES)
            carry[h] = update(h, carry[h][0], carry[h][1], s_ref[item], off, tk)
        return tuple(carry)

    carry = lax.fori_loop(0, jnp.where(pl.program_id(1) == 0, 0, n_iter), body, carry)
    for h in range(hb):
        acc = carry[h][1]
        o_ref[h] = acc[:dv] / acc[dv:dv + 1]


def _attention(qt, k, vt, *, hb, nsub, tq=256, tk=512, ahead=2):
    nh, dq, r = qt.shape
    k_shared = k.shape[0] != nh
    v_shared = vt.shape[0] != nh
    assert nh % hb == 0 and k.shape[0] in (nh, 1) and vt.shape[0] in (nh, nh // hb)
    dv = V_ROWS - 16
    assert (r - ROW_TILE) % (tk * nsub) == 0 and ahead < hb * nsub
    kern = functools.partial(_attn_kernel, tk=tk, nsub=nsub, n_ctx=ROW_TILE, n_iter=(r - ROW_TILE) // (tk * nsub),
                             dv=dv, hb=hb, k_shared=k_shared, v_shared=v_shared, ahead=ahead)
    k_spec = (pl.BlockSpec((1, r, dq), lambda g, i: (0, 0, 0)) if k_shared
              else pl.BlockSpec((hb, r, dq), lambda g, i: (g, 0, 0)))
    v_spec = pl.BlockSpec((1 if v_shared else hb, V_ROWS, r), lambda g, i: (g, 0, 0))
    return pl.pallas_call(
        kern,
        grid=(nh // hb, r // tq),
        in_specs=[pl.BlockSpec((hb, dq, tq), lambda g, i: (g, 0, i)), k_spec, v_spec],
        out_specs=pl.BlockSpec((hb, dv, tq), lambda g, i: (g, 0, i)),
        out_shape=jax.ShapeDtypeStruct((nh, dv, r), F32),
        scratch_shapes=[pltpu.VMEM((hb * nsub, tk, tq), F32)],
        compiler_params=_cparams("parallel", "arbitrary"),
        name="attention",
    )(qt, k, vt)


def _s5_kernel(u_ref, blk_ref, bp_ref, cp_ref, ar_ref, ai_ref, y_ref, xr_ref, xi_ref, *, n_ctx_chunks):
    t, j, _ = u_ref.shape
    npair = t // 2
    ns = xr_ref.shape[1]
    row = lax.broadcasted_iota(jnp.int32, (j, ns), 0)

    def run(reverse):
        ucat = jnp.concatenate([u_ref[k] for k in range(t)], axis=1)
        xc = _dot(ucat, bp_ref[...])
        if reverse:
            shift = j - n_ctx_chunks - 1
            keep = row < j - 1
        else:
            shift = 1
            keep = row >= 1
        xr_ref[...] = jnp.where(keep, pltpu.roll(xc[:, :ns], shift, 0), 0.0)
        xi_ref[...] = jnp.where(keep, pltpu.roll(xc[:, ns:], shift, 0), 0.0)
        step = 1
        k = 0
        while step < j:
            ar = ar_ref[k:k + 1, :]
            ai = ai_ref[k:k + 1, :]
            xr = xr_ref[...]
            xi = xi_ref[...]
            if reverse:
                keep = row < j - step
                sr = jnp.where(keep, pltpu.roll(xr, j - step, 0), 0.0)
                si = jnp.where(keep, pltpu.roll(xi, j - step, 0), 0.0)
            else:
                keep = row >= step
                sr = jnp.where(keep, pltpu.roll(xr, step, 0), 0.0)
                si = jnp.where(keep, pltpu.roll(xi, step, 0), 0.0)
            xr_ref[...] = xr + ar * sr - ai * si
            xi_ref[...] = xi + ar * si + ai * sr
            step *= 2
            k += 1
        xin = jnp.concatenate([xr_ref[...], xi_ref[...]], axis=1)
        if reverse:
            xin = pltpu.roll(xin, n_ctx_chunks, 0)
        xin = xin.astype(BF16)
        for po in range(npair):
            y = _dot(xin, cp_ref[:, po * 256:(po + 1) * 256])
            pins = range(po, npair) if reverse else range(0, po + 1)
            for pi_ in pins:
                upair = jnp.concatenate([u_ref[2 * pi_], u_ref[2 * pi_ + 1]], axis=1)
                y += _dot(upair, blk_ref[abs(po - pi_)])
            if reverse:
                y_ref[2 * po] += y[:, :128]
                y_ref[2 * po + 1] += y[:, 128:]
            else:
                y_ref[2 * po] = y[:, :128]
                y_ref[2 * po + 1] = y[:, 128:]

    @pl.when(pl.program_id(1) == 0)
    def _():
        run(False)

    @pl.when(pl.program_id(1) == 1)
    def _():
        run(True)


def _s5_scan(ut, mats, n_ctx_chunks):
    t, j, w = ut.shape
    no = w // LANES
    ns = (LANES // SSM_GROUP) * SSM_STATE
    wspec = lambda *tail: pl.BlockSpec((None, None) + tail, lambda o, d: (d, o) + (0,) * len(tail))
    kern = functools.partial(_s5_kernel, n_ctx_chunks=n_ctx_chunks)
    return pl.pallas_call(
        kern,
        grid=(no, SSM_DIRS),
        in_specs=[pl.BlockSpec((t, j, LANES), lambda o, d: (0, 0, o)),
                  wspec(t // 2, 256, 256), wspec(t * LANES, 2 * ns), wspec(2 * ns, t * LANES),
                  wspec(16, ns), wspec(16, ns)],
        out_specs=pl.BlockSpec((t, j, LANES), lambda o, d: (0, 0, o)),
        out_shape=jax.ShapeDtypeStruct((t, j, w), F32),
        scratch_shapes=[pltpu.VMEM((j, ns), F32)] * 2,
        compiler_params=_cparams("parallel", "arbitrary"),
        name="s5_scan",
    )(ut, mats["blk"], mats["bp"], mats["cp"], mats["ar"], mats["ai"])


def _s5_matrices(a_re, a_im, log_dt, b_re, b_im, c_re, c_im, d_skip):
    t = SSM_CHUNK
    a_re = a_re.astype(F32)
    a_im = a_im.astype(F32)
    dt = jnp.exp(log_dt.astype(F32))[..., None]
    den = a_re * a_re + a_im * a_im

    def lpow(k):
        kf = jnp.asarray(k, F32)
        mag = jnp.exp(a_re * dt * kf)
        return mag * jnp.cos(a_im * dt * kf), mag * jnp.sin(a_im * dt * kf)

    lr, li = lpow(1.0)
    cr = ((lr - 1.0) * a_re + li * a_im) / den
    ci = (li * a_re - (lr - 1.0) * a_im) / den
    bbr = cr[..., None] * b_re - ci[..., None] * b_im
    bbi = cr[..., None] * b_im + ci[..., None] * b_re
    c_re = c_re.astype(F32)
    c_im = c_im.astype(F32)

    ks = jnp.arange(t + 1, dtype=F32)[:, None, None, None]
    pr, pi = lpow(ks)
    lbr = pr[:t, ..., None] * bbr - pi[:t, ..., None] * bbi
    lbi = pr[:t, ..., None] * bbi + pi[:t, ..., None] * bbr
    kern = (jnp.einsum('dgop,kdgpc->dgkoc', c_re, lbr, precision='highest')
            - jnp.einsum('dgop,kdgpc->dgkoc', c_im, lbi, precision='highest'))
    skip = d_skip.astype(F32).reshape(SSM_GROUPS, SSM_GROUP)
    eye = jnp.eye(SSM_GROUP, dtype=F32)
    kern = kern.at[0, :, 0].add(skip[:, :, None] * eye)
    gl = LANES // SSM_GROUP
    no = SSM_GROUPS // gl
    eye_g = jnp.eye(gl, dtype=F32)
    dd = jnp.arange(t // 2)[:, None, None]
    ti = jnp.arange(2)[None, :, None]
    to = jnp.arange(2)[None, None, :]
    lag = jnp.stack([2 * dd + to - ti, 2 * dd + ti - to])
    pick = jax.vmap(lambda kd, ld: jnp.take(kd, jnp.clip(ld, 0, t - 1).reshape(-1), axis=1))(kern, lag)
    pick = pick.reshape(SSM_DIRS, no, gl, t // 2, 2, 2, SSM_GROUP, SSM_GROUP)
    pick = jnp.where((lag >= 0)[:, None, None, :, :, :, None, None], pick, 0.0)
    blk = jnp.einsum('dogDabxy,gh->doDagybhx', pick, eye_g).reshape(SSM_DIRS, no, t // 2, 256, 256)
    def contrib(lb):
        both = jnp.stack([lb[::-1, 0], lb[:, 1]])
        both = both.reshape(SSM_DIRS, t, no, gl, SSM_STATE, SSM_GROUP)
        return jnp.einsum('dtogpc,gh->dotgchp', both, eye_g).reshape(SSM_DIRS, no, t * LANES, gl * SSM_STATE)
    bp = jnp.concatenate([contrib(lbr), contrib(lbi)], axis=-1)
    p1r = pr[1:, :, :, None, :]
    p1i = pi[1:, :, :, None, :]
    rd_r = c_re * p1r - c_im * p1i
    rd_i = -(c_re * p1i + c_im * p1r)
    def readout(rd):
        both = jnp.stack([rd[:, 0], rd[::-1, 1]])
        both = both.reshape(SSM_DIRS, t, no, gl, SSM_GROUP, SSM_STATE)
        return jnp.einsum('dtogcp,gh->dogpthc', both, eye_g).reshape(SSM_DIRS, no, gl * SSM_STATE, t * LANES)
    cp = jnp.concatenate([readout(rd_r), readout(rd_i)], axis=2)
    kk = (t * 2.0 ** jnp.minimum(jnp.arange(16), 10)).astype(F32)[:, None, None, None]
    qr, qi = lpow(kk)
    lanes = lambda q: jnp.transpose(q, (1, 2, 0, 3)).reshape(SSM_DIRS, no, gl, 16, SSM_STATE)
    ar = jnp.transpose(lanes(qr), (0, 1, 3, 2, 4)).reshape(SSM_DIRS, no, 16, gl * SSM_STATE)
    ai = jnp.transpose(lanes(qi), (0, 1, 3, 2, 4)).reshape(SSM_DIRS, no, 16, gl * SSM_STATE)
    return dict(blk=blk.astype(BF16), bp=bp.astype(BF16), cp=cp.astype(BF16), ar=ar, ai=ai)


def _merge_kernel(x_ref, mod_ref, om_ref, og_ref, y_ref, gate_ref,
                  wglu_ref, bglu_ref, wbm_ref, wbs_ref, wbg_ref, wout_ref,
                  gpost_ref, gffn_ref, wrt_hi_ref, wrt_lo_ref, brt_ref,
                  xo_ref, h_ref, cw_ref):
    d = x_ref.shape[1]
    y = y_ref[...]
    y1 = 0.5 * y * (1.0 + jnp.tanh(math.sqrt(2.0 / math.pi) * (y + 0.044715 * (y * y * y))))
    ssm = y1 * _sigmoid(_dot(y1.astype(BF16), wglu_ref[...]) + bglu_ref[...])
    gate = gate_ref[...]
    o_mla = om_ref[...].reshape(-1, om_ref.shape[2]).T.astype(BF16)
    o_gqa = og_ref[...].reshape(-1, og_ref.shape[2]).T.astype(BF16)
    m = (gate[:, 0:d].astype(F32) * _dot(o_mla, wbm_ref[...])
         + gate[:, d:2 * d].astype(F32) * _dot(ssm.astype(BF16), wbs_ref[...])
         + gate[:, 2 * d:3 * d].astype(F32) * _dot(o_gqa, wbg_ref[...]))
    ymix = _dot(m.astype(BF16), wout_ref[...])
    x = x_ref[...] + _mod_row(mod_ref, 2) * _rms(ymix, gpost_ref[...])
    xo_ref[...] = x
    h = _rms(x, gffn_ref[...]) * (1.0 + _mod_row(mod_ref, 4)) + _mod_row(mod_ref, 3)
    h_ref[...] = h.astype(BF16)

    h_hi = h.astype(BF16)
    h_lo = (h - h_hi.astype(F32)).astype(BF16)
    lg = (_dot(h_hi, wrt_hi_ref[...]) + _dot(h_lo, wrt_hi_ref[...]) + _dot(h_hi, wrt_lo_ref[...])
          + brt_ref[...])
    lane = lax.broadcasted_iota(jnp.int32, lg.shape, 1)
    neg = jnp.float32(-1e30)
    is_g = lane < N_GROUPS
    gl = jnp.where(is_g, lg, neg)
    gmax = jnp.max(gl, axis=-1, keepdims=True)
    gsel = jnp.min(jnp.where(is_g & (gl == gmax), lane, LANES), axis=-1, keepdims=True)
    pg = 1.0 / jnp.sum(jnp.where(is_g, jnp.exp(gl - gmax), 0.0), axis=-1, keepdims=True)
    lo = N_GROUPS + gsel * EXPERTS_PER_GROUP
    in_grp = (lane >= lo) & (lane < lo + EXPERTS_PER_GROUP)
    el = jnp.where(in_grp, lg, neg)
    m1 = jnp.max(el, axis=-1, keepdims=True)
    i1 = jnp.min(jnp.where(in_grp & (el == m1), lane, LANES), axis=-1, keepdims=True)
    el2 = jnp.where(lane == i1, neg, el)
    m2 = jnp.max(el2, axis=-1, keepdims=True)
    i2 = jnp.min(jnp.where(in_grp & (el2 == m2), lane, LANES), axis=-1, keepdims=True)
    e2 = jnp.exp(m2 - m1)
    t1 = pg / (1.0 + e2)
    first = i1 < i2
    ea = jnp.minimum(i1, i2) - lo
    eb = jnp.maximum(i1, i2) - lo
    bucket = gsel * PAIRS_PER_GROUP + jnp.right_shift(ea * (7 - ea), 1) + (eb - ea - 1)
    w_a = jnp.where(first, t1, t1 * e2)
    w_b = jnp.where(first, t1 * e2, t1)
    cw_ref[...] = (jnp.where(lane == 0, bucket.astype(F32), 0.0) + jnp.where(lane == 1, w_a, 0.0)
                   + jnp.where(lane == 2, w_b, 0.0))


def _merge(rows, mod_l, o_mla, o_gqa, y_ssm, gates, wts):
    r, d = rows.shape
    nt = r // ROW_TILE
    row_spec = lambda w: pl.BlockSpec((ROW_TILE, w), lambda i: (i, 0))
    full = lambda a: pl.BlockSpec(a.shape, lambda i: (0,) * a.ndim)
    consts = [wts["w_glu"], wts["b_glu"], wts["w_br_mla"], wts["w_br_ssm"], wts["w_br_gqa"], wts["w_out"],
              wts["g_post_mix"], wts["g_pre_ffn"], wts["w_rt_hi"], wts["w_rt_lo"], wts["b_rt"]]
    return pl.pallas_call(
        _merge_kernel,
        grid=(nt,),
        in_specs=[row_spec(d), full(mod_l),
                  pl.BlockSpec(o_mla.shape[:2] + (ROW_TILE,), lambda i: (0, 0, i)),
                  pl.BlockSpec(o_gqa.shape[:2] + (ROW_TILE,), lambda i: (0, 0, i)),
                  row_spec(SSM_W), row_spec(3 * d)] + [full(a) for a in consts],
        out_specs=[row_spec(d), row_spec(d), row_spec(LANES)],
        out_shape=[jax.ShapeDtypeStruct((r, d), F32), jax.ShapeDtypeStruct((r, d), BF16),
                   jax.ShapeDtypeStruct((r, LANES), F32)],
        compiler_params=_cparams("parallel"),
        name="merge",
    )(rows, mod_l, o_mla, o_gqa, y_ssm, gates, *consts)


def _moe_kernel(ea_ref, eb_ref, used_ref, h_ref, rt_ref, wga_ref, wua_ref, wda_ref, wgb_ref, wub_ref, wdb_ref,
                gpost_ref, o_ref):
    i = pl.program_id(0)

    @pl.when(used_ref[i] == 1)
    def _():
        h = h_ref[...]
        rt = rt_ref[...]
        lane = lax.broadcasted_iota(jnp.int32, rt.shape, 1)
        w_a = jnp.sum(jnp.where(lane == 1, rt, 0.0), axis=-1, keepdims=True)
        w_b = jnp.sum(jnp.where(lane == 2, rt, 0.0), axis=-1, keepdims=True)

        def expert(wg_ref, wu_ref, wd_ref):
            a = _dot(h, wg_ref[...])
            hid = (a * _sigmoid(a)) * _dot(h, wu_ref[...])
            return _dot(hid.astype(BF16), wd_ref[...])

        y = w_a * expert(wga_ref, wua_ref, wda_ref) + w_b * expert(wgb_ref, wub_ref, wdb_ref)
        o_ref[...] = _rms(y, gpost_ref[...]).astype(o_ref.dtype)

    @pl.when(used_ref[i] == 0)
    def _():
        o_ref[...] = jnp.zeros_like(o_ref)


def _moe(h_sorted, rt_sorted, tile_ea, tile_eb, tile_used, wg, wu, wd, g_post):
    p, d = h_sorted.shape
    ne, _, de = wg.shape
    nt = p // MOE_TILE
    w_in = lambda sel: pl.BlockSpec((None, d, de), lambda i, ea, eb, used: ((ea, eb)[sel][i], 0, 0))
    w_out = lambda sel: pl.BlockSpec((None, de, d), lambda i, ea, eb, used: ((ea, eb)[sel][i], 0, 0))
    return pl.pallas_call(
        _moe_kernel,
        grid_spec=pltpu.PrefetchScalarGridSpec(
            num_scalar_prefetch=3,
            grid=(nt,),
            in_specs=[
                pl.BlockSpec((MOE_TILE, d), lambda i, ea, eb, used: (i, 0)),
                pl.BlockSpec((MOE_TILE, LANES), lambda i, ea, eb, used: (i, 0)),
                w_in(0), w_in(0), w_out(0), w_in(1), w_in(1), w_out(1),
                pl.BlockSpec(g_post.shape, lambda i, ea, eb, used: (0, 0)),
            ],
            out_specs=pl.BlockSpec((MOE_TILE, d), lambda i, ea, eb, used: (i, 0)),
        ),
        out_shape=jax.ShapeDtypeStruct((p, d), BF16),
        compiler_params=_cparams("arbitrary"),
        name="moe",
    )(tile_ea, tile_eb, tile_used, h_sorted, rt_sorted, wg, wu, wd, wg, wu, wd, g_post)


def _dispatch(bucket, n_rows):
    nb = N_GROUPS * PAIRS_PER_GROUP
    n_tiles = n_rows // MOE_TILE + nb
    onehot = (bucket[:, None] == jnp.arange(nb, dtype=jnp.int32)[None, :]).astype(jnp.int32)
    counts = jnp.sum(onehot, axis=0)
    rank = jnp.sum((jnp.cumsum(onehot, axis=0) - onehot) * onehot, axis=1)
    tiles_per = (counts + MOE_TILE - 1) // MOE_TILE
    tile_end = jnp.cumsum(tiles_per)
    start = (tile_end - tiles_per) * MOE_TILE
    pos = jnp.take(start, bucket) + rank
    src = jnp.zeros((n_tiles * MOE_TILE,), jnp.int32).at[pos].set(jnp.arange(n_rows, dtype=jnp.int32))
    tile_id = jnp.arange(n_tiles, dtype=jnp.int32)
    tile_bucket = jnp.minimum(jnp.searchsorted(tile_end, tile_id, side='right'), nb - 1).astype(jnp.int32)
    tile_used = (tile_id < tile_end[-1]).astype(jnp.int32)
    last_bucket = jnp.take(tile_bucket, jnp.maximum(tile_end[-1] - 1, 0))
    tile_bucket = jnp.where(tile_used == 1, tile_bucket, last_bucket)
    pair_a = jnp.asarray([0, 0, 0, 1, 1, 2], jnp.int32)
    pair_b = jnp.asarray([1, 2, 3, 2, 3, 3], jnp.int32)
    grp = tile_bucket // PAIRS_PER_GROUP
    tile_ea = grp * EXPERTS_PER_GROUP + jnp.take(pair_a, tile_bucket % PAIRS_PER_GROUP)
    tile_eb = grp * EXPERTS_PER_GROUP + jnp.take(pair_b, tile_bucket % PAIRS_PER_GROUP)
    return pos, src, tile_ea, tile_eb, tile_used


def _residual_kernel(x_ref, z_ref, mod_ref, o_ref):
    o_ref[...] = x_ref[...] + _mod_row(mod_ref, 5) * z_ref[...].astype(F32)


def _residual(rows, z, mod_l):
    r, d = rows.shape
    row_spec = pl.BlockSpec((ROW_TILE, d), lambda i: (i, 0))
    return pl.pallas_call(
        _residual_kernel,
        grid=(r // ROW_TILE,),
        in_specs=[row_spec, row_spec, pl.BlockSpec(mod_l.shape, lambda i: (0, 0))],
        out_specs=row_spec,
        out_shape=jax.ShapeDtypeStruct((r, d), F32),
        compiler_params=_cparams("parallel"),
        name="residual",
    )(rows, z, mod_l)


def _rope_tables(seq, n_ctx):
    n_rows = seq // GRID_W

    def pattern(width):
        half = width // 4
        freqs = (np.float32(ROPE_THETA) ** (-np.arange(half, dtype=np.float32) / np.float32(half))).astype(np.float32)
        ar = np.arange(n_rows, dtype=np.float32)[:, None] * freqs[None, :]
        ac = np.arange(GRID_W, dtype=np.float32)[:, None] * freqs[None, :]
        by_row = lambda tab: jnp.broadcast_to(jnp.asarray(tab, F32)[:, None, :], (n_rows, GRID_W, half)).reshape(seq, half)
        by_col = lambda tab: jnp.broadcast_to(jnp.asarray(tab, F32)[None, :, :], (n_rows, GRID_W, half)).reshape(seq, half)
        cr, sr, cc, sc = by_row(np.cos(ar)), by_row(np.sin(ar)), by_col(np.cos(ac)), by_col(np.sin(ac))
        c = jnp.concatenate([cr, cr, cc, cc], axis=1)
        s = jnp.concatenate([-sr, sr, -sc, sc], axis=1)
        return c, s

    c32, s32 = pattern(MLA_ROPE)
    c64, s64 = pattern(GQA_HD)
    one = jnp.ones((seq, 1), F32)
    cm = jnp.concatenate([one * jnp.ones((1, MLA_NOPE)), c32, one * jnp.ones((1, 32))], axis=1)
    sm = jnp.concatenate([jnp.zeros((seq, MLA_NOPE)), s32, jnp.zeros((seq, 32))], axis=1)
    cg = jnp.concatenate([c64, c64], axis=1)
    sg = jnp.concatenate([s64, s64], axis=1)
    ctx_c = jnp.ones((n_ctx, LANES), F32)
    ctx_s = jnp.zeros((n_ctx, LANES), F32)
    cat = lambda a, b: jnp.concatenate([a, b], axis=0)
    return dict(cm=cat(ctx_c, cm), sm=cat(ctx_s, sm), cg=cat(ctx_c, cg), sg=cat(ctx_s, sg))


def _swap_perm(width):
    q = width // 4
    return np.concatenate([np.arange(q, 2 * q), np.arange(0, q), np.arange(3 * q, 4 * q), np.arange(2 * q, 3 * q)])


def _layer_weights(l, p):
    d = p["w_in"].shape[1]
    w_in = p["w_in"][l]
    o = 0
    cq = w_in[:, o:o + MLA_Q_RANK]; o += MLA_Q_RANK
    ckv = w_in[:, o:o + MLA_KV_RANK]; o += MLA_KV_RANK
    kr = w_in[:, o:o + MLA_ROPE]; o += MLA_ROPE
    ssm = w_in[:, o:o + SSM_W]; o += SSM_W
    gq = w_in[:, o:o + GQA_W]; o += GQA_W
    gk = w_in[:, o:o + GQA_KV_HEADS * GQA_HD]; o += GQA_KV_HEADS * GQA_HD
    gv = w_in[:, o:o + GQA_KV_HEADS * GQA_HD]; o += GQA_KV_HEADS * GQA_HD
    gate = w_in[:, o:]
    p32 = _swap_perm(MLA_ROPE)
    p64 = _swap_perm(GQA_HD)
    perm_q = np.concatenate([p64 + GQA_HD * h for h in range(GQA_HEADS)])
    perm_k = np.concatenate([p64 + GQA_HD * h for h in range(GQA_KV_HEADS)])
    z = lambda n: jnp.zeros((d, n), F32)
    kr128 = jnp.concatenate([z(MLA_NOPE), kr, z(32)], axis=1)
    krs128 = jnp.concatenate([z(MLA_NOPE), kr[:, p32], z(32)], axis=1)
    w_cat = jnp.concatenate([cq, ckv, kr128, krs128, ssm, gq, gq[:, perm_q], gk, gk[:, perm_k], gv, gate],
                            axis=1).astype(BF16)

    w_uq = p["w_uq"][l].reshape(MLA_Q_RANK, MLA_HEADS, MLA_NOPE + MLA_ROPE)
    zq = jnp.zeros((MLA_Q_RANK, MLA_HEADS, 32), F32)
    wuq = jnp.concatenate([w_uq, zq], axis=2).reshape(MLA_Q_RANK, -1)
    wuqs = jnp.concatenate([jnp.zeros((MLA_Q_RANK, MLA_HEADS, MLA_NOPE), F32),
                            w_uq[:, :, MLA_NOPE:][:, :, p32], zq], axis=2).reshape(MLA_Q_RANK, -1)
    w_ukv = p["w_ukv"][l].reshape(MLA_KV_RANK, MLA_HEADS, MLA_NOPE + MLA_V)
    wuk = jnp.concatenate([w_ukv[:, :, :MLA_NOPE], jnp.zeros((MLA_KV_RANK, MLA_HEADS, 64), F32)],
                          axis=2).reshape(MLA_KV_RANK, -1)
    wuv = w_ukv[:, :, MLA_NOPE:].reshape(MLA_KV_RANK, -1)

    g_qn = p["g_qn"][l]
    g_kn = p["g_kn"][l]
    blk = np.arange(GQA_W) // GQA_HD
    ones = jnp.asarray((blk[:, None] == blk[None, :]).astype(np.float32), BF16)

    w_rt = jnp.zeros((d, LANES), F32)
    w_rt = w_rt.at[:, :N_GROUPS].set(p["w_group"][l]).at[:, N_GROUPS:N_GROUPS + N_EXPERTS].set(p["w_router"][l])
    w_rt_hi = w_rt.astype(BF16)
    w_rt_lo = (w_rt - w_rt_hi.astype(F32)).astype(BF16)
    b_rt = jnp.zeros((1, LANES), F32)
    b_rt = b_rt.at[0, :N_GROUPS].set(p["b_group"][l]).at[0, N_GROUPS:N_GROUPS + N_EXPERTS].set(p["b_router"][l])

    row = lambda v: v.reshape(1, -1).astype(F32)
    return dict(
        w_cat=w_cat, wuq=wuq.astype(BF16), wuqs=wuqs.astype(BF16), wuk=wuk.astype(BF16), wuv=wuv.astype(BF16),
        g_cq=row(p["g_cq"][l]), g_ckv=row(p["g_ckv"][l]),
        g_q=row(jnp.tile(g_qn, GQA_HEADS)), g_qs=row(jnp.tile(g_qn[p64], GQA_HEADS)),
        g_k=row(jnp.tile(g_kn, GQA_KV_HEADS)), g_ks=row(jnp.tile(g_kn[p64], GQA_KV_HEADS)),
        ones=ones,
        w_glu=p["w_glu"][l].astype(BF16), b_glu=row(p["b_glu"][l]),
        w_br_mla=p["w_br_mla"][l].astype(BF16), w_br_ssm=p["w_br_ssm"][l].astype(BF16),
        w_br_gqa=p["w_br_gqa"][l].astype(BF16), w_out=p["w_out"][l].astype(BF16),
        g_post_mix=row(p["g_post_mix"][l]), g_pre_ffn=row(p["g_pre_ffn"][l]),
        w_rt_hi=w_rt_hi, w_rt_lo=w_rt_lo, b_rt=b_rt,
    )


def kernel(x, c, ctx, c_ctx, w_mod, b_mod, g_pre_mix, g_post_mix, g_pre_ffn, g_post_ffn, w_in, g_cq, g_ckv, w_uq, w_ukv, g_qn, g_kn, ssm_a_re, ssm_a_im, ssm_log_dt, ssm_b_re, ssm_b_im, ssm_c_re, ssm_c_im, ssm_d, w_glu, b_glu, w_br_mla, w_br_ssm, w_br_gqa, w_out, w_group, b_group, w_router, b_router, w_exp_gate, w_exp_up, w_exp_down):
    assert x.shape[0] == 1 and ctx.shape[0] == 1
    seq, d = x.shape[1], x.shape[2]
    n_ctx = ctx.shape[1]
    assert n_ctx == ROW_TILE and seq % (SSM_CHUNK * SSM_SUPER) == 0
    depth = w_in.shape[0]
    params = dict(w_in=w_in, g_cq=g_cq, g_ckv=g_ckv, w_uq=w_uq, w_ukv=w_ukv, g_qn=g_qn, g_kn=g_kn,
                  w_glu=w_glu, b_glu=b_glu, w_br_mla=w_br_mla, w_br_ssm=w_br_ssm, w_br_gqa=w_br_gqa,
                  w_out=w_out, g_post_mix=g_post_mix, g_pre_ffn=g_pre_ffn,
                  w_group=w_group, b_group=b_group, w_router=w_router, b_router=b_router)

    cond8 = jnp.zeros((8, d), F32).at[0].set(c[0]).at[1].set(c_ctx)
    mods = _modulation(cond8, w_mod, b_mod)
    tabs = _rope_tables(seq, n_ctx)
    rows = jnp.concatenate([ctx[0], x[0]], axis=0)
    r = rows.shape[0]

    for l in range(depth):
        wts = _layer_weights(l, params)
        mod_l = mods[l]
        qm, km, vm, qg, kg, vg, u, gates = _premix(rows, mod_l, g_pre_mix[l].reshape(1, d), wts, tabs)

        o_mla = _attention(qm, km, vm, hb=2, nsub=8, ahead=6)
        o_gqa = _attention(qg, kg, vg, hb=GQA_HEADS // GQA_KV_HEADS, nsub=4, ahead=4)

        mats = _s5_matrices(ssm_a_re[l], ssm_a_im[l], ssm_log_dt[l], ssm_b_re[l], ssm_b_im[l],
                            ssm_c_re[l], ssm_c_im[l], ssm_d[l])
        ut = jnp.transpose(u.reshape(r // SSM_CHUNK, SSM_CHUNK, SSM_W), (1, 0, 2))
        yt = _s5_scan(ut, mats, n_ctx // SSM_CHUNK)
        y_ssm = jnp.transpose(yt, (1, 0, 2)).reshape(r, SSM_W)

        rows, h, route = _merge(rows, mod_l, o_mla, o_gqa, y_ssm, gates, wts)
        pos, src, tile_ea, tile_eb, tile_used = _dispatch(route[:, 0].astype(jnp.int32), r)
        z_sorted = _moe(jnp.take(h, src, axis=0), jnp.take(route, src, axis=0), tile_ea, tile_eb, tile_used,
                        w_exp_gate[l].astype(BF16), w_exp_up[l].astype(BF16), w_exp_down[l].astype(BF16),
                        g_post_ffn[l].reshape(1, d))
        rows = _residual(rows, jnp.take(z_sorted, pos, axis=0), mod_l)

    return rows[n_ctx:].reshape(1, seq, d)
```

```python
import functools
import math

import jax
import jax.numpy as jnp
import numpy as np
from jax import lax
from jax.experimental import pallas as pl
from jax.experimental.pallas import tpu as pltpu

F32 = jnp.float32
BF16 = jnp.bfloat16

GRID_W = 64
ROPE_THETA = 10000.0
NORM_EPS = 1e-6
N_MOD = 6

MLA_HEADS = 8
MLA_NOPE = 64
MLA_ROPE = 32
MLA_V = 64
MLA_Q_RANK = 256
MLA_KV_RANK = 128
MLA_SCALE = (MLA_NOPE + MLA_ROPE) ** -0.5
MLA_HEAD_PAD = 128

SSM_GROUP = 16
SSM_GROUPS = 32
SSM_STATE = 64
SSM_W = SSM_GROUP * SSM_GROUPS
SSM_DIRS = 2
SSM_CHUNK = 16
SSM_SUPER = 16
SSM_CW = SSM_CHUNK * SSM_GROUP

GQA_HEADS = 8
GQA_KV_HEADS = 2
GQA_HD = 64
GQA_W = GQA_HEADS * GQA_HD
GQA_SCALE = GQA_HD ** -0.5
LOG2E = math.log2(math.e)

N_GROUPS = 4
EXPERTS_PER_GROUP = 4
N_EXPERTS = N_GROUPS * EXPERTS_PER_GROUP
D_EXPERT = 512
PAIRS_PER_GROUP = EXPERTS_PER_GROUP * (EXPERTS_PER_GROUP - 1) // 2
GATHER_TILE = 256
MOE_TILE = GATHER_TILE

ROW_TILE = 256
LANES = 128
V_ROWS = 80
VMEM_LIMIT = 56 * 1024 * 1024


def _cparams(*sem):
    return pltpu.CompilerParams(dimension_semantics=sem, vmem_limit_bytes=VMEM_LIMIT)


def _dot(a, b):
    return jnp.dot(a, b, preferred_element_type=F32)


def _dot_f32(a, b):
    a_hi = a.astype(BF16)
    a_lo = (a - a_hi.astype(F32)).astype(BF16)
    return _dot(a_hi, b) + _dot(a_lo, b)


def _rms(x, g):
    return x * lax.rsqrt(jnp.mean(x * x, axis=-1, keepdims=True) + NORM_EPS) * g


def _sigmoid(x):
    return 1.0 / (1.0 + jnp.exp(-x))


def _mod_kernel(cond_ref, w_ref, b_ref, o_ref):
    a = cond_ref[...]
    s = a * _sigmoid(a)
    w = w_ref[...]
    s_hi = s.astype(BF16)
    s_lo = (s - s_hi.astype(F32)).astype(BF16)
    w_hi = w.astype(BF16)
    w_lo = (w - w_hi.astype(F32)).astype(BF16)
    acc = _dot(s_hi, w_hi) + _dot(s_lo, w_hi) + _dot(s_hi, w_lo)
    o_ref[...] = acc + b_ref[...]


def _modulation(cond8, w_mod, b_mod):
    nl, d, n = w_mod.shape
    tn = 1536
    return pl.pallas_call(
        _mod_kernel,
        grid=(nl, n // tn),
        in_specs=[
            pl.BlockSpec((8, d), lambda l, j: (0, 0)),
            pl.BlockSpec((None, d, tn), lambda l, j: (l, 0, j)),
            pl.BlockSpec((None, 1, tn), lambda l, j: (l, 0, j)),
        ],
        out_specs=pl.BlockSpec((None, 8, tn), lambda l, j: (l, 0, j)),
        out_shape=jax.ShapeDtypeStruct((nl, 8, n), F32),
        compiler_params=_cparams("parallel", "parallel"),
        name="modulation",
    )(cond8, w_mod, b_mod.reshape(nl, 1, n))


def _mod_row(mod_ref, idx):
    m = mod_ref[...]
    d = m.shape[1] // N_MOD
    lat = m[0:1, idx * d:(idx + 1) * d]
    ctx = m[1:2, idx * d:(idx + 1) * d]
    return jnp.where(pl.program_id(0) == 0, ctx, lat)


_SEG = {}
_off = 0
for _name, _w in (("cq", 256), ("ckv", 128), ("kr", 128), ("krs", 128), ("ssm", 512), ("gq", 512),
                  ("gqs", 512), ("gk", 128), ("gks", 128), ("gv", 128), ("gate", 3072)):
    _SEG[_name] = (_off, _off + _w)
    _off += _w
W_CAT = _off


def _premix_kernel(x_ref, mod_ref, gpre_ref, w_ref, wuq_ref, wuqs_ref, wuk_ref, wuv_ref,
                   gcq_ref, gckv_ref, gq_ref, gqs_ref, gk_ref, gks_ref, ones_ref,
                   cm_ref, sm_ref, cg_ref, sg_ref,
                   qm_ref, km_ref, vm_ref, qg_ref, kg_ref, vg_ref, u_ref, gate_ref):
    x = x_ref[...]
    h = _rms(x, gpre_ref[...]) * (1.0 + _mod_row(mod_ref, 1)) + _mod_row(mod_ref, 0)
    hb = h.astype(BF16)

    def proj(name):
        a, b = _SEG[name]
        return _dot(hb, w_ref[:, a:b])

    cm = cm_ref[...]
    sm = sm_ref[...]
    cg = cg_ref[...]
    sg = sg_ref[...]

    cqn = _rms(proj("cq"), gcq_ref[...]).astype(BF16)
    ckvn = _rms(proj("ckv"), gckv_ref[...]).astype(BF16)
    q = _dot(cqn, wuq_ref[...])
    qs = _dot(cqn, wuqs_ref[...])
    kn = _dot(ckvn, wuk_ref[...])
    kr = proj("kr") * cm + proj("krs") * sm
    tr = x.shape[0]
    vrow = lax.broadcasted_iota(jnp.int32, (V_ROWS - MLA_V, tr), 0)
    v_tail = jnp.where(vrow == 0, 1.0, 0.0).astype(BF16)
    for hd in range(MLA_HEADS):
        sl = slice(hd * MLA_HEAD_PAD, (hd + 1) * MLA_HEAD_PAD)
        qh = (q[:, sl] * cm + qs[:, sl] * sm) * (MLA_SCALE * LOG2E)
        qm_ref[hd] = qh.T.astype(BF16)
        km_ref[hd] = (kn[:, sl] + kr).astype(BF16)
    vmt = _dot(ckvn, wuv_ref[...]).T
    for hd in range(MLA_HEADS):
        vm_ref[hd, 0:MLA_V, :] = vmt[hd * MLA_V:(hd + 1) * MLA_V, :].astype(BF16)
        vm_ref[hd, MLA_V:V_ROWS, :] = v_tail

    ones = ones_ref[...]

    def head_rms_scale(v, width):
        ms = _dot_f32(v * v, ones[:width, :width]) * (1.0 / GQA_HD)
        return lax.rsqrt(ms + NORM_EPS)

    gq = proj("gq")
    gqs = proj("gqs")
    rq = head_rms_scale(gq, GQA_W)
    cg4 = jnp.concatenate([cg] * (GQA_W // LANES), axis=1)
    sg4 = jnp.concatenate([sg] * (GQA_W // LANES), axis=1)
    qg = rq * (gq * gq_ref[...] * cg4 + gqs * gqs_ref[...] * sg4)
    qgt = (qg * (GQA_SCALE * LOG2E)).T.astype(BF16)
    zeros_q = jnp.zeros((GQA_HD, tr), BF16)
    for hd in range(GQA_HEADS):
        kvh = hd // (GQA_HEADS // GQA_KV_HEADS)
        qh = qgt[hd * GQA_HD:(hd + 1) * GQA_HD, :]
        qg_ref[hd] = jnp.concatenate([qh, zeros_q] if kvh == 0 else [zeros_q, qh], axis=0)
    gk = proj("gk")
    gks = proj("gks")
    rk = head_rms_scale(gk, GQA_KV_HEADS * GQA_HD)
    kg_ref[0] = (rk * (gk * gk_ref[...] * cg + gks * gks_ref[...] * sg)).astype(BF16)
    vgt = proj("gv").T
    for kvh in range(GQA_KV_HEADS):
        vg_ref[kvh, 0:GQA_HD, :] = vgt[kvh * GQA_HD:(kvh + 1) * GQA_HD, :].astype(BF16)
        vg_ref[kvh, GQA_HD:V_ROWS, :] = v_tail

    u_ref[...] = proj("ssm").astype(BF16)
    gate_ref[...] = _sigmoid(proj("gate")).astype(BF16)


def _premix(rows, mod_l, g_pre, wts, tabs):
    r, d = rows.shape
    nt = r // ROW_TILE
    row_spec = lambda w: pl.BlockSpec((ROW_TILE, w), lambda i: (i, 0))
    full = lambda a: pl.BlockSpec(a.shape, lambda i: (0,) * a.ndim)
    consts = [g_pre, wts["w_cat"], wts["wuq"], wts["wuqs"], wts["wuk"], wts["wuv"],
              wts["g_cq"], wts["g_ckv"], wts["g_q"], wts["g_qs"], wts["g_k"], wts["g_ks"], wts["ones"]]
    t_spec = lambda nh, f: pl.BlockSpec((nh, f, ROW_TILE), lambda i: (0, 0, i))
    k_spec = lambda nh, f: pl.BlockSpec((nh, ROW_TILE, f), lambda i: (0, i, 0))
    outs = [(t_spec(MLA_HEADS, MLA_HEAD_PAD), (MLA_HEADS, MLA_HEAD_PAD, r)),
            (k_spec(MLA_HEADS, MLA_HEAD_PAD), (MLA_HEADS, r, MLA_HEAD_PAD)),
            (t_spec(MLA_HEADS, V_ROWS), (MLA_HEADS, V_ROWS, r)),
            (t_spec(GQA_HEADS, GQA_KV_HEADS * GQA_HD), (GQA_HEADS, GQA_KV_HEADS * GQA_HD, r)),
            (k_spec(1, GQA_KV_HEADS * GQA_HD), (1, r, GQA_KV_HEADS * GQA_HD)),
            (t_spec(GQA_KV_HEADS, V_ROWS), (GQA_KV_HEADS, V_ROWS, r)),
            (row_spec(SSM_W), (r, SSM_W)),
            (row_spec(3 * d), (r, 3 * d))]
    return pl.pallas_call(
        _premix_kernel,
        grid=(nt,),
        in_specs=[row_spec(d), full(mod_l)] + [full(a) for a in consts] + [row_spec(LANES)] * 4,
        out_specs=[s for s, _ in outs],
        out_shape=[jax.ShapeDtypeStruct(shp, BF16) for _, shp in outs],
        compiler_params=_cparams("parallel"),
        name="premix",
    )(rows, mod_l, *consts, tabs["cm"], tabs["sm"], tabs["cg"], tabs["sg"])


def _attn_kernel(q_ref, k_ref, v_ref, o_ref, s_ref, *, tk, nsub, n_ctx, n_iter, dv, hb, k_shared, v_shared,
                 ahead):
    n_items = hb * nsub
    tq = q_ref.shape[2]

    def scores(h, off, size):
        return _dot(k_ref[0 if k_shared else h, pl.ds(off, size), :], q_ref[h])

    def update(h, m, acc, s, off, size):
        m_new = jnp.maximum(m, jnp.max(s, axis=0, keepdims=True))
        p = jnp.exp2(s - m_new).astype(BF16)
        pv = _dot(v_ref[0 if v_shared else h, :, pl.ds(off, size)], p)
        return m_new, jnp.exp2(m - m_new) * acc + pv

    def issue(item, base):
        h, sub = item % hb, item // hb
        s_ref[item] = scores(h, pl.multiple_of(base + sub * tk, LANES), tk)

    m0 = jnp.full((1, tq), -1e30, F32)
    acc0 = jnp.zeros((V_ROWS, tq), F32)
    s_ctx = [scores(h, 0, n_ctx) for h in range(hb)]
    for item in range(ahead):
        issue(item, n_ctx)
    carry = tuple(update(h, m0, acc0, s_ctx[h], 0, n_ctx) for h in range(hb))

    def body(j, carry):
        carry = list(carry)
        base = n_ctx + j * (nsub * tk)
        base_next = n_ctx + jnp.minimum(j + 1, n_iter - 1) * (nsub * tk)
        for item in range(n_items):
            nxt = item + ahead
            if nxt < n_items:
                issue(nxt, base)
            else:
                issue(nxt - n_items, base_next)
            h, sub = item % hb, item // hb
            off = pl.multiple_of(base + sub * tk, LANES)
            carry[h] = update(h, carry[h][0], carry[h][1], s_ref[item], off, tk)
        return tuple(carry)

    carry = lax.fori_loop(0, jnp.where(pl.program_id(1) == 0, 0, n_iter), body, carry)
    for h in range(hb):
        acc = carry[h][1]
        o_ref[h] = acc[:dv] / acc[dv:dv + 1]


def _attention(qt, k, vt, *, hb, nsub, tq=256, tk=512, ahead=2):
    nh, dq, r = qt.shape
    k_shared = k.shape[0] != nh
    v_shared = vt.shape[0] != nh
    assert nh % hb == 0 and k.shape[0] in (nh, 1) and vt.shape[0] in (nh, nh // hb)
    dv = V_ROWS - 16
    assert (r - ROW_TILE) % (tk * nsub) == 0 and ahead < hb * nsub
    kern = functools.partial(_attn_kernel, tk=tk, nsub=nsub, n_ctx=ROW_TILE, n_iter=(r - ROW_TILE) // (tk * nsub),
                             dv=dv, hb=hb, k_shared=k_shared, v_shared=v_shared, ahead=ahead)
    k_spec = (pl.BlockSpec((1, r, dq), lambda g, i: (0, 0, 0)) if k_shared
              else pl.BlockSpec((hb, r, dq), lambda g, i: (g, 0, 0)))
    v_spec = pl.BlockSpec((1 if v_shared else hb, V_ROWS, r), lambda g, i: (g, 0, 0))
    return pl.pallas_call(
        kern,
        grid=(nh // hb, r // tq),
        in_specs=[pl.BlockSpec((hb, dq, tq), lambda g, i: (g, 0, i)), k_spec, v_spec],
        out_specs=pl.BlockSpec((hb, dv, tq), lambda g, i: (g, 0, i)),
        out_shape=jax.ShapeDtypeStruct((nh, dv, r), F32),
        scratch_shapes=[pltpu.VMEM((hb * nsub, tk, tq), F32)],
        compiler_params=_cparams("parallel", "arbitrary"),
        name="attention",
    )(qt, k, vt)


def _s5_kernel(u_ref, blk_ref, bp_ref, cp_ref, ar_ref, ai_ref, y_ref, xr_ref, xi_ref, *, n_ctx_chunks):
    t, j, _ = u_ref.shape
    npair = t // 2
    ns = xr_ref.shape[1]
    row = lax.broadcasted_iota(jnp.int32, (j, ns), 0)

    def run(reverse):
        ucat = jnp.concatenate([u_ref[k] for k in range(t)], axis=1)
        xc = _dot(ucat, bp_ref[...])
        if reverse:
            shift = j - n_ctx_chunks - 1
            keep = row < j - 1
        else:
            shift = 1
            keep = row >= 1
        xr_ref[...] = jnp.where(keep, pltpu.roll(xc[:, :ns], shift, 0), 0.0)
        xi_ref[...] = jnp.where(keep, pltpu.roll(xc[:, ns:], shift, 0), 0.0)
        step = 1
        k = 0
        while step < j:
            ar = ar_ref[k:k + 1, :]
            ai = ai_ref[k:k + 1, :]
            xr = xr_ref[...]
            xi = xi_ref[...]
            if reverse:
                keep = row < j - step
                sr = jnp.where(keep, pltpu.roll(xr, j - step, 0), 0.0)
                si = jnp.where(keep, pltpu.roll(xi, j - step, 0), 0.0)
            else:
                keep = row >= step
                sr = jnp.where(keep, pltpu.roll(xr, step, 0), 0.0)
                si = jnp.where(keep, pltpu.roll(xi, step, 0), 0.0)
            xr_ref[...] = xr + ar * sr - ai * si
            xi_ref[...] = xi + ar * si + ai * sr
            step *= 2
            k += 1
        xin = jnp.concatenate([xr_ref[...], xi_ref[...]], axis=1)
        if reverse:
            xin = pltpu.roll(xin, n_ctx_chunks, 0)
        xin = xin.astype(BF16)
        for po in range(npair):
            y = _dot(xin, cp_ref[:, po * 256:(po + 1) * 256])
            pins = range(po, npair) if reverse else range(0, po + 1)
            for pi_ in pins:
                upair = jnp.concatenate([u_ref[2 * pi_], u_ref[2 * pi_ + 1]], axis=1)
                y += _dot(upair, blk_ref[abs(po - pi_)])
            if reverse:
                y_ref[2 * po] += y[:, :128]
                y_ref[2 * po + 1] += y[:, 128:]
            else:
                y_ref[2 * po] = y[:, :128]
                y_ref[2 * po + 1] = y[:, 128:]

    @pl.when(pl.program_id(1) == 0)
    def _():
        run(False)

    @pl.when(pl.program_id(1) == 1)
    def _():
        run(True)


def _s5_scan(ut, mats, n_ctx_chunks):
    t, j, w = ut.shape
    no = w // LANES
    ns = (LANES // SSM_GROUP) * SSM_STATE
    wspec = lambda *tail: pl.BlockSpec((None, None) + tail, lambda o, d: (d, o) + (0,) * len(tail))
    kern = functools.partial(_s5_kernel, n_ctx_chunks=n_ctx_chunks)
    return pl.pallas_call(
        kern,
        grid=(no, SSM_DIRS),
        in_specs=[pl.BlockSpec((t, j, LANES), lambda o, d: (0, 0, o)),
                  wspec(t // 2, 256, 256), wspec(t * LANES, 2 * ns), wspec(2 * ns, t * LANES),
                  wspec(16, ns), wspec(16, ns)],
        out_specs=pl.BlockSpec((t, j, LANES), lambda o, d: (0, 0, o)),
        out_shape=jax.ShapeDtypeStruct((t, j, w), F32),
        scratch_shapes=[pltpu.VMEM((j, ns), F32)] * 2,
        compiler_params=_cparams("parallel", "arbitrary"),
        name="s5_scan",
    )(ut, mats["blk"], mats["bp"], mats["cp"], mats["ar"], mats["ai"])


def _s5_matrices(a_re, a_im, log_dt, b_re, b_im, c_re, c_im, d_skip):
    t = SSM_CHUNK
    a_re = a_re.astype(F32)
    a_im = a_im.astype(F32)
    dt = jnp.exp(log_dt.astype(F32))[..., None]
    den = a_re * a_re + a_im * a_im

    def lpow(k):
        kf = jnp.asarray(k, F32)
        mag = jnp.exp(a_re * dt * kf)
        return mag * jnp.cos(a_im * dt * kf), mag * jnp.sin(a_im * dt * kf)

    lr, li = lpow(1.0)
    cr = ((lr - 1.0) * a_re + li * a_im) / den
    ci = (li * a_re - (lr - 1.0) * a_im) / den
    bbr = cr[..., None] * b_re - ci[..., None] * b_im
    bbi = cr[..., None] * b_im + ci[..., None] * b_re
    c_re = c_re.astype(F32)
    c_im = c_im.astype(F32)

    ks = jnp.arange(t + 1, dtype=F32)[:, None, None, None]
    pr, pi = lpow(ks)
    lbr = pr[:t, ..., None] * bbr - pi[:t, ..., None] * bbi
    lbi = pr[:t, ..., None] * bbi + pi[:t, ..., None] * bbr
    kern = (jnp.einsum('dgop,kdgpc->dgkoc', c_re, lbr, precision='highest')
            - jnp.einsum('dgop,kdgpc->dgkoc', c_im, lbi, precision='highest'))
    skip = d_skip.astype(F32).reshape(SSM_GROUPS, SSM_GROUP)
    eye = jnp.eye(SSM_GROUP, dtype=F32)
    kern = kern.at[0, :, 0].add(skip[:, :, None] * eye)
    gl = LANES // SSM_GROUP
    no = SSM_GROUPS // gl
    eye_g = jnp.eye(gl, dtype=F32)
    dd = jnp.arange(t // 2)[:, None, None]
    ti = jnp.arange(2)[None, :, None]
    to = jnp.arange(2)[None, None, :]
    lag = jnp.stack([2 * dd + to - ti, 2 * dd + ti - to])
    pick = jax.vmap(lambda kd, ld: jnp.take(kd, jnp.clip(ld, 0, t - 1).reshape(-1), axis=1))(kern, lag)
    pick = pick.reshape(SSM_DIRS, no, gl, t // 2, 2, 2, SSM_GROUP, SSM_GROUP)
    pick = jnp.where((lag >= 0)[:, None, None, :, :, :, None, None], pick, 0.0)
    blk = jnp.einsum('dogDabxy,gh->doDagybhx', pick, eye_g).reshape(SSM_DIRS, no, t // 2, 256, 256)
    def contrib(lb):
        both = jnp.stack([lb[::-1, 0], lb[:, 1]])
        both = both.reshape(SSM_DIRS, t, no, gl, SSM_STATE, SSM_GROUP)
        return jnp.einsum('dtogpc,gh->dotgchp', both, eye_g).reshape(SSM_DIRS, no, t * LANES, gl * SSM_STATE)
    bp = jnp.concatenate([contrib(lbr), contrib(lbi)], axis=-1)
    p1r = pr[1:, :, :, None, :]
    p1i = pi[1:, :, :, None, :]
    rd_r = c_re * p1r - c_im * p1i
    rd_i = -(c_re * p1i + c_im * p1r)
    def readout(rd):
        both = jnp.stack([rd[:, 0], rd[::-1, 1]])
        both = both.reshape(SSM_DIRS, t, no, gl, SSM_GROUP, SSM_STATE)
        return jnp.einsum('dtogcp,gh->dogpthc', both, eye_g).reshape(SSM_DIRS, no, gl * SSM_STATE, t * LANES)
    cp = jnp.concatenate([readout(rd_r), readout(rd_i)], axis=2)
    kk = (t * 2.0 ** jnp.minimum(jnp.arange(16), 10)).astype(F32)[:, None, None, None]
    qr, qi = lpow(kk)
    lanes = lambda q: jnp.transpose(q, (1, 2, 0, 3)).reshape(SSM_DIRS, no, gl, 16, SSM_STATE)
    ar = jnp.transpose(lanes(qr), (0, 1, 3, 2, 4)).reshape(SSM_DIRS, no, 16, gl * SSM_STATE)
    ai = jnp.transpose(lanes(qi), (0, 1, 3, 2, 4)).reshape(SSM_DIRS, no, 16, gl * SSM_STATE)
    return dict(blk=blk.astype(BF16), bp=bp.astype(BF16), cp=cp.astype(BF16), ar=ar, ai=ai)


def _merge_kernel(x_ref, mod_ref, om_ref, og_ref, y_ref, gate_ref,
                  wglu_ref, bglu_ref, wbm_ref, wbs_ref, wbg_ref, wout_ref,
                  gpost_ref, gffn_ref, wrt_hi_ref, wrt_lo_ref, brt_ref,
                  xo_ref, h_ref, bk_ref):
    d = x_ref.shape[1]
    y = y_ref[...]
    y1 = 0.5 * y * (1.0 + jnp.tanh(math.sqrt(2.0 / math.pi) * (y + 0.044715 * (y * y * y))))
    ssm = y1 * _sigmoid(_dot(y1.astype(BF16), wglu_ref[...]) + bglu_ref[...])
    gate = gate_ref[...]
    o_mla = om_ref[...].reshape(-1, om_ref.shape[2]).T.astype(BF16)
    o_gqa = og_ref[...].reshape(-1, og_ref.shape[2]).T.astype(BF16)
    m = (gate[:, 0:d].astype(F32) * _dot(o_mla, wbm_ref[...])
         + gate[:, d:2 * d].astype(F32) * _dot(ssm.astype(BF16), wbs_ref[...])
         + gate[:, 2 * d:3 * d].astype(F32) * _dot(o_gqa, wbg_ref[...]))
    ymix = _dot(m.astype(BF16), wout_ref[...])
    x = x_ref[...] + _mod_row(mod_ref, 2) * _rms(ymix, gpost_ref[...])
    xo_ref[...] = x
    h = _rms(x, gffn_ref[...]) * (1.0 + _mod_row(mod_ref, 4)) + _mod_row(mod_ref, 3)
    h_ref[:, 0:d] = h

    h_hi = h.astype(BF16)
    h_lo = (h - h_hi.astype(F32)).astype(BF16)
    lg = (_dot(h_hi, wrt_hi_ref[...]) + _dot(h_lo, wrt_hi_ref[...]) + _dot(h_hi, wrt_lo_ref[...])
          + brt_ref[...])
    lane = lax.broadcasted_iota(jnp.int32, lg.shape, 1)
    neg = jnp.float32(-1e30)
    is_g = lane < N_GROUPS
    gl = jnp.where(is_g, lg, neg)
    gmax = jnp.max(gl, axis=-1, keepdims=True)
    gsel = jnp.min(jnp.where(is_g & (gl == gmax), lane, LANES), axis=-1, keepdims=True)
    pg = 1.0 / jnp.sum(jnp.where(is_g, jnp.exp(gl - gmax), 0.0), axis=-1, keepdims=True)
    lo = N_GROUPS + gsel * EXPERTS_PER_GROUP
    in_grp = (lane >= lo) & (lane < lo + EXPERTS_PER_GROUP)
    el = jnp.where(in_grp, lg, neg)
    m1 = jnp.max(el, axis=-1, keepdims=True)
    i1 = jnp.min(jnp.where(in_grp & (el == m1), lane, LANES), axis=-1, keepdims=True)
    el2 = jnp.where(lane == i1, neg, el)
    m2 = jnp.max(el2, axis=-1, keepdims=True)
    i2 = jnp.min(jnp.where(in_grp & (el2 == m2), lane, LANES), axis=-1, keepdims=True)
    e2 = jnp.exp(m2 - m1)
    t1 = pg / (1.0 + e2)
    first = i1 < i2
    ea = jnp.minimum(i1, i2) - lo
    eb = jnp.maximum(i1, i2) - lo
    bucket = gsel * PAIRS_PER_GROUP + jnp.right_shift(ea * (7 - ea), 1) + (eb - ea - 1)
    w_a = jnp.where(first, t1, t1 * e2)
    w_b = jnp.where(first, t1 * e2, t1)
    route = (jnp.where(lane == 0, bucket.astype(F32), 0.0) + jnp.where(lane == 1, w_a, 0.0)
             + jnp.where(lane == 2, w_b, 0.0))
    h_ref[:, d:d + LANES] = route
    bk_ref[...] = route


def _merge(rows, mod_l, o_mla, o_gqa, y_ssm, gates, wts):
    r, d = rows.shape
    nt = r // ROW_TILE
    row_spec = lambda w: pl.BlockSpec((ROW_TILE, w), lambda i: (i, 0))
    full = lambda a: pl.BlockSpec(a.shape, lambda i: (0,) * a.ndim)
    consts = [wts["w_glu"], wts["b_glu"], wts["w_br_mla"], wts["w_br_ssm"], wts["w_br_gqa"], wts["w_out"],
              wts["g_post_mix"], wts["g_pre_ffn"], wts["w_rt_hi"], wts["w_rt_lo"], wts["b_rt"]]
    return pl.pallas_call(
        _merge_kernel,
        grid=(nt,),
        in_specs=[row_spec(d), full(mod_l),
                  pl.BlockSpec(o_mla.shape[:2] + (ROW_TILE,), lambda i: (0, 0, i)),
                  pl.BlockSpec(o_gqa.shape[:2] + (ROW_TILE,), lambda i: (0, 0, i)),
                  row_spec(SSM_W), row_spec(3 * d)] + [full(a) for a in consts],
        out_specs=[row_spec(d), row_spec(d + LANES), row_spec(LANES)],
        out_shape=[jax.ShapeDtypeStruct((r, d), F32), jax.ShapeDtypeStruct((r, d + LANES), F32),
                   jax.ShapeDtypeStruct((r, LANES), F32)],
        compiler_params=_cparams("parallel"),
        name="merge",
    )(rows, mod_l, o_mla, o_gqa, y_ssm, gates, *consts)


def _row_copy(idx, r, src_hbm, buf, sem, slot):
    return pltpu.make_async_copy(src_hbm.at[pl.ds(idx, 1)], buf.at[slot, pl.ds(r, 1)], sem.at[slot])


def _gathered_tile(idx_ref, src_hbm, buf, sem):
    i = pl.program_id(0)
    last = pl.num_programs(0) - 1
    n = buf.shape[1]
    slot = i % 2

    def start(step, dst_slot):
        for r in range(n):
            _row_copy(idx_ref[step * n + r], r, src_hbm, buf, sem, dst_slot).start()

    def wait(dst_slot):
        for r in range(n):
            _row_copy(0, r, src_hbm, buf, sem, dst_slot).wait()

    @pl.when(i == 0)
    def _():
        start(0, 0)

    wait(slot)
    start(jnp.minimum(i + 1, last), 1 - slot)

    @pl.when(i == last)
    def _():
        wait(1 - slot)


def _moe_kernel(src_ref, ea_ref, eb_ref, used_ref, tok_hbm, wga_ref, wua_ref, wda_ref, wgb_ref, wub_ref, wdb_ref,
                gpost_ref, o_ref, buf, sem):
    i = pl.program_id(0)
    _gathered_tile(src_ref, tok_hbm, buf, sem)

    @pl.when(used_ref[i] == 1)
    def _():
        d = o_ref.shape[1]
        tok = buf[i % 2]
        h = tok[:, 0:d].astype(BF16)
        rt = tok[:, d:d + LANES]
        lane = lax.broadcasted_iota(jnp.int32, rt.shape, 1)
        w_a = jnp.sum(jnp.where(lane == 1, rt, 0.0), axis=-1, keepdims=True)
        w_b = jnp.sum(jnp.where(lane == 2, rt, 0.0), axis=-1, keepdims=True)

        def expert(wg_ref, wu_ref, wd_ref):
            a = _dot(h, wg_ref[...])
            hid = (a * _sigmoid(a)) * _dot(h, wu_ref[...])
            return _dot(hid.astype(BF16), wd_ref[...])

        y = w_a * expert(wga_ref, wua_ref, wda_ref) + w_b * expert(wgb_ref, wub_ref, wdb_ref)
        o_ref[...] = _rms(y, gpost_ref[...]).astype(o_ref.dtype)

    @pl.when(used_ref[i] == 0)
    def _():
        o_ref[...] = jnp.zeros_like(o_ref)


def _moe(tokens, src, tile_ea, tile_eb, tile_used, wg, wu, wd, g_post):
    w = tokens.shape[1]
    d = w - LANES
    ne, _, de = wg.shape
    nt = src.shape[0] // GATHER_TILE
    w_in = lambda sel: pl.BlockSpec((None, d, de), lambda i, src, ea, eb, used: ((ea, eb)[sel][i], 0, 0))
    w_out = lambda sel: pl.BlockSpec((None, de, d), lambda i, src, ea, eb, used: ((ea, eb)[sel][i], 0, 0))
    return pl.pallas_call(
        _moe_kernel,
        grid_spec=pltpu.PrefetchScalarGridSpec(
            num_scalar_prefetch=4,
            grid=(nt,),
            in_specs=[
                pl.BlockSpec(memory_space=pl.ANY),
                w_in(0), w_in(0), w_out(0), w_in(1), w_in(1), w_out(1),
                pl.BlockSpec(g_post.shape, lambda i, src, ea, eb, used: (0, 0)),
            ],
            out_specs=pl.BlockSpec((GATHER_TILE, d), lambda i, src, ea, eb, used: (i, 0)),
            scratch_shapes=[pltpu.VMEM((2, GATHER_TILE, w), F32), pltpu.SemaphoreType.DMA((2,))],
        ),
        out_shape=jax.ShapeDtypeStruct((src.shape[0], d), F32),
        compiler_params=_cparams("arbitrary"),
        name="moe",
    )(src, tile_ea, tile_eb, tile_used, tokens, wg, wu, wd, wg, wu, wd, g_post)


def _dispatch(bucket, n_rows):
    nb = N_GROUPS * PAIRS_PER_GROUP
    n_tiles = n_rows // MOE_TILE + nb
    onehot = (bucket[:, None] == jnp.arange(nb, dtype=jnp.int32)[None, :]).astype(jnp.int32)
    counts = jnp.sum(onehot, axis=0)
    rank = jnp.sum((jnp.cumsum(onehot, axis=0) - onehot) * onehot, axis=1)
    tiles_per = (counts + MOE_TILE - 1) // MOE_TILE
    tile_end = jnp.cumsum(tiles_per)
    start = (tile_end - tiles_per) * MOE_TILE
    pos = jnp.take(start, bucket) + rank
    src = jnp.zeros((n_tiles * MOE_TILE,), jnp.int32).at[pos].set(jnp.arange(n_rows, dtype=jnp.int32))
    tile_id = jnp.arange(n_tiles, dtype=jnp.int32)
    tile_bucket = jnp.minimum(jnp.searchsorted(tile_end, tile_id, side='right'), nb - 1).astype(jnp.int32)
    tile_used = (tile_id < tile_end[-1]).astype(jnp.int32)
    last_bucket = jnp.take(tile_bucket, jnp.maximum(tile_end[-1] - 1, 0))
    tile_bucket = jnp.where(tile_used == 1, tile_bucket, last_bucket)
    pair_a = jnp.asarray([0, 0, 0, 1, 1, 2], jnp.int32)
    pair_b = jnp.asarray([1, 2, 3, 2, 3, 3], jnp.int32)
    grp = tile_bucket // PAIRS_PER_GROUP
    tile_ea = grp * EXPERTS_PER_GROUP + jnp.take(pair_a, tile_bucket % PAIRS_PER_GROUP)
    tile_eb = grp * EXPERTS_PER_GROUP + jnp.take(pair_b, tile_bucket % PAIRS_PER_GROUP)
    return pos, src, tile_ea, tile_eb, tile_used


def _residual_kernel(pos_ref, x_ref, z_hbm, mod_ref, o_ref, buf, sem):
    _gathered_tile(pos_ref, z_hbm, buf, sem)
    z = buf[pl.program_id(0) % 2]
    o_ref[...] = x_ref[...] + _mod_row(mod_ref, 5) * z


def _residual(rows, z_sorted, pos, mod_l):
    r, d = rows.shape
    row_spec = pl.BlockSpec((GATHER_TILE, d), lambda i, pos: (i, 0))
    return pl.pallas_call(
        _residual_kernel,
        grid_spec=pltpu.PrefetchScalarGridSpec(
            num_scalar_prefetch=1,
            grid=(r // GATHER_TILE,),
            in_specs=[row_spec, pl.BlockSpec(memory_space=pl.ANY),
                      pl.BlockSpec(mod_l.shape, lambda i, pos: (0, 0))],
            out_specs=row_spec,
            scratch_shapes=[pltpu.VMEM((2, GATHER_TILE, d), F32), pltpu.SemaphoreType.DMA((2,))],
        ),
        out_shape=jax.ShapeDtypeStruct((r, d), F32),
        compiler_params=_cparams("arbitrary"),
        name="residual",
    )(pos, rows, z_sorted, mod_l)


def _rope_tables(seq, n_ctx):
    n_rows = seq // GRID_W

    def pattern(width):
        half = width // 4
        freqs = (np.float32(ROPE_THETA) ** (-np.arange(half, dtype=np.float32) / np.float32(half))).astype(np.float32)
        ar = np.arange(n_rows, dtype=np.float32)[:, None] * freqs[None, :]
        ac = np.arange(GRID_W, dtype=np.float32)[:, None] * freqs[None, :]
        by_row = lambda tab: jnp.broadcast_to(jnp.asarray(tab, F32)[:, None, :], (n_rows, GRID_W, half)).reshape(seq, half)
        by_col = lambda tab: jnp.broadcast_to(jnp.asarray(tab, F32)[None, :, :], (n_rows, GRID_W, half)).reshape(seq, half)
        cr, sr, cc, sc = by_row(np.cos(ar)), by_row(np.sin(ar)), by_col(np.cos(ac)), by_col(np.sin(ac))
        c = jnp.concatenate([cr, cr, cc, cc], axis=1)
        s = jnp.concatenate([-sr, sr, -sc, sc], axis=1)
        return c, s

    c32, s32 = pattern(MLA_ROPE)
    c64, s64 = pattern(GQA_HD)
    one = jnp.ones((seq, 1), F32)
    cm = jnp.concatenate([one * jnp.ones((1, MLA_NOPE)), c32, one * jnp.ones((1, 32))], axis=1)
    sm = jnp.concatenate([jnp.zeros((seq, MLA_NOPE)), s32, jnp.zeros((seq, 32))], axis=1)
    cg = jnp.concatenate([c64, c64], axis=1)
    sg = jnp.concatenate([s64, s64], axis=1)
    ctx_c = jnp.ones((n_ctx, LANES), F32)
    ctx_s = jnp.zeros((n_ctx, LANES), F32)
    cat = lambda a, b: jnp.concatenate([a, b], axis=0)
    return dict(cm=cat(ctx_c, cm), sm=cat(ctx_s, sm), cg=cat(ctx_c, cg), sg=cat(ctx_s, sg))


def _swap_perm(width):
    q = width // 4
    return np.concatenate([np.arange(q, 2 * q), np.arange(0, q), np.arange(3 * q, 4 * q), np.arange(2 * q, 3 * q)])


def _layer_weights(l, p):
    d = p["w_in"].shape[1]
    w_in = p["w_in"][l]
    o = 0
    cq = w_in[:, o:o + MLA_Q_RANK]; o += MLA_Q_RANK
    ckv = w_in[:, o:o + MLA_KV_RANK]; o += MLA_KV_RANK
    kr = w_in[:, o:o + MLA_ROPE]; o += MLA_ROPE
    ssm = w_in[:, o:o + SSM_W]; o += SSM_W
    gq = w_in[:, o:o + GQA_W]; o += GQA_W
    gk = w_in[:, o:o + GQA_KV_HEADS * GQA_HD]; o += GQA_KV_HEADS * GQA_HD
    gv = w_in[:, o:o + GQA_KV_HEADS * GQA_HD]; o += GQA_KV_HEADS * GQA_HD
    gate = w_in[:, o:]
    p32 = _swap_perm(MLA_ROPE)
    p64 = _swap_perm(GQA_HD)
    perm_q = np.concatenate([p64 + GQA_HD * h for h in range(GQA_HEADS)])
    perm_k = np.concatenate([p64 + GQA_HD * h for h in range(GQA_KV_HEADS)])
    z = lambda n: jnp.zeros((d, n), F32)
    kr128 = jnp.concatenate([z(MLA_NOPE), kr, z(32)], axis=1)
    krs128 = jnp.concatenate([z(MLA_NOPE), kr[:, p32], z(32)], axis=1)
    w_cat = jnp.concatenate([cq, ckv, kr128, krs128, ssm, gq, gq[:, perm_q], gk, gk[:, perm_k], gv, gate],
                            axis=1).astype(BF16)

    w_uq = p["w_uq"][l].reshape(MLA_Q_RANK, MLA_HEADS, MLA_NOPE + MLA_ROPE)
    zq = jnp.zeros((MLA_Q_RANK, MLA_HEADS, 32), F32)
    wuq = jnp.concatenate([w_uq, zq], axis=2).reshape(MLA_Q_RANK, -1)
    wuqs = jnp.concatenate([jnp.zeros((MLA_Q_RANK, MLA_HEADS, MLA_NOPE), F32),
                            w_uq[:, :, MLA_NOPE:][:, :, p32], zq], axis=2).reshape(MLA_Q_RANK, -1)
    w_ukv = p["w_ukv"][l].reshape(MLA_KV_RANK, MLA_HEADS, MLA_NOPE + MLA_V)
    wuk = jnp.concatenate([w_ukv[:, :, :MLA_NOPE], jnp.zeros((MLA_KV_RANK, MLA_HEADS, 64), F32)],
                          axis=2).reshape(MLA_KV_RANK, -1)
    wuv = w_ukv[:, :, MLA_NOPE:].reshape(MLA_KV_RANK, -1)

    g_qn = p["g_qn"][l]
    g_kn = p["g_kn"][l]
    blk = np.arange(GQA_W) // GQA_HD
    ones = jnp.asarray((blk[:, None] == blk[None, :]).astype(np.float32), BF16)

    w_rt = jnp.zeros((d, LANES), F32)
    w_rt = w_rt.at[:, :N_GROUPS].set(p["w_group"][l]).at[:, N_GROUPS:N_GROUPS + N_EXPERTS].set(p["w_router"][l])
    w_rt_hi = w_rt.astype(BF16)
    w_rt_lo = (w_rt - w_rt_hi.astype(F32)).astype(BF16)
    b_rt = jnp.zeros((1, LANES), F32)
    b_rt = b_rt.at[0, :N_GROUPS].set(p["b_group"][l]).at[0, N_GROUPS:N_GROUPS + N_EXPERTS].set(p["b_router"][l])

    row = lambda v: v.reshape(1, -1).astype(F32)
    return dict(
        w_cat=w_cat, wuq=wuq.astype(BF16), wuqs=wuqs.astype(BF16), wuk=wuk.astype(BF16), wuv=wuv.astype(BF16),
        g_cq=row(p["g_cq"][l]), g_ckv=row(p["g_ckv"][l]),
        g_q=row(jnp.tile(g_qn, GQA_HEADS)), g_qs=row(jnp.tile(g_qn[p64], GQA_HEADS)),
        g_k=row(jnp.tile(g_kn, GQA_KV_HEADS)), g_ks=row(jnp.tile(g_kn[p64], GQA_KV_HEADS)),
        ones=ones,
        w_glu=p["w_glu"][l].astype(BF16), b_glu=row(p["b_glu"][l]),
        w_br_mla=p["w_br_mla"][l].astype(BF16), w_br_ssm=p["w_br_ssm"][l].astype(BF16),
        w_br_gqa=p["w_br_gqa"][l].astype(BF16), w_out=p["w_out"][l].astype(BF16),
        g_post_mix=row(p["g_post_mix"][l]), g_pre_ffn=row(p["g_pre_ffn"][l]),
        w_rt_hi=w_rt_hi, w_rt_lo=w_rt_lo, b_rt=b_rt,
    )


def kernel(x, c, ctx, c_ctx, w_mod, b_mod, g_pre_mix, g_post_mix, g_pre_ffn, g_post_ffn, w_in, g_cq, g_ckv, w_uq, w_ukv, g_qn, g_kn, ssm_a_re, ssm_a_im, ssm_log_dt, ssm_b_re, ssm_b_im, ssm_c_re, ssm_c_im, ssm_d, w_glu, b_glu, w_br_mla, w_br_ssm, w_br_gqa, w_out, w_group, b_group, w_router, b_router, w_exp_gate, w_exp_up, w_exp_down):
    assert x.shape[0] == 1 and ctx.shape[0] == 1
    seq, d = x.shape[1], x.shape[2]
    n_ctx = ctx.shape[1]
    assert n_ctx == ROW_TILE and seq % (SSM_CHUNK * SSM_SUPER) == 0
    depth = w_in.shape[0]
    params = dict(w_in=w_in, g_cq=g_cq, g_ckv=g_ckv, w_uq=w_uq, w_ukv=w_ukv, g_qn=g_qn, g_kn=g_kn,
                  w_glu=w_glu, b_glu=b_glu, w_br_mla=w_br_mla, w_br_ssm=w_br_ssm, w_br_gqa=w_br_gqa,
                  w_out=w_out, g_post_mix=g_post_mix, g_pre_ffn=g_pre_ffn,
                  w_group=w_group, b_group=b_group, w_router=w_router, b_router=b_router)

    cond8 = jnp.zeros((8, d), F32).at[0].set(c[0]).at[1].set(c_ctx)
    mods = _modulation(cond8, w_mod, b_mod)
    tabs = _rope_tables(seq, n_ctx)
    rows = jnp.concatenate([ctx[0], x[0]], axis=0)
    r = rows.shape[0]

    for l in range(depth):
        wts = _layer_weights(l, params)
        mod_l = mods[l]
        qm, km, vm, qg, kg, vg, u, gates = _premix(rows, mod_l, g_pre_mix[l].reshape(1, d), wts, tabs)

        o_mla = _attention(qm, km, vm, hb=2, nsub=8, ahead=6)
        o_gqa = _attention(qg, kg, vg, hb=GQA_HEADS // GQA_KV_HEADS, nsub=4, ahead=4)

        mats = _s5_matrices(ssm_a_re[l], ssm_a_im[l], ssm_log_dt[l], ssm_b_re[l], ssm_b_im[l],
                            ssm_c_re[l], ssm_c_im[l], ssm_d[l])
        ut = jnp.transpose(u.reshape(r // SSM_CHUNK, SSM_CHUNK, SSM_W), (1, 0, 2))
        yt = _s5_scan(ut, mats, n_ctx // SSM_CHUNK)
        y_ssm = jnp.transpose(yt, (1, 0, 2)).reshape(r, SSM_W)

        rows, tokens, route = _merge(rows, mod_l, o_mla, o_gqa, y_ssm, gates, wts)
        pos, src, tile_ea, tile_eb, tile_used = _dispatch(route[:, 0].astype(jnp.int32), r)
        z_sorted = _moe(tokens, src, tile_ea, tile_eb, tile_used,
                        w_exp_gate[l].astype(BF16), w_exp_up[l].astype(BF16), w_exp_down[l].astype(BF16),
                        g_post_ffn[l].reshape(1, d))
        rows = _residual(rows, z_sorted, pos, mod_l)

    return rows[n_ctx:].reshape(1, seq, d)
```

```python
import functools
import math

import jax
import jax.numpy as jnp
import numpy as np
from jax import lax
from jax.experimental import pallas as pl
from jax.experimental.pallas import tpu as pltpu

F32 = jnp.float32
BF16 = jnp.bfloat16

GRID_W = 64
ROPE_THETA = 10000.0
NORM_EPS = 1e-6
N_MOD = 6

MLA_HEADS = 8
MLA_NOPE = 64
MLA_ROPE = 32
MLA_V = 64
MLA_Q_RANK = 256
MLA_KV_RANK = 128
MLA_SCALE = (MLA_NOPE + MLA_ROPE) ** -0.5
MLA_HEAD_PAD = 128

SSM_GROUP = 16
SSM_GROUPS = 32
SSM_STATE = 64
SSM_W = SSM_GROUP * SSM_GROUPS
SSM_DIRS = 2
SSM_CHUNK = 16
SSM_SUPER = 16
SSM_CW = SSM_CHUNK * SSM_GROUP

GQA_HEADS = 8
GQA_KV_HEADS = 2
GQA_HD = 64
GQA_W = GQA_HEADS * GQA_HD
GQA_SCALE = GQA_HD ** -0.5
LOG2E = math.log2(math.e)

N_GROUPS = 4
EXPERTS_PER_GROUP = 4
N_EXPERTS = N_GROUPS * EXPERTS_PER_GROUP
D_EXPERT = 512
PAIRS_PER_GROUP = EXPERTS_PER_GROUP * (EXPERTS_PER_GROUP - 1) // 2
GATHER_TILE = 256
GATHER_DEPTH = 3
MOE_TILE = GATHER_TILE

ROW_TILE = 256
LANES = 128
V_ROWS = 80
VMEM_LIMIT = 56 * 1024 * 1024


def _cparams(*sem):
    return pltpu.CompilerParams(dimension_semantics=sem, vmem_limit_bytes=VMEM_LIMIT)


def _dot(a, b):
    return jnp.dot(a, b, preferred_element_type=F32)


def _dot_f32(a, b):
    a_hi = a.astype(BF16)
    a_lo = (a - a_hi.astype(F32)).astype(BF16)
    return _dot(a_hi, b) + _dot(a_lo, b)


def _rms(x, g):
    return x * lax.rsqrt(jnp.mean(x * x, axis=-1, keepdims=True) + NORM_EPS) * g


def _sigmoid(x):
    return 1.0 / (1.0 + jnp.exp(-x))


def _mod_kernel(cond_ref, w_ref, b_ref, o_ref):
    a = cond_ref[...]
    s = a * _sigmoid(a)
    w = w_ref[...]
    s_hi = s.astype(BF16)
    s_lo = (s - s_hi.astype(F32)).astype(BF16)
    w_hi = w.astype(BF16)
    w_lo = (w - w_hi.astype(F32)).astype(BF16)
    acc = _dot(s_hi, w_hi) + _dot(s_lo, w_hi) + _dot(s_hi, w_lo)
    o_ref[...] = acc + b_ref[...]


def _modulation(cond8, w_mod, b_mod):
    nl, d, n = w_mod.shape
    tn = 1536
    return pl.pallas_call(
        _mod_kernel,
        grid=(nl, n // tn),
        in_specs=[
            pl.BlockSpec((8, d), lambda l, j: (0, 0)),
            pl.BlockSpec((None, d, tn), lambda l, j: (l, 0, j)),
            pl.BlockSpec((None, 1, tn), lambda l, j: (l, 0, j)),
        ],
        out_specs=pl.BlockSpec((None, 8, tn), lambda l, j: (l, 0, j)),
        out_shape=jax.ShapeDtypeStruct((nl, 8, n), F32),
        compiler_params=_cparams("parallel", "parallel"),
        name="modulation",
    )(cond8, w_mod, b_mod.reshape(nl, 1, n))


def _mod_row(mod_ref, idx):
    m = mod_ref[...]
    d = m.shape[1] // N_MOD
    lat = m[0:1, idx * d:(idx + 1) * d]
    ctx = m[1:2, idx * d:(idx + 1) * d]
    return jnp.where(pl.program_id(0) == 0, ctx, lat)


_SEG = {}
_off = 0
for _name, _w in (("cq", 256), ("ckv", 128), ("kr", 128), ("krs", 128), ("ssm", 512), ("gq", 512),
                  ("gqs", 512), ("gk", 128), ("gks", 128), ("gv", 128), ("gate", 3072)):
    _SEG[_name] = (_off, _off + _w)
    _off += _w
W_CAT = _off


def _premix_kernel(x_ref, mod_ref, gpre_ref, w_ref, wuq_ref, wuqs_ref, wuk_ref, wuv_ref,
                   gcq_ref, gckv_ref, gq_ref, gqs_ref, gk_ref, gks_ref, ones_ref,
                   cm_ref, sm_ref, cg_ref, sg_ref,
                   qm_ref, km_ref, vm_ref, qg_ref, kg_ref, vg_ref, u_ref, gate_ref):
    x = x_ref[...]
    h = _rms(x, gpre_ref[...]) * (1.0 + _mod_row(mod_ref, 1)) + _mod_row(mod_ref, 0)
    hb = h.astype(BF16)

    def proj(name):
        a, b = _SEG[name]
        return _dot(hb, w_ref[:, a:b])

    cm = cm_ref[...]
    sm = sm_ref[...]
    cg = cg_ref[...]
    sg = sg_ref[...]

    cqn = _rms(proj("cq"), gcq_ref[...]).astype(BF16)
    ckvn = _rms(proj("ckv"), gckv_ref[...]).astype(BF16)
    q = _dot(cqn, wuq_ref[...])
    qs = _dot(cqn, wuqs_ref[...])
    kn = _dot(ckvn, wuk_ref[...])
    kr = proj("kr") * cm + proj("krs") * sm
    tr = x.shape[0]
    vrow = lax.broadcasted_iota(jnp.int32, (V_ROWS - MLA_V, tr), 0)
    v_tail = jnp.where(vrow == 0, 1.0, 0.0).astype(BF16)
    for hd in range(MLA_HEADS):
        sl = slice(hd * MLA_HEAD_PAD, (hd + 1) * MLA_HEAD_PAD)
        qh = (q[:, sl] * cm + qs[:, sl] * sm) * (MLA_SCALE * LOG2E)
        qm_ref[hd] = qh.T.astype(BF16)
        km_ref[hd] = (kn[:, sl] + kr).astype(BF16)
    vmt = _dot(ckvn, wuv_ref[...]).T
    for hd in range(MLA_HEADS):
        vm_ref[hd, 0:MLA_V, :] = vmt[hd * MLA_V:(hd + 1) * MLA_V, :].astype(BF16)
        vm_ref[hd, MLA_V:V_ROWS, :] = v_tail

    ones = ones_ref[...]

    def head_rms_scale(v, width):
        ms = _dot_f32(v * v, ones[:width, :width]) * (1.0 / GQA_HD)
        return lax.rsqrt(ms + NORM_EPS)

    gq = proj("gq")
    gqs = proj("gqs")
    rq = head_rms_scale(gq, GQA_W)
    cg4 = jnp.concatenate([cg] * (GQA_W // LANES), axis=1)
    sg4 = jnp.concatenate([sg] * (GQA_W // LANES), axis=1)
    qg = rq * (gq * gq_ref[...] * cg4 + gqs * gqs_ref[...] * sg4)
    qgt = (qg * (GQA_SCALE * LOG2E)).T.astype(BF16)
    zeros_q = jnp.zeros((GQA_HD, tr), BF16)
    for hd in range(GQA_HEADS):
        kvh = hd // (GQA_HEADS // GQA_KV_HEADS)
        qh = qgt[hd * GQA_HD:(hd + 1) * GQA_HD, :]
        qg_ref[hd] = jnp.concatenate([qh, zeros_q] if kvh == 0 else [zeros_q, qh], axis=0)
    gk = proj("gk")
    gks = proj("gks")
    rk = head_rms_scale(gk, GQA_KV_HEADS * GQA_HD)
    kg_ref[0] = (rk * (gk * gk_ref[...] * cg + gks * gks_ref[...] * sg)).astype(BF16)
    vgt = proj("gv").T
    for kvh in range(GQA_KV_HEADS):
        vg_ref[kvh, 0:GQA_HD, :] = vgt[kvh * GQA_HD:(kvh + 1) * GQA_HD, :].astype(BF16)
        vg_ref[kvh, GQA_HD:V_ROWS, :] = v_tail

    u_ref[...] = proj("ssm").astype(BF16)
    gate_ref[...] = _sigmoid(proj("gate")).astype(BF16)


def _premix(rows, mod_l, g_pre, wts, tabs):
    r, d = rows.shape
    nt = r // ROW_TILE
    row_spec = lambda w: pl.BlockSpec((ROW_TILE, w), lambda i: (i, 0))
    full = lambda a: pl.BlockSpec(a.shape, lambda i: (0,) * a.ndim)
    consts = [g_pre, wts["w_cat"], wts["wuq"], wts["wuqs"], wts["wuk"], wts["wuv"],
              wts["g_cq"], wts["g_ckv"], wts["g_q"], wts["g_qs"], wts["g_k"], wts["g_ks"], wts["ones"]]
    t_spec = lambda nh, f: pl.BlockSpec((nh, f, ROW_TILE), lambda i: (0, 0, i))
    k_spec = lambda nh, f: pl.BlockSpec((nh, ROW_TILE, f), lambda i: (0, i, 0))
    outs = [(t_spec(MLA_HEADS, MLA_HEAD_PAD), (MLA_HEADS, MLA_HEAD_PAD, r)),
            (k_spec(MLA_HEADS, MLA_HEAD_PAD), (MLA_HEADS, r, MLA_HEAD_PAD)),
            (t_spec(MLA_HEADS, V_ROWS), (MLA_HEADS, V_ROWS, r)),
            (t_spec(GQA_HEADS, GQA_KV_HEADS * GQA_HD), (GQA_HEADS, GQA_KV_HEADS * GQA_HD, r)),
            (k_spec(1, GQA_KV_HEADS * GQA_HD), (1, r, GQA_KV_HEADS * GQA_HD)),
            (t_spec(GQA_KV_HEADS, V_ROWS), (GQA_KV_HEADS, V_ROWS, r)),
            (row_spec(SSM_W), (r, SSM_W)),
            (row_spec(3 * d), (r, 3 * d))]
    return pl.pallas_call(
        _premix_kernel,
        grid=(nt,),
        in_specs=[row_spec(d), full(mod_l)] + [full(a) for a in consts] + [row_spec(LANES)] * 4,
        out_specs=[s for s, _ in outs],
        out_shape=[jax.ShapeDtypeStruct(shp, BF16) for _, shp in outs],
        compiler_params=_cparams("parallel"),
        name="premix",
    )(rows, mod_l, *consts, tabs["cm"], tabs["sm"], tabs["cg"], tabs["sg"])


def _attn_kernel(q_ref, k_ref, v_ref, o_ref, s_ref, *, tk, nsub, n_ctx, n_iter, dv, hb, k_shared, v_shared,
                 ahead):
    n_items = hb * nsub
    tq = q_ref.shape[2]

    def scores(h, off, size):
        return _dot(k_ref[0 if k_shared else h, pl.ds(off, size), :], q_ref[h])

    def update(h, m, acc, s, off, size):
        m_new = jnp.maximum(m, jnp.max(s, axis=0, keepdims=True))
        p = jnp.exp2(s - m_new).astype(BF16)
        pv = _dot(v_ref[0 if v_shared else h, :, pl.ds(off, size)], p)
        return m_new, jnp.exp2(m - m_new) * acc + pv

    def issue(item, base):
        h, sub = item % hb, item // hb
        s_ref[item] = scores(h, pl.multiple_of(base + sub * tk, LANES), tk)

    m0 = jnp.full((1, tq), -1e30, F32)
    acc0 = jnp.zeros((V_ROWS, tq), F32)
    s_ctx = [scores(h, 0, n_ctx) for h in range(hb)]
    for item in range(ahead):
        issue(item, n_ctx)
    carry = tuple(update(h, m0, acc0, s_ctx[h], 0, n_ctx) for h in range(hb))

    def body(j, carry):
        carry = list(carry)
        base = n_ctx + j * (nsub * tk)
        base_next = n_ctx + jnp.minimum(j + 1, n_iter - 1) * (nsub * tk)
        for item in range(n_items):
            nxt = item + ahead
            if nxt < n_items:
                issue(nxt, base)
            else:
                issue(nxt - n_items, base_next)
            h, sub = item % hb, item // hb
            off = pl.multiple_of(base + sub * tk, LANES)
            carry[h] = update(h, carry[h][0], carry[h][1], s_ref[item], off, tk)
        return tuple(carry)

    carry = lax.fori_loop(0, jnp.where(pl.program_id(1) == 0, 0, n_iter), body, carry)
    for h in range(hb):
        acc = carry[h][1]
        o_ref[h] = acc[:dv] / acc[dv:dv + 1]


def _attention(qt, k, vt, *, hb, nsub, tq=256, tk=512, ahead=2):
    nh, dq, r = qt.shape
    k_shared = k.shape[0] != nh
    v_shared = vt.shape[0] != nh
    assert nh % hb == 0 and k.shape[0] in (nh, 1) and vt.shape[0] in (nh, nh // hb)
    dv = V_ROWS - 16
    assert (r - ROW_TILE) % (tk * nsub) == 0 and ahead < hb * nsub
    kern = functools.partial(_attn_kernel, tk=tk, nsub=nsub, n_ctx=ROW_TILE, n_iter=(r - ROW_TILE) // (tk * nsub),
                             dv=dv, hb=hb, k_shared=k_shared, v_shared=v_shared, ahead=ahead)
    k_spec = (pl.BlockSpec((1, r, dq), lambda g, i: (0, 0, 0)) if k_shared
              else pl.BlockSpec((hb, r, dq), lambda g, i: (g, 0, 0)))
    v_spec = pl.BlockSpec((1 if v_shared else hb, V_ROWS, r), lambda g, i: (g, 0, 0))
    return pl.pallas_call(
        kern,
        grid=(nh // hb, r // tq),
        in_specs=[pl.BlockSpec((hb, dq, tq), lambda g, i: (g, 0, i)), k_spec, v_spec],
        out_specs=pl.BlockSpec((hb, dv, tq), lambda g, i: (g, 0, i)),
        out_shape=jax.ShapeDtypeStruct((nh, dv, r), F32),
        scratch_shapes=[pltpu.VMEM((hb * nsub, tk, tq), F32)],
        compiler_params=_cparams("parallel", "arbitrary"),
        name="attention",
    )(qt, k, vt)


def _s5_kernel(u_ref, blk_ref, bp_ref, cp_ref, ar_ref, ai_ref, y_ref, xr_ref, xi_ref, *, n_ctx_chunks):
    t, j, _ = u_ref.shape
    npair = t // 2
    ns = xr_ref.shape[1]
    row = lax.broadcasted_iota(jnp.int32, (j, ns), 0)

    def run(reverse):
        ucat = jnp.concatenate([u_ref[k] for k in range(t)], axis=1)
        xc = _dot(ucat, bp_ref[...])
        if reverse:
            shift = j - n_ctx_chunks - 1
            keep = row < j - 1
        else:
            shift = 1
            keep = row >= 1
        xr_ref[...] = jnp.where(keep, pltpu.roll(xc[:, :ns], shift, 0), 0.0)
        xi_ref[...] = jnp.where(keep, pltpu.roll(xc[:, ns:], shift, 0), 0.0)
        step = 1
        k = 0
        while step < j:
            ar = ar_ref[k:k + 1, :]
            ai = ai_ref[k:k + 1, :]
            xr = xr_ref[...]
            xi = xi_ref[...]
            if reverse:
                keep = row < j - step
                sr = jnp.where(keep, pltpu.roll(xr, j - step, 0), 0.0)
                si = jnp.where(keep, pltpu.roll(xi, j - step, 0), 0.0)
            else:
                keep = row >= step
                sr = jnp.where(keep, pltpu.roll(xr, step, 0), 0.0)
                si = jnp.where(keep, pltpu.roll(xi, step, 0), 0.0)
            xr_ref[...] = xr + ar * sr - ai * si
            xi_ref[...] = xi + ar * si + ai * sr
            step *= 2
            k += 1
        xin = jnp.concatenate([xr_ref[...], xi_ref[...]], axis=1)
        if reverse:
            xin = pltpu.roll(xin, n_ctx_chunks, 0)
        xin = xin.astype(BF16)
        for po in range(npair):
            y = _dot(xin, cp_ref[:, po * 256:(po + 1) * 256])
            pins = range(po, npair) if reverse else range(0, po + 1)
            for pi_ in pins:
                upair = jnp.concatenate([u_ref[2 * pi_], u_ref[2 * pi_ + 1]], axis=1)
                y += _dot(upair, blk_ref[abs(po - pi_)])
            if reverse:
                y_ref[2 * po] += y[:, :128]
                y_ref[2 * po + 1] += y[:, 128:]
            else:
                y_ref[2 * po] = y[:, :128]
                y_ref[2 * po + 1] = y[:, 128:]

    @pl.when(pl.program_id(1) == 0)
    def _():
        run(False)

    @pl.when(pl.program_id(1) == 1)
    def _():
        run(True)


def _s5_scan(ut, mats, n_ctx_chunks):
    t, j, w = ut.shape
    no = w // LANES
    ns = (LANES // SSM_GROUP) * SSM_STATE
    wspec = lambda *tail: pl.BlockSpec((None, None) + tail, lambda o, d: (d, o) + (0,) * len(tail))
    kern = functools.partial(_s5_kernel, n_ctx_chunks=n_ctx_chunks)
    return pl.pallas_call(
        kern,
        grid=(no, SSM_DIRS),
        in_specs=[pl.BlockSpec((t, j, LANES), lambda o, d: (0, 0, o)),
                  wspec(t // 2, 256, 256), wspec(t * LANES, 2 * ns), wspec(2 * ns, t * LANES),
                  wspec(16, ns), wspec(16, ns)],
        out_specs=pl.BlockSpec((t, j, LANES), lambda o, d: (0, 0, o)),
        out_shape=jax.ShapeDtypeStruct((t, j, w), F32),
        scratch_shapes=[pltpu.VMEM((j, ns), F32)] * 2,
        compiler_params=_cparams("parallel", "arbitrary"),
        name="s5_scan",
    )(ut, mats["blk"], mats["bp"], mats["cp"], mats["ar"], mats["ai"])


def _s5_matrices(a_re, a_im, log_dt, b_re, b_im, c_re, c_im, d_skip):
    t = SSM_CHUNK
    a_re = a_re.astype(F32)
    a_im = a_im.astype(F32)
    dt = jnp.exp(log_dt.astype(F32))[..., None]
    den = a_re * a_re + a_im * a_im

    def lpow(k):
        kf = jnp.asarray(k, F32)
        mag = jnp.exp(a_re * dt * kf)
        return mag * jnp.cos(a_im * dt * kf), mag * jnp.sin(a_im * dt * kf)

    lr, li = lpow(1.0)
    cr = ((lr - 1.0) * a_re + li * a_im) / den
    ci = (li * a_re - (lr - 1.0) * a_im) / den
    bbr = cr[..., None] * b_re - ci[..., None] * b_im
    bbi = cr[..., None] * b_im + ci[..., None] * b_re
    c_re = c_re.astype(F32)
    c_im = c_im.astype(F32)

    ks = jnp.arange(t + 1, dtype=F32)[:, None, None, None]
    pr, pi = lpow(ks)
    lbr = pr[:t, ..., None] * bbr - pi[:t, ..., None] * bbi
    lbi = pr[:t, ..., None] * bbi + pi[:t, ..., None] * bbr
    kern = (jnp.einsum('dgop,kdgpc->dgkoc', c_re, lbr, precision='highest')
            - jnp.einsum('dgop,kdgpc->dgkoc', c_im, lbi, precision='highest'))
    skip = d_skip.astype(F32).reshape(SSM_GROUPS, SSM_GROUP)
    eye = jnp.eye(SSM_GROUP, dtype=F32)
    kern = kern.at[0, :, 0].add(skip[:, :, None] * eye)
    gl = LANES // SSM_GROUP
    no = SSM_GROUPS // gl
    eye_g = jnp.eye(gl, dtype=F32)
    dd = jnp.arange(t // 2)[:, None, None]
    ti = jnp.arange(2)[None, :, None]
    to = jnp.arange(2)[None, None, :]
    lag = jnp.stack([2 * dd + to - ti, 2 * dd + ti - to])
    pick = jax.vmap(lambda kd, ld: jnp.take(kd, jnp.clip(ld, 0, t - 1).reshape(-1), axis=1))(kern, lag)
    pick = pick.reshape(SSM_DIRS, no, gl, t // 2, 2, 2, SSM_GROUP, SSM_GROUP)
    pick = jnp.where((lag >= 0)[:, None, None, :, :, :, None, None], pick, 0.0)
    blk = jnp.einsum('dogDabxy,gh->doDagybhx', pick, eye_g).reshape(SSM_DIRS, no, t // 2, 256, 256)
    def contrib(lb):
        both = jnp.stack([lb[::-1, 0], lb[:, 1]])
        both = both.reshape(SSM_DIRS, t, no, gl, SSM_STATE, SSM_GROUP)
        return jnp.einsum('dtogpc,gh->dotgchp', both, eye_g).reshape(SSM_DIRS, no, t * LANES, gl * SSM_STATE)
    bp = jnp.concatenate([contrib(lbr), contrib(lbi)], axis=-1)
    p1r = pr[1:, :, :, None, :]
    p1i = pi[1:, :, :, None, :]
    rd_r = c_re * p1r - c_im * p1i
    rd_i = -(c_re * p1i + c_im * p1r)
    def readout(rd):
        both = jnp.stack([rd[:, 0], rd[::-1, 1]])
        both = both.reshape(SSM_DIRS, t, no, gl, SSM_GROUP, SSM_STATE)
        return jnp.einsum('dtogcp,gh->dogpthc', both, eye_g).reshape(SSM_DIRS, no, gl * SSM_STATE, t * LANES)
    cp = jnp.concatenate([readout(rd_r), readout(rd_i)], axis=2)
    kk = (t * 2.0 ** jnp.minimum(jnp.arange(16), 10)).astype(F32)[:, None, None, None]
    qr, qi = lpow(kk)
    lanes = lambda q: jnp.transpose(q, (1, 2, 0, 3)).reshape(SSM_DIRS, no, gl, 16, SSM_STATE)
    ar = jnp.transpose(lanes(qr), (0, 1, 3, 2, 4)).reshape(SSM_DIRS, no, 16, gl * SSM_STATE)
    ai = jnp.transpose(lanes(qi), (0, 1, 3, 2, 4)).reshape(SSM_DIRS, no, 16, gl * SSM_STATE)
    return dict(blk=blk.astype(BF16), bp=bp.astype(BF16), cp=cp.astype(BF16), ar=ar, ai=ai)


def _merge_kernel(x_ref, mod_ref, om_ref, og_ref, y_ref, gate_ref,
                  wglu_ref, bglu_ref, wbm_ref, wbs_ref, wbg_ref, wout_ref,
                  gpost_ref, gffn_ref, wrt_hi_ref, wrt_lo_ref, brt_ref,
                  xo_ref, h_ref, bk_ref):
    d = x_ref.shape[1]
    y = y_ref[...]
    y1 = 0.5 * y * (1.0 + jnp.tanh(math.sqrt(2.0 / math.pi) * (y + 0.044715 * (y * y * y))))
    ssm = y1 * _sigmoid(_dot(y1.astype(BF16), wglu_ref[...]) + bglu_ref[...])
    gate = gate_ref[...]
    o_mla = om_ref[...].reshape(-1, om_ref.shape[2]).T.astype(BF16)
    o_gqa = og_ref[...].reshape(-1, og_ref.shape[2]).T.astype(BF16)
    m = (gate[:, 0:d].astype(F32) * _dot(o_mla, wbm_ref[...])
         + gate[:, d:2 * d].astype(F32) * _dot(ssm.astype(BF16), wbs_ref[...])
         + gate[:, 2 * d:3 * d].astype(F32) * _dot(o_gqa, wbg_ref[...]))
    ymix = _dot(m.astype(BF16), wout_ref[...])
    x = x_ref[...] + _mod_row(mod_ref, 2) * _rms(ymix, gpost_ref[...])
    xo_ref[...] = x
    h = _rms(x, gffn_ref[...]) * (1.0 + _mod_row(mod_ref, 4)) + _mod_row(mod_ref, 3)
    h_ref[:, 0:d] = h

    h_hi = h.astype(BF16)
    h_lo = (h - h_hi.astype(F32)).astype(BF16)
    lg = (_dot(h_hi, wrt_hi_ref[...]) + _dot(h_lo, wrt_hi_ref[...]) + _dot(h_hi, wrt_lo_ref[...])
          + brt_ref[...])
    lane = lax.broadcasted_iota(jnp.int32, lg.shape, 1)
    neg = jnp.float32(-1e30)
    is_g = lane < N_GROUPS
    gl = jnp.where(is_g, lg, neg)
    gmax = jnp.max(gl, axis=-1, keepdims=True)
    gsel = jnp.min(jnp.where(is_g & (gl == gmax), lane, LANES), axis=-1, keepdims=True)
    pg = 1.0 / jnp.sum(jnp.where(is_g, jnp.exp(gl - gmax), 0.0), axis=-1, keepdims=True)
    lo = N_GROUPS + gsel * EXPERTS_PER_GROUP
    in_grp = (lane >= lo) & (lane < lo + EXPERTS_PER_GROUP)
    el = jnp.where(in_grp, lg, neg)
    m1 = jnp.max(el, axis=-1, keepdims=True)
    i1 = jnp.min(jnp.where(in_grp & (el == m1), lane, LANES), axis=-1, keepdims=True)
    el2 = jnp.where(lane == i1, neg, el)
    m2 = jnp.max(el2, axis=-1, keepdims=True)
    i2 = jnp.min(jnp.where(in_grp & (el2 == m2), lane, LANES), axis=-1, keepdims=True)
    e2 = jnp.exp(m2 - m1)
    t1 = pg / (1.0 + e2)
    first = i1 < i2
    ea = jnp.minimum(i1, i2) - lo
    eb = jnp.maximum(i1, i2) - lo
    bucket = gsel * PAIRS_PER_GROUP + jnp.right_shift(ea * (7 - ea), 1) + (eb - ea - 1)
    w_a = jnp.where(first, t1, t1 * e2)
    w_b = jnp.where(first, t1 * e2, t1)
    route = (jnp.where(lane == 0, bucket.astype(F32), 0.0) + jnp.where(lane == 1, w_a, 0.0)
             + jnp.where(lane == 2, w_b, 0.0))
    h_ref[:, d:d + LANES] = route
    bk_ref[...] = route


def _merge(rows, mod_l, o_mla, o_gqa, y_ssm, gates, wts):
    r, d = rows.shape
    nt = r // ROW_TILE
    row_spec = lambda w: pl.BlockSpec((ROW_TILE, w), lambda i: (i, 0))
    full = lambda a: pl.BlockSpec(a.shape, lambda i: (0,) * a.ndim)
    consts = [wts["w_glu"], wts["b_glu"], wts["w_br_mla"], wts["w_br_ssm"], wts["w_br_gqa"], wts["w_out"],
              wts["g_post_mix"], wts["g_pre_ffn"], wts["w_rt_hi"], wts["w_rt_lo"], wts["b_rt"]]
    return pl.pallas_call(
        _merge_kernel,
        grid=(nt,),
        in_specs=[row_spec(d), full(mod_l),
                  pl.BlockSpec(o_mla.shape[:2] + (ROW_TILE,), lambda i: (0, 0, i)),
                  pl.BlockSpec(o_gqa.shape[:2] + (ROW_TILE,), lambda i: (0, 0, i)),
                  row_spec(SSM_W), row_spec(3 * d)] + [full(a) for a in consts],
        out_specs=[row_spec(d), row_spec(d + LANES), row_spec(LANES)],
        out_shape=[jax.ShapeDtypeStruct((r, d), F32), jax.ShapeDtypeStruct((r, d + LANES), F32),
                   jax.ShapeDtypeStruct((r, LANES), F32)],
        compiler_params=_cparams("parallel"),
        name="merge",
    )(rows, mod_l, o_mla, o_gqa, y_ssm, gates, *consts)


def _row_copy(idx, r, src_hbm, buf, sem, slot):
    return pltpu.make_async_copy(src_hbm.at[pl.ds(idx, 1)], buf.at[slot, pl.ds(r, 1)], sem.at[slot])


def _gathered_tile(idx_ref, src_hbm, buf, sem):
    i = pl.program_id(0)
    last = pl.num_programs(0) - 1
    depth, n = buf.shape[0], buf.shape[1]
    ahead = depth - 1

    def start(step, dst_slot):
        for r in range(n):
            _row_copy(idx_ref[jnp.minimum(step, last) * n + r], r, src_hbm, buf, sem, dst_slot).start()

    def wait(dst_slot):
        for r in range(n):
            _row_copy(0, r, src_hbm, buf, sem, dst_slot).wait()

    @pl.when(i == 0)
    def _():
        for k in range(ahead):
            start(k, k)

    slot = i % depth
    wait(slot)
    tile = buf[slot]

    def finish():
        start(i + ahead, (i + ahead) % depth)

        @pl.when(i == last)
        def _():
            for k in range(1, depth):
                wait((i + k) % depth)

    return tile, finish


def _moe_kernel(src_ref, ea_ref, eb_ref, tok_hbm, wga_ref, wua_ref, wda_ref, wgb_ref, wub_ref, wdb_ref,
                gpost_ref, o_ref, buf, sem):
    tok, finish = _gathered_tile(src_ref, tok_hbm, buf, sem)
    d = o_ref.shape[1]
    h = tok[:, 0:d].astype(BF16)
    rt = tok[:, d:d + LANES]
    lane = lax.broadcasted_iota(jnp.int32, rt.shape, 1)
    w_a = jnp.sum(jnp.where(lane == 1, rt, 0.0), axis=-1, keepdims=True)
    w_b = jnp.sum(jnp.where(lane == 2, rt, 0.0), axis=-1, keepdims=True)

    def expert(wg_ref, wu_ref, wd_ref):
        a = _dot(h, wg_ref[...])
        hid = (a * _sigmoid(a)) * _dot(h, wu_ref[...])
        return _dot(hid.astype(BF16), wd_ref[...])

    y = w_a * expert(wga_ref, wua_ref, wda_ref) + w_b * expert(wgb_ref, wub_ref, wdb_ref)
    o_ref[...] = _rms(y, gpost_ref[...]).astype(o_ref.dtype)
    finish()


def _moe(tokens, src, tile_ea, tile_eb, wg, wu, wd, g_post):
    w = tokens.shape[1]
    d = w - LANES
    ne, _, de = wg.shape
    nt = src.shape[0] // GATHER_TILE
    w_in = lambda sel: pl.BlockSpec((None, d, de), lambda i, src, ea, eb: ((ea, eb)[sel][i], 0, 0))
    w_out = lambda sel: pl.BlockSpec((None, de, d), lambda i, src, ea, eb: ((ea, eb)[sel][i], 0, 0))
    return pl.pallas_call(
        _moe_kernel,
        grid_spec=pltpu.PrefetchScalarGridSpec(
            num_scalar_prefetch=3,
            grid=(nt,),
            in_specs=[
                pl.BlockSpec(memory_space=pl.ANY),
                w_in(0), w_in(0), w_out(0), w_in(1), w_in(1), w_out(1),
                pl.BlockSpec(g_post.shape, lambda i, src, ea, eb: (0, 0)),
            ],
            out_specs=pl.BlockSpec((GATHER_TILE, d), lambda i, src, ea, eb: (i, 0)),
            scratch_shapes=[pltpu.VMEM((GATHER_DEPTH, GATHER_TILE, w), F32),
                            pltpu.SemaphoreType.DMA((GATHER_DEPTH,))],
        ),
        out_shape=jax.ShapeDtypeStruct((src.shape[0], d), F32),
        compiler_params=_cparams("arbitrary"),
        name="moe",
    )(src, tile_ea, tile_eb, tokens, wg, wu, wd, wg, wu, wd, g_post)


def _dispatch(bucket, n_rows):
    nb = N_GROUPS * PAIRS_PER_GROUP
    n_tiles = n_rows // MOE_TILE + nb
    onehot = (bucket[:, None] == jnp.arange(nb, dtype=jnp.int32)[None, :]).astype(jnp.int32)
    counts = jnp.sum(onehot, axis=0)
    rank = jnp.sum((jnp.cumsum(onehot, axis=0) - onehot) * onehot, axis=1)
    tiles_per = (counts + MOE_TILE - 1) // MOE_TILE
    tile_end = jnp.cumsum(tiles_per)
    start = (tile_end - tiles_per) * MOE_TILE
    pos = jnp.take(start, bucket) + rank
    src = jnp.zeros((n_tiles * MOE_TILE,), jnp.int32).at[pos].set(jnp.arange(n_rows, dtype=jnp.int32))
    tile_id = jnp.arange(n_tiles, dtype=jnp.int32)
    tile_bucket = jnp.minimum(jnp.searchsorted(tile_end, tile_id, side='right'), nb - 1).astype(jnp.int32)
    tile_used = (tile_id < tile_end[-1]).astype(jnp.int32)
    last_bucket = jnp.take(tile_bucket, jnp.maximum(tile_end[-1] - 1, 0))
    tile_bucket = jnp.where(tile_used == 1, tile_bucket, last_bucket)
    pair_a = jnp.asarray([0, 0, 0, 1, 1, 2], jnp.int32)
    pair_b = jnp.asarray([1, 2, 3, 2, 3, 3], jnp.int32)
    grp = tile_bucket // PAIRS_PER_GROUP
    tile_ea = grp * EXPERTS_PER_GROUP + jnp.take(pair_a, tile_bucket % PAIRS_PER_GROUP)
    tile_eb = grp * EXPERTS_PER_GROUP + jnp.take(pair_b, tile_bucket % PAIRS_PER_GROUP)
    return pos, src, tile_ea, tile_eb


def _residual_kernel(pos_ref, x_ref, z_hbm, mod_ref, o_ref, buf, sem):
    z, finish = _gathered_tile(pos_ref, z_hbm, buf, sem)
    o_ref[...] = x_ref[...] + _mod_row(mod_ref, 5) * z
    finish()


def _residual(rows, z_sorted, pos, mod_l):
    r, d = rows.shape
    row_spec = pl.BlockSpec((GATHER_TILE, d), lambda i, pos: (i, 0))
    return pl.pallas_call(
        _residual_kernel,
        grid_spec=pltpu.PrefetchScalarGridSpec(
            num_scalar_prefetch=1,
            grid=(r // GATHER_TILE,),
            in_specs=[row_spec, pl.BlockSpec(memory_space=pl.ANY),
                      pl.BlockSpec(mod_l.shape, lambda i, pos: (0, 0))],
            out_specs=row_spec,
            scratch_shapes=[pltpu.VMEM((GATHER_DEPTH, GATHER_TILE, d), F32),
                            pltpu.SemaphoreType.DMA((GATHER_DEPTH,))],
        ),
        out_shape=jax.ShapeDtypeStruct((r, d), F32),
        compiler_params=_cparams("arbitrary"),
        name="residual",
    )(pos, rows, z_sorted, mod_l)


def _rope_tables(seq, n_ctx):
    n_rows = seq // GRID_W

    def pattern(width):
        half = width // 4
        freqs = (np.float32(ROPE_THETA) ** (-np.arange(half, dtype=np.float32) / np.float32(half))).astype(np.float32)
        ar = np.arange(n_rows, dtype=np.float32)[:, None] * freqs[None, :]
        ac = np.arange(GRID_W, dtype=np.float32)[:, None] * freqs[None, :]
        by_row = lambda tab: jnp.broadcast_to(jnp.asarray(tab, F32)[:, None, :], (n_rows, GRID_W, half)).reshape(seq, half)
        by_col = lambda tab: jnp.broadcast_to(jnp.asarray(tab, F32)[None, :, :], (n_rows, GRID_W, half)).reshape(seq, half)
        cr, sr, cc, sc = by_row(np.cos(ar)), by_row(np.sin(ar)), by_col(np.cos(ac)), by_col(np.sin(ac))
        c = jnp.concatenate([cr, cr, cc, cc], axis=1)
        s = jnp.concatenate([-sr, sr, -sc, sc], axis=1)
        return c, s

    c32, s32 = pattern(MLA_ROPE)
    c64, s64 = pattern(GQA_HD)
    one = jnp.ones((seq, 1), F32)
    cm = jnp.concatenate([one * jnp.ones((1, MLA_NOPE)), c32, one * jnp.ones((1, 32))], axis=1)
    sm = jnp.concatenate([jnp.zeros((seq, MLA_NOPE)), s32, jnp.zeros((seq, 32))], axis=1)
    cg = jnp.concatenate([c64, c64], axis=1)
    sg = jnp.concatenate([s64, s64], axis=1)
    ctx_c = jnp.ones((n_ctx, LANES), F32)
    ctx_s = jnp.zeros((n_ctx, LANES), F32)
    cat = lambda a, b: jnp.concatenate([a, b], axis=0)
    return dict(cm=cat(ctx_c, cm), sm=cat(ctx_s, sm), cg=cat(ctx_c, cg), sg=cat(ctx_s, sg))


def _swap_perm(width):
    q = width // 4
    return np.concatenate([np.arange(q, 2 * q), np.arange(0, q), np.arange(3 * q, 4 * q), np.arange(2 * q, 3 * q)])


def _layer_weights(l, p):
    d = p["w_in"].shape[1]
    w_in = p["w_in"][l]
    o = 0
    cq = w_in[:, o:o + MLA_Q_RANK]; o += MLA_Q_RANK
    ckv = w_in[:, o:o + MLA_KV_RANK]; o += MLA_KV_RANK
    kr = w_in[:, o:o + MLA_ROPE]; o += MLA_ROPE
    ssm = w_in[:, o:o + SSM_W]; o += SSM_W
    gq = w_in[:, o:o + GQA_W]; o += GQA_W
    gk = w_in[:, o:o + GQA_KV_HEADS * GQA_HD]; o += GQA_KV_HEADS * GQA_HD
    gv = w_in[:, o:o + GQA_KV_HEADS * GQA_HD]; o += GQA_KV_HEADS * GQA_HD
    gate = w_in[:, o:]
    p32 = _swap_perm(MLA_ROPE)
    p64 = _swap_perm(GQA_HD)
    perm_q = np.concatenate([p64 + GQA_HD * h for h in range(GQA_HEADS)])
    perm_k = np.concatenate([p64 + GQA_HD * h for h in range(GQA_KV_HEADS)])
    z = lambda n: jnp.zeros((d, n), F32)
    kr128 = jnp.concatenate([z(MLA_NOPE), kr, z(32)], axis=1)
    krs128 = jnp.concatenate([z(MLA_NOPE), kr[:, p32], z(32)], axis=1)
    w_cat = jnp.concatenate([cq, ckv, kr128, krs128, ssm, gq, gq[:, perm_q], gk, gk[:, perm_k], gv, gate],
                            axis=1).astype(BF16)

    w_uq = p["w_uq"][l].reshape(MLA_Q_RANK, MLA_HEADS, MLA_NOPE + MLA_ROPE)
    zq = jnp.zeros((MLA_Q_RANK, MLA_HEADS, 32), F32)
    wuq = jnp.concatenate([w_uq, zq], axis=2).reshape(MLA_Q_RANK, -1)
    wuqs = jnp.concatenate([jnp.zeros((MLA_Q_RANK, MLA_HEADS, MLA_NOPE), F32),
                            w_uq[:, :, MLA_NOPE:][:, :, p32], zq], axis=2).reshape(MLA_Q_RANK, -1)
    w_ukv = p["w_ukv"][l].reshape(MLA_KV_RANK, MLA_HEADS, MLA_NOPE + MLA_V)
    wuk = jnp.concatenate([w_ukv[:, :, :MLA_NOPE], jnp.zeros((MLA_KV_RANK, MLA_HEADS, 64), F32)],
                          axis=2).reshape(MLA_KV_RANK, -1)
    wuv = w_ukv[:, :, MLA_NOPE:].reshape(MLA_KV_RANK, -1)

    g_qn = p["g_qn"][l]
    g_kn = p["g_kn"][l]
    blk = np.arange(GQA_W) // GQA_HD
    ones = jnp.asarray((blk[:, None] == blk[None, :]).astype(np.float32), BF16)

    w_rt = jnp.zeros((d, LANES), F32)
    w_rt = w_rt.at[:, :N_GROUPS].set(p["w_group"][l]).at[:, N_GROUPS:N_GROUPS + N_EXPERTS].set(p["w_router"][l])
    w_rt_hi = w_rt.astype(BF16)
    w_rt_lo = (w_rt - w_rt_hi.astype(F32)).astype(BF16)
    b_rt = jnp.zeros((1, LANES), F32)
    b_rt = b_rt.at[0, :N_GROUPS].set(p["b_group"][l]).at[0, N_GROUPS:N_GROUPS + N_EXPERTS].set(p["b_router"][l])

    row = lambda v: v.reshape(1, -1).astype(F32)
    return dict(
        w_cat=w_cat, wuq=wuq.astype(BF16), wuqs=wuqs.astype(BF16), wuk=wuk.astype(BF16), wuv=wuv.astype(BF16),
        g_cq=row(p["g_cq"][l]), g_ckv=row(p["g_ckv"][l]),
        g_q=row(jnp.tile(g_qn, GQA_HEADS)), g_qs=row(jnp.tile(g_qn[p64], GQA_HEADS)),
        g_k=row(jnp.tile(g_kn, GQA_KV_HEADS)), g_ks=row(jnp.tile(g_kn[p64], GQA_KV_HEADS)),
        ones=ones,
        w_glu=p["w_glu"][l].astype(BF16), b_glu=row(p["b_glu"][l]),
        w_br_mla=p["w_br_mla"][l].astype(BF16), w_br_ssm=p["w_br_ssm"][l].astype(BF16),
        w_br_gqa=p["w_br_gqa"][l].astype(BF16), w_out=p["w_out"][l].astype(BF16),
        g_post_mix=row(p["g_post_mix"][l]), g_pre_ffn=row(p["g_pre_ffn"][l]),
        w_rt_hi=w_rt_hi, w_rt_lo=w_rt_lo, b_rt=b_rt,
    )


def kernel(x, c, ctx, c_ctx, w_mod, b_mod, g_pre_mix, g_post_mix, g_pre_ffn, g_post_ffn, w_in, g_cq, g_ckv, w_uq, w_ukv, g_qn, g_kn, ssm_a_re, ssm_a_im, ssm_log_dt, ssm_b_re, ssm_b_im, ssm_c_re, ssm_c_im, ssm_d, w_glu, b_glu, w_br_mla, w_br_ssm, w_br_gqa, w_out, w_group, b_group, w_router, b_router, w_exp_gate, w_exp_up, w_exp_down):
    assert x.shape[0] == 1 and ctx.shape[0] == 1
    seq, d = x.shape[1], x.shape[2]
    n_ctx = ctx.shape[1]
    assert n_ctx == ROW_TILE and seq % (SSM_CHUNK * SSM_SUPER) == 0
    depth = w_in.shape[0]
    params = dict(w_in=w_in, g_cq=g_cq, g_ckv=g_ckv, w_uq=w_uq, w_ukv=w_ukv, g_qn=g_qn, g_kn=g_kn,
                  w_glu=w_glu, b_glu=b_glu, w_br_mla=w_br_mla, w_br_ssm=w_br_ssm, w_br_gqa=w_br_gqa,
                  w_out=w_out, g_post_mix=g_post_mix, g_pre_ffn=g_pre_ffn,
                  w_group=w_group, b_group=b_group, w_router=w_router, b_router=b_router)

    cond8 = jnp.zeros((8, d), F32).at[0].set(c[0]).at[1].set(c_ctx)
    mods = _modulation(cond8, w_mod, b_mod)
    tabs = _rope_tables(seq, n_ctx)
    rows = jnp.concatenate([ctx[0], x[0]], axis=0)
    r = rows.shape[0]

    for l in range(depth):
        wts = _layer_weights(l, params)
        mod_l = mods[l]
        qm, km, vm, qg, kg, vg, u, gates = _premix(rows, mod_l, g_pre_mix[l].reshape(1, d), wts, tabs)

        o_mla = _attention(qm, km, vm, hb=2, nsub=16, ahead=6)
        o_gqa = _attention(qg, kg, vg, hb=GQA_HEADS // GQA_KV_HEADS, nsub=8, ahead=4)

        mats = _s5_matrices(ssm_a_re[l], ssm_a_im[l], ssm_log_dt[l], ssm_b_re[l], ssm_b_im[l],
                            ssm_c_re[l], ssm_c_im[l], ssm_d[l])
        ut = jnp.transpose(u.reshape(r // SSM_CHUNK, SSM_CHUNK, SSM_W), (1, 0, 2))
        yt = _s5_scan(ut, mats, n_ctx // SSM_CHUNK)
        y_ssm = jnp.transpose(yt, (1, 0, 2)).reshape(r, SSM_W)

        rows, tokens, route = _merge(rows, mod_l, o_mla, o_gqa, y_ssm, gates, wts)
        pos, src, tile_ea, tile_eb = _dispatch(route[:, 0].astype(jnp.int32), r)
        z_sorted = _moe(tokens, src, tile_ea, tile_eb,
                        w_exp_gate[l].astype(BF16), w_exp_up[l].astype(BF16), w_exp_down[l].astype(BF16),
                        g_post_ffn[l].reshape(1, d))
        rows = _residual(rows, z_sorted, pos, mod_l)

    return rows[n_ctx:].reshape(1, seq, d)
```

```python
import functools
import math

import jax
import jax.numpy as jnp
import numpy as np
from jax import lax
from jax.experimental import pallas as pl
from jax.experimental.pallas import tpu as pltpu

F32 = jnp.float32
BF16 = jnp.bfloat16

GRID_W = 64
ROPE_THETA = 10000.0
NORM_EPS = 1e-6
N_MOD = 6

MLA_HEADS = 8
MLA_NOPE = 64
MLA_ROPE = 32
MLA_V = 64
MLA_Q_RANK = 256
MLA_KV_RANK = 128
MLA_SCALE = (MLA_NOPE + MLA_ROPE) ** -0.5
MLA_HEAD_PAD = 128

SSM_GROUP = 16
SSM_GROUPS = 32
SSM_STATE = 64
SSM_W = SSM_GROUP * SSM_GROUPS
SSM_DIRS = 2
SSM_CHUNK = 16
SSM_SUPER = 16
SSM_CW = SSM_CHUNK * SSM_GROUP

GQA_HEADS = 8
GQA_KV_HEADS = 2
GQA_HD = 64
GQA_W = GQA_HEADS * GQA_HD
GQA_SCALE = GQA_HD ** -0.5
LOG2E = math.log2(math.e)

N_GROUPS = 4
EXPERTS_PER_GROUP = 4
N_EXPERTS = N_GROUPS * EXPERTS_PER_GROUP
D_EXPERT = 512
PAIRS_PER_GROUP = EXPERTS_PER_GROUP * (EXPERTS_PER_GROUP - 1) // 2
GATHER_TILE = 256
GATHER_DEPTH = 3
MOE_TILE = GATHER_TILE

ROW_TILE = 256
LANES = 128
V_ROWS = 80
VMEM_LIMIT = 56 * 1024 * 1024


def _cparams(*sem):
    return pltpu.CompilerParams(dimension_semantics=sem, vmem_limit_bytes=VMEM_LIMIT)


def _dot(a, b):
    return jnp.dot(a, b, preferred_element_type=F32)


def _dot_nt(a, b):
    return lax.dot_general(a, b, (((1,), (1,)), ((), ())), preferred_element_type=F32)


def _dot_f32(a, b):
    a_hi = a.astype(BF16)
    a_lo = (a - a_hi.astype(F32)).astype(BF16)
    return _dot(a_hi, b) + _dot(a_lo, b)


def _rms(x, g):
    return x * lax.rsqrt(jnp.mean(x * x, axis=-1, keepdims=True) + NORM_EPS) * g


def _sigmoid(x):
    return 1.0 / (1.0 + jnp.exp(-x))


def _mod_kernel(cond_ref, w_ref, b_ref, o_ref):
    a = cond_ref[...]
    s = a * _sigmoid(a)
    w = w_ref[...]
    s_hi = s.astype(BF16)
    s_lo = (s - s_hi.astype(F32)).astype(BF16)
    w_hi = w.astype(BF16)
    w_lo = (w - w_hi.astype(F32)).astype(BF16)
    acc = _dot(s_hi, w_hi) + _dot(s_lo, w_hi) + _dot(s_hi, w_lo)
    o_ref[...] = acc + b_ref[...]


def _modulation(cond8, w_mod, b_mod):
    nl, d, n = w_mod.shape
    tn = 1536
    return pl.pallas_call(
        _mod_kernel,
        grid=(nl, n // tn),
        in_specs=[
            pl.BlockSpec((8, d), lambda l, j: (0, 0)),
            pl.BlockSpec((None, d, tn), lambda l, j: (l, 0, j)),
            pl.BlockSpec((None, 1, tn), lambda l, j: (l, 0, j)),
        ],
        out_specs=pl.BlockSpec((None, 8, tn), lambda l, j: (l, 0, j)),
        out_shape=jax.ShapeDtypeStruct((nl, 8, n), F32),
        compiler_params=_cparams("parallel", "parallel"),
        name="modulation",
    )(cond8, w_mod, b_mod.reshape(nl, 1, n))


def _mod_row(mod_ref, idx):
    m = mod_ref[...]
    d = m.shape[1] // N_MOD
    lat = m[0:1, idx * d:(idx + 1) * d]
    ctx = m[1:2, idx * d:(idx + 1) * d]
    return jnp.where(pl.program_id(0) == 0, ctx, lat)


_SEG = {}
_off = 0
for _name, _w in (("cq", 256), ("ckv", 128), ("kr", 128), ("krs", 128), ("ssm", 512), ("gq", 512),
                  ("gqs", 512), ("gk", 128), ("gks", 128), ("gv", 128), ("gate", 3072)):
    _SEG[_name] = (_off, _off + _w)
    _off += _w
W_CAT = _off


def _premix_kernel(x_ref, mod_ref, gpre_ref, w_ref, wuq_ref, wuqs_ref, wuk_ref, wuv_ref,
                   gcq_ref, gckv_ref, gq_ref, gqs_ref, gk_ref, gks_ref, ones_ref,
                   cm_ref, sm_ref, cg_ref, sg_ref,
                   qm_ref, km_ref, vm_ref, qg_ref, kg_ref, vg_ref, u_ref, gate_ref):
    x = x_ref[...]
    h = _rms(x, gpre_ref[...]) * (1.0 + _mod_row(mod_ref, 1)) + _mod_row(mod_ref, 0)
    hb = h.astype(BF16)

    def proj(name):
        a, b = _SEG[name]
        return _dot(hb, w_ref[:, a:b])

    cm = cm_ref[...]
    sm = sm_ref[...]
    cg = cg_ref[...]
    sg = sg_ref[...]

    cqn = _rms(proj("cq"), gcq_ref[...]).astype(BF16)
    ckvn = _rms(proj("ckv"), gckv_ref[...]).astype(BF16)
    q = _dot(cqn, wuq_ref[...])
    qs = _dot(cqn, wuqs_ref[...])
    kn = _dot(ckvn, wuk_ref[...])
    kr = proj("kr") * cm + proj("krs") * sm
    tr = x.shape[0]
    vrow = lax.broadcasted_iota(jnp.int32, (V_ROWS - MLA_V, tr), 0)
    v_tail = jnp.where(vrow == 0, 1.0, 0.0).astype(BF16)
    for hd in range(MLA_HEADS):
        sl = slice(hd * MLA_HEAD_PAD, (hd + 1) * MLA_HEAD_PAD)
        qh = (q[:, sl] * cm + qs[:, sl] * sm) * (MLA_SCALE * LOG2E)
        qm_ref[hd] = qh.T.astype(BF16)
        km_ref[hd] = (kn[:, sl] + kr).astype(BF16)
    vmt = _dot(ckvn, wuv_ref[...]).T
    for hd in range(MLA_HEADS):
        vm_ref[hd, 0:MLA_V, :] = vmt[hd * MLA_V:(hd + 1) * MLA_V, :].astype(BF16)
        vm_ref[hd, MLA_V:V_ROWS, :] = v_tail

    ones = ones_ref[...]

    def head_rms_scale(v, width):
        ms = _dot_f32(v * v, ones[:width, :width]) * (1.0 / GQA_HD)
        return lax.rsqrt(ms + NORM_EPS)

    gq = proj("gq")
    gqs = proj("gqs")
    rq = head_rms_scale(gq, GQA_W)
    cg4 = jnp.concatenate([cg] * (GQA_W // LANES), axis=1)
    sg4 = jnp.concatenate([sg] * (GQA_W // LANES), axis=1)
    qg = rq * (gq * gq_ref[...] * cg4 + gqs * gqs_ref[...] * sg4)
    qgt = (qg * (GQA_SCALE * LOG2E)).T.astype(BF16)
    zeros_q = jnp.zeros((GQA_HD, tr), BF16)
    for hd in range(GQA_HEADS):
        kvh = hd // (GQA_HEADS // GQA_KV_HEADS)
        qh = qgt[hd * GQA_HD:(hd + 1) * GQA_HD, :]
        qg_ref[hd] = jnp.concatenate([qh, zeros_q] if kvh == 0 else [zeros_q, qh], axis=0)
    gk = proj("gk")
    gks = proj("gks")
    rk = head_rms_scale(gk, GQA_KV_HEADS * GQA_HD)
    kg_ref[0] = (rk * (gk * gk_ref[...] * cg + gks * gks_ref[...] * sg)).astype(BF16)
    vgt = proj("gv").T
    for kvh in range(GQA_KV_HEADS):
        vg_ref[kvh, 0:GQA_HD, :] = vgt[kvh * GQA_HD:(kvh + 1) * GQA_HD, :].astype(BF16)
        vg_ref[kvh, GQA_HD:V_ROWS, :] = v_tail

    u_ref[...] = proj("ssm").astype(BF16)
    gate_ref[...] = _sigmoid(proj("gate")).astype(BF16)


def _premix(rows, mod_l, g_pre, wts, tabs):
    r, d = rows.shape
    nt = r // ROW_TILE
    row_spec = lambda w: pl.BlockSpec((ROW_TILE, w), lambda i: (i, 0))
    full = lambda a: pl.BlockSpec(a.shape, lambda i: (0,) * a.ndim)
    consts = [g_pre, wts["w_cat"], wts["wuq"], wts["wuqs"], wts["wuk"], wts["wuv"],
              wts["g_cq"], wts["g_ckv"], wts["g_q"], wts["g_qs"], wts["g_k"], wts["g_ks"], wts["ones"]]
    t_spec = lambda nh, f: pl.BlockSpec((nh, f, ROW_TILE), lambda i: (0, 0, i))
    k_spec = lambda nh, f: pl.BlockSpec((nh, ROW_TILE, f), lambda i: (0, i, 0))
    outs = [(t_spec(MLA_HEADS, MLA_HEAD_PAD), (MLA_HEADS, MLA_HEAD_PAD, r)),
            (k_spec(MLA_HEADS, MLA_HEAD_PAD), (MLA_HEADS, r, MLA_HEAD_PAD)),
            (t_spec(MLA_HEADS, V_ROWS), (MLA_HEADS, V_ROWS, r)),
            (t_spec(GQA_HEADS, GQA_KV_HEADS * GQA_HD), (GQA_HEADS, GQA_KV_HEADS * GQA_HD, r)),
            (k_spec(1, GQA_KV_HEADS * GQA_HD), (1, r, GQA_KV_HEADS * GQA_HD)),
            (t_spec(GQA_KV_HEADS, V_ROWS), (GQA_KV_HEADS, V_ROWS, r)),
            (row_spec(SSM_W), (r, SSM_W)),
            (row_spec(3 * d), (r, 3 * d))]
    return pl.pallas_call(
        _premix_kernel,
        grid=(nt,),
        in_specs=[row_spec(d), full(mod_l)] + [full(a) for a in consts] + [row_spec(LANES)] * 4,
        out_specs=[s for s, _ in outs],
        out_shape=[jax.ShapeDtypeStruct(shp, BF16) for _, shp in outs],
        compiler_params=_cparams("parallel"),
        name="premix",
    )(rows, mod_l, *consts, tabs["cm"], tabs["sm"], tabs["cg"], tabs["sg"])


def _attn_kernel(q_ref, k_ref, v_ref, o_ref, s_ref, *, tk, nsub, n_ctx, n_iter, dv, hb, k_shared, v_shared,
                 ahead):
    n_items = hb * nsub
    tq = q_ref.shape[2]

    def scores(h, off, size):
        return _dot(k_ref[0 if k_shared else h, pl.ds(off, size), :], q_ref[h])

    def update(h, m, acc, s, off, size):
        m_new = jnp.maximum(m, jnp.max(s, axis=0, keepdims=True))
        p = jnp.exp2(s - m_new).astype(BF16)
        pv = _dot(v_ref[0 if v_shared else h, :, pl.ds(off, size)], p)
        return m_new, jnp.exp2(m - m_new) * acc + pv

    def issue(item, base):
        h, sub = item % hb, item // hb
        s_ref[item] = scores(h, pl.multiple_of(base + sub * tk, LANES), tk)

    m0 = jnp.full((1, tq), -1e30, F32)
    acc0 = jnp.zeros((V_ROWS, tq), F32)
    s_ctx = [scores(h, 0, n_ctx) for h in range(hb)]
    for item in range(ahead):
        issue(item, n_ctx)
    carry = tuple(update(h, m0, acc0, s_ctx[h], 0, n_ctx) for h in range(hb))

    def body(j, carry):
        carry = list(carry)
        base = n_ctx + j * (nsub * tk)
        base_next = n_ctx + jnp.minimum(j + 1, n_iter - 1) * (nsub * tk)
        for item in range(n_items):
            nxt = item + ahead
            if nxt < n_items:
                issue(nxt, base)
            else:
                issue(nxt - n_items, base_next)
            h, sub = item % hb, item // hb
            off = pl.multiple_of(base + sub * tk, LANES)
            carry[h] = update(h, carry[h][0], carry[h][1], s_ref[item], off, tk)
        return tuple(carry)

    carry = lax.fori_loop(0, jnp.where(pl.program_id(1) == 0, 0, n_iter), body, carry)
    for h in range(hb):
        acc = carry[h][1]
        o_ref[h] = acc[:dv] / acc[dv:dv + 1]


def _attention(qt, k, vt, *, hb, nsub, tq=256, tk=512, ahead=2):
    nh, dq, r = qt.shape
    k_shared = k.shape[0] != nh
    v_shared = vt.shape[0] != nh
    assert nh % hb == 0 and k.shape[0] in (nh, 1) and vt.shape[0] in (nh, nh // hb)
    dv = V_ROWS - 16
    assert (r - ROW_TILE) % (tk * nsub) == 0 and ahead < hb * nsub
    kern = functools.partial(_attn_kernel, tk=tk, nsub=nsub, n_ctx=ROW_TILE, n_iter=(r - ROW_TILE) // (tk * nsub),
                             dv=dv, hb=hb, k_shared=k_shared, v_shared=v_shared, ahead=ahead)
    k_spec = (pl.BlockSpec((1, r, dq), lambda g, i: (0, 0, 0)) if k_shared
              else pl.BlockSpec((hb, r, dq), lambda g, i: (g, 0, 0)))
    v_spec = pl.BlockSpec((1 if v_shared else hb, V_ROWS, r), lambda g, i: (g, 0, 0))
    return pl.pallas_call(
        kern,
        grid=(nh // hb, r // tq),
        in_specs=[pl.BlockSpec((hb, dq, tq), lambda g, i: (g, 0, i)), k_spec, v_spec],
        out_specs=pl.BlockSpec((hb, dv, tq), lambda g, i: (g, 0, i)),
        out_shape=jax.ShapeDtypeStruct((nh, dv, r), F32),
        scratch_shapes=[pltpu.VMEM((hb * nsub, tk, tq), F32)],
        compiler_params=_cparams("parallel", "arbitrary"),
        name="attention",
    )(qt, k, vt)


def _s5_kernel(u_ref, blk_ref, bp_ref, cpt_ref, ar_ref, ai_ref, y_ref, xr_ref, xi_ref, *, n_ctx_chunks):
    t, j, _ = u_ref.shape
    npair = t // 2
    ns = xr_ref.shape[1]
    row = lax.broadcasted_iota(jnp.int32, (j, ns), 0)

    def run(reverse):
        ucat = jnp.concatenate([u_ref[k] for k in range(t)], axis=1)
        xc = _dot(ucat, bp_ref[...])
        if reverse:
            shift = j - n_ctx_chunks - 1
            keep = row < j - 1
        else:
            shift = 1
            keep = row >= 1
        xr_ref[...] = jnp.where(keep, pltpu.roll(xc[:, :ns], shift, 0), 0.0)
        xi_ref[...] = jnp.where(keep, pltpu.roll(xc[:, ns:], shift, 0), 0.0)
        step = 1
        k = 0
        while step < j:
            ar = ar_ref[k:k + 1, :]
            ai = ai_ref[k:k + 1, :]
            xr = xr_ref[...]
            xi = xi_ref[...]
            if reverse:
                keep = row < j - step
                sr = jnp.where(keep, pltpu.roll(xr, j - step, 0), 0.0)
                si = jnp.where(keep, pltpu.roll(xi, j - step, 0), 0.0)
            else:
                keep = row >= step
                sr = jnp.where(keep, pltpu.roll(xr, step, 0), 0.0)
                si = jnp.where(keep, pltpu.roll(xi, step, 0), 0.0)
            xr_ref[...] = xr + ar * sr - ai * si
            xi_ref[...] = xi + ar * si + ai * sr
            step *= 2
            k += 1
        xin = jnp.concatenate([xr_ref[...], xi_ref[...]], axis=1)
        if reverse:
            xin = pltpu.roll(xin, n_ctx_chunks, 0)
        xin = xin.astype(BF16)
        for po in range(npair):
            y = _dot_nt(xin, cpt_ref[po * 256:(po + 1) * 256, :])
            pins = range(po, npair) if reverse else range(0, po + 1)
            for pi_ in pins:
                upair = jnp.concatenate([u_ref[2 * pi_], u_ref[2 * pi_ + 1]], axis=1)
                y += _dot(upair, blk_ref[abs(po - pi_)])
            if reverse:
                y_ref[2 * po] += y[:, :128]
                y_ref[2 * po + 1] += y[:, 128:]
            else:
                y_ref[2 * po] = y[:, :128]
                y_ref[2 * po + 1] = y[:, 128:]

    @pl.when(pl.program_id(1) == 0)
    def _():
        run(False)

    @pl.when(pl.program_id(1) == 1)
    def _():
        run(True)


def _s5_scan(ut, mats, n_ctx_chunks):
    t, j, w = ut.shape
    no = w // LANES
    ns = (LANES // SSM_GROUP) * SSM_STATE
    wspec = lambda *tail: pl.BlockSpec((None, None) + tail, lambda o, d: (d, o) + (0,) * len(tail))
    kern = functools.partial(_s5_kernel, n_ctx_chunks=n_ctx_chunks)
    return pl.pallas_call(
        kern,
        grid=(no, SSM_DIRS),
        in_specs=[pl.BlockSpec((t, j, LANES), lambda o, d: (0, 0, o)),
                  wspec(t // 2, 256, 256), wspec(t * LANES, 2 * ns), wspec(t * LANES, 2 * ns),
                  wspec(16, ns), wspec(16, ns)],
        out_specs=pl.BlockSpec((t, j, LANES), lambda o, d: (0, 0, o)),
        out_shape=jax.ShapeDtypeStruct((t, j, w), F32),
        scratch_shapes=[pltpu.VMEM((j, ns), F32)] * 2,
        compiler_params=_cparams("parallel", "arbitrary"),
        name="s5_scan",
    )(ut, mats["blk"], mats["bp"], mats["cpt"], mats["ar"], mats["ai"])


def _s5_matrices(a_re, a_im, log_dt, b_re, b_im, c_re, c_im, d_skip):
    t = SSM_CHUNK
    a_re = a_re.astype(F32)
    a_im = a_im.astype(F32)
    dt = jnp.exp(log_dt.astype(F32))[..., None]
    den = a_re * a_re + a_im * a_im

    def lpow(k):
        kf = jnp.asarray(k, F32)
        mag = jnp.exp(a_re * dt * kf)
        return mag * jnp.cos(a_im * dt * kf), mag * jnp.sin(a_im * dt * kf)

    lr, li = lpow(1.0)
    cr = ((lr - 1.0) * a_re + li * a_im) / den
    ci = (li * a_re - (lr - 1.0) * a_im) / den
    bbr = cr[..., None] * b_re - ci[..., None] * b_im
    bbi = cr[..., None] * b_im + ci[..., None] * b_re
    c_re = c_re.astype(F32)
    c_im = c_im.astype(F32)

    ks = jnp.arange(t + 1, dtype=F32)[:, None, None, None]
    pr, pi = lpow(ks)
    lbr = pr[:t, ..., None] * bbr - pi[:t, ..., None] * bbi
    lbi = pr[:t, ..., None] * bbi + pi[:t, ..., None] * bbr
    kern = (jnp.einsum('dgop,kdgpc->dgkoc', c_re, lbr, precision='highest')
            - jnp.einsum('dgop,kdgpc->dgkoc', c_im, lbi, precision='highest'))
    skip = d_skip.astype(F32).reshape(SSM_GROUPS, SSM_GROUP)
    eye = jnp.eye(SSM_GROUP, dtype=F32)
    kern = kern.at[0, :, 0].add(skip[:, :, None] * eye)
    gl = LANES // SSM_GROUP
    no = SSM_GROUPS // gl
    def widen(compact, col_of, n_cols, row_group, col_group):
        expand = np.zeros((compact.shape[-1], n_cols), np.float32)
        for h in range(gl):
            src_cols = np.arange(compact.shape[-1])
            expand[src_cols, col_of(src_cols, h)] = 1.0
        wide = jnp.einsum('...rj,jk->...rk', compact.astype(BF16), jnp.asarray(expand, BF16),
                          preferred_element_type=BF16)
        rows_g = row_group(lax.broadcasted_iota(jnp.int32, wide.shape[-2:], 0))
        cols_g = col_group(lax.broadcasted_iota(jnp.int32, wide.shape[-2:], 1))
        return jnp.where(rows_g == cols_g, wide, jnp.zeros((), BF16))

    dd = jnp.arange(t // 2)[:, None, None]
    ti = jnp.arange(2)[None, :, None]
    to = jnp.arange(2)[None, None, :]
    lag = jnp.stack([2 * dd + to - ti, 2 * dd + ti - to])
    pick = jax.vmap(lambda kd, ld: jnp.take(kd, jnp.clip(ld, 0, t - 1).reshape(-1), axis=1))(kern, lag)
    pick = pick.reshape(SSM_DIRS, no, gl, t // 2, 2, 2, SSM_GROUP, SSM_GROUP)
    pick = jnp.where((lag >= 0)[:, None, None, :, :, :, None, None], pick, 0.0)
    kc = jnp.transpose(pick, (0, 1, 3, 4, 2, 7, 5, 6)).reshape(SSM_DIRS, no, t // 2, 256, 2 * SSM_GROUP)
    blk = widen(kc, lambda j, h: (j // SSM_GROUP) * LANES + h * SSM_GROUP + j % SSM_GROUP, 256,
                lambda r: (r // SSM_GROUP) % gl, lambda c: (c // SSM_GROUP) % gl)
    def contrib(lb):
        both = jnp.stack([lb[::-1, 0], lb[:, 1]])
        both = both.reshape(SSM_DIRS, t, no, gl, SSM_STATE, SSM_GROUP)
        return jnp.transpose(both, (0, 2, 1, 3, 5, 4)).reshape(SSM_DIRS, no, t * LANES, SSM_STATE)
    bp = widen(jnp.concatenate([contrib(lbr), contrib(lbi)], axis=-1),
               lambda j, h: (j // SSM_STATE) * (gl * SSM_STATE) + h * SSM_STATE + j % SSM_STATE, 2 * gl * SSM_STATE,
               lambda r: (r // SSM_GROUP) % gl, lambda c: (c // SSM_STATE) % gl)
    p1r = pr[1:, :, :, None, :]
    p1i = pi[1:, :, :, None, :]
    rd_r = c_re * p1r - c_im * p1i
    rd_i = -(c_re * p1i + c_im * p1r)
    def readout(rd):
        both = jnp.stack([rd[:, 0], rd[::-1, 1]])
        both = both.reshape(SSM_DIRS, t, no, gl, SSM_GROUP, SSM_STATE)
        return jnp.transpose(both, (0, 2, 1, 3, 4, 5)).reshape(SSM_DIRS, no, t * LANES, SSM_STATE)
    cpt = widen(jnp.concatenate([readout(rd_r), readout(rd_i)], axis=-1),
                lambda j, h: (j // SSM_STATE) * (gl * SSM_STATE) + h * SSM_STATE + j % SSM_STATE, 2 * gl * SSM_STATE,
                lambda r: (r // SSM_GROUP) % gl, lambda c: (c // SSM_STATE) % gl)
    kk = (t * 2.0 ** jnp.minimum(jnp.arange(16), 10)).astype(F32)[:, None, None, None]
    qr, qi = lpow(kk)
    lanes = lambda q: jnp.transpose(q, (1, 2, 0, 3)).reshape(SSM_DIRS, no, gl, 16, SSM_STATE)
    ar = jnp.transpose(lanes(qr), (0, 1, 3, 2, 4)).reshape(SSM_DIRS, no, 16, gl * SSM_STATE)
    ai = jnp.transpose(lanes(qi), (0, 1, 3, 2, 4)).reshape(SSM_DIRS, no, 16, gl * SSM_STATE)
    return dict(blk=blk, bp=bp, cpt=cpt, ar=ar, ai=ai)


def _merge_kernel(x_ref, mod_ref, om_ref, og_ref, y_ref, gate_ref,
                  wglu_ref, bglu_ref, wbm_ref, wbs_ref, wbg_ref, wout_ref,
                  gpost_ref, gffn_ref, wrt_hi_ref, wrt_lo_ref, brt_ref,
                  xo_ref, h_ref, bk_ref):
    d = x_ref.shape[1]
    y = y_ref[...]
    y1 = 0.5 * y * (1.0 + jnp.tanh(math.sqrt(2.0 / math.pi) * (y + 0.044715 * (y * y * y))))
    ssm = y1 * _sigmoid(_dot(y1.astype(BF16), wglu_ref[...]) + bglu_ref[...])
    gate = gate_ref[...]
    o_mla = om_ref[...].reshape(-1, om_ref.shape[2]).T.astype(BF16)
    o_gqa = og_ref[...].reshape(-1, og_ref.shape[2]).T.astype(BF16)
    m = (gate[:, 0:d].astype(F32) * _dot(o_mla, wbm_ref[...])
         + gate[:, d:2 * d].astype(F32) * _dot(ssm.astype(BF16), wbs_ref[...])
         + gate[:, 2 * d:3 * d].astype(F32) * _dot(o_gqa, wbg_ref[...]))
    ymix = _dot(m.astype(BF16), wout_ref[...])
    x = x_ref[...] + _mod_row(mod_ref, 2) * _rms(ymix, gpost_ref[...])
    xo_ref[...] = x
    h = _rms(x, gffn_ref[...]) * (1.0 + _mod_row(mod_ref, 4)) + _mod_row(mod_ref, 3)
    h_ref[:, 0:d] = h

    h_hi = h.astype(BF16)
    h_lo = (h - h_hi.astype(F32)).astype(BF16)
    lg = (_dot(h_hi, wrt_hi_ref[...]) + _dot(h_lo, wrt_hi_ref[...]) + _dot(h_hi, wrt_lo_ref[...])
          + brt_ref[...])
    lane = lax.broadcasted_iota(jnp.int32, lg.shape, 1)
    neg = jnp.float32(-1e30)
    is_g = lane < N_GROUPS
    gl = jnp.where(is_g, lg, neg)
    gmax = jnp.max(gl, axis=-1, keepdims=True)
    gsel = jnp.min(jnp.where(is_g & (gl == gmax), lane, LANES), axis=-1, keepdims=True)
    pg = 1.0 / jnp.sum(jnp.where(is_g, jnp.exp(gl - gmax), 0.0), axis=-1, keepdims=True)
    lo = N_GROUPS + gsel * EXPERTS_PER_GROUP
    in_grp = (lane >= lo) & (lane < lo + EXPERTS_PER_GROUP)
    el = jnp.where(in_grp, lg, neg)
    m1 = jnp.max(el, axis=-1, keepdims=True)
    i1 = jnp.min(jnp.where(in_grp & (el == m1), lane, LANES), axis=-1, keepdims=True)
    el2 = jnp.where(lane == i1, neg, el)
    m2 = jnp.max(el2, axis=-1, keepdims=True)
    i2 = jnp.min(jnp.where(in_grp & (el2 == m2), lane, LANES), axis=-1, keepdims=True)
    e2 = jnp.exp(m2 - m1)
    t1 = pg / (1.0 + e2)
    first = i1 < i2
    ea = jnp.minimum(i1, i2) - lo
    eb = jnp.maximum(i1, i2) - lo
    bucket = gsel * PAIRS_PER_GROUP + jnp.right_shift(ea * (7 - ea), 1) + (eb - ea - 1)
    w_a = jnp.where(first, t1, t1 * e2)
    w_b = jnp.where(first, t1 * e2, t1)
    route = (jnp.where(lane == 0, bucket.astype(F32), 0.0) + jnp.where(lane == 1, w_a, 0.0)
             + jnp.where(lane == 2, w_b, 0.0))
    h_ref[:, d:d + LANES] = route
    bk_ref[...] = route


def _merge(rows, mod_l, o_mla, o_gqa, y_ssm, gates, wts):
    r, d = rows.shape
    nt = r // ROW_TILE
    row_spec = lambda w: pl.BlockSpec((ROW_TILE, w), lambda i: (i, 0))
    full = lambda a: pl.BlockSpec(a.shape, lambda i: (0,) * a.ndim)
    consts = [wts["w_glu"], wts["b_glu"], wts["w_br_mla"], wts["w_br_ssm"], wts["w_br_gqa"], wts["w_out"],
              wts["g_post_mix"], wts["g_pre_ffn"], wts["w_rt_hi"], wts["w_rt_lo"], wts["b_rt"]]
    return pl.pallas_call(
        _merge_kernel,
        grid=(nt,),
        in_specs=[row_spec(d), full(mod_l),
                  pl.BlockSpec(o_mla.shape[:2] + (ROW_TILE,), lambda i: (0, 0, i)),
                  pl.BlockSpec(o_gqa.shape[:2] + (ROW_TILE,), lambda i: (0, 0, i)),
                  row_spec(SSM_W), row_spec(3 * d)] + [full(a) for a in consts],
        out_specs=[row_spec(d), row_spec(d + LANES), row_spec(LANES)],
        out_shape=[jax.ShapeDtypeStruct((r, d), F32), jax.ShapeDtypeStruct((r, d + LANES), F32),
                   jax.ShapeDtypeStruct((r, LANES), F32)],
        compiler_params=_cparams("parallel"),
        name="merge",
    )(rows, mod_l, o_mla, o_gqa, y_ssm, gates, *consts)


def _row_copy(idx, r, src_hbm, buf, sem, slot):
    return pltpu.make_async_copy(src_hbm.at[pl.ds(idx, 1)], buf.at[slot, pl.ds(r, 1)], sem.at[slot])


def _gathered_tile(idx_ref, src_hbm, buf, sem):
    i = pl.program_id(0)
    last = pl.num_programs(0) - 1
    depth, n = buf.shape[0], buf.shape[1]
    ahead = depth - 1

    def start(step, dst_slot):
        for r in range(n):
            _row_copy(idx_ref[jnp.minimum(step, last) * n + r], r, src_hbm, buf, sem, dst_slot).start()

    def wait(dst_slot):
        for r in range(n):
            _row_copy(0, r, src_hbm, buf, sem, dst_slot).wait()

    @pl.when(i == 0)
    def _():
        for k in range(ahead):
            start(k, k)

    slot = i % depth
    wait(slot)
    tile = buf[slot]

    def finish():
        start(i + ahead, (i + ahead) % depth)

        @pl.when(i == last)
        def _():
            for k in range(1, depth):
                wait((i + k) % depth)

    return tile, finish


def _moe_kernel(src_ref, ea_ref, eb_ref, tok_hbm, wga_ref, wua_ref, wda_ref, wgb_ref, wub_ref, wdb_ref,
                gpost_ref, o_ref, buf, sem):
    tok, finish = _gathered_tile(src_ref, tok_hbm, buf, sem)
    d = o_ref.shape[1]
    h = tok[:, 0:d].astype(BF16)
    rt = tok[:, d:d + LANES]
    lane = lax.broadcasted_iota(jnp.int32, rt.shape, 1)
    w_a = jnp.sum(jnp.where(lane == 1, rt, 0.0), axis=-1, keepdims=True)
    w_b = jnp.sum(jnp.where(lane == 2, rt, 0.0), axis=-1, keepdims=True)

    def expert(wg_ref, wu_ref, wd_ref):
        a = _dot(h, wg_ref[...])
        hid = (a * _sigmoid(a)) * _dot(h, wu_ref[...])
        return _dot(hid.astype(BF16), wd_ref[...])

    y = w_a * expert(wga_ref, wua_ref, wda_ref) + w_b * expert(wgb_ref, wub_ref, wdb_ref)
    o_ref[...] = _rms(y, gpost_ref[...]).astype(o_ref.dtype)
    finish()


def _moe(tokens, src, tile_ea, tile_eb, wg, wu, wd, g_post):
    w = tokens.shape[1]
    d = w - LANES
    ne, _, de = wg.shape
    nt = src.shape[0] // GATHER_TILE
    w_in = lambda sel: pl.BlockSpec((None, d, de), lambda i, src, ea, eb: ((ea, eb)[sel][i], 0, 0))
    w_out = lambda sel: pl.BlockSpec((None, de, d), lambda i, src, ea, eb: ((ea, eb)[sel][i], 0, 0))
    return pl.pallas_call(
        _moe_kernel,
        grid_spec=pltpu.PrefetchScalarGridSpec(
            num_scalar_prefetch=3,
            grid=(nt,),
            in_specs=[
                pl.BlockSpec(memory_space=pl.ANY),
                w_in(0), w_in(0), w_out(0), w_in(1), w_in(1), w_out(1),
                pl.BlockSpec(g_post.shape, lambda i, src, ea, eb: (0, 0)),
            ],
            out_specs=pl.BlockSpec((GATHER_TILE, d), lambda i, src, ea, eb: (i, 0)),
            scratch_shapes=[pltpu.VMEM((GATHER_DEPTH, GATHER_TILE, w), F32),
                            pltpu.SemaphoreType.DMA((GATHER_DEPTH,))],
        ),
        out_shape=jax.ShapeDtypeStruct((src.shape[0], d), F32),
        compiler_params=_cparams("arbitrary"),
        name="moe",
    )(src, tile_ea, tile_eb, tokens, wg, wu, wd, wg, wu, wd, g_post)


def _dispatch(bucket, n_rows):
    nb = N_GROUPS * PAIRS_PER_GROUP
    n_tiles = n_rows // MOE_TILE + nb
    onehot = (bucket[:, None] == jnp.arange(nb, dtype=jnp.int32)[None, :]).astype(jnp.int32)
    counts = jnp.sum(onehot, axis=0)
    rank = jnp.sum((jnp.cumsum(onehot, axis=0) - onehot) * onehot, axis=1)
    tiles_per = (counts + MOE_TILE - 1) // MOE_TILE
    tile_end = jnp.cumsum(tiles_per)
    start = (tile_end - tiles_per) * MOE_TILE
    pos = jnp.take(start, bucket) + rank
    src = jnp.zeros((n_tiles * MOE_TILE,), jnp.int32).at[pos].set(jnp.arange(n_rows, dtype=jnp.int32))
    tile_id = jnp.arange(n_tiles, dtype=jnp.int32)
    tile_bucket = jnp.minimum(jnp.searchsorted(tile_end, tile_id, side='right'), nb - 1).astype(jnp.int32)
    tile_used = (tile_id < tile_end[-1]).astype(jnp.int32)
    last_bucket = jnp.take(tile_bucket, jnp.maximum(tile_end[-1] - 1, 0))
    tile_bucket = jnp.where(tile_used == 1, tile_bucket, last_bucket)
    pair_a = jnp.asarray([0, 0, 0, 1, 1, 2], jnp.int32)
    pair_b = jnp.asarray([1, 2, 3, 2, 3, 3], jnp.int32)
    grp = tile_bucket // PAIRS_PER_GROUP
    tile_ea = grp * EXPERTS_PER_GROUP + jnp.take(pair_a, tile_bucket % PAIRS_PER_GROUP)
    tile_eb = grp * EXPERTS_PER_GROUP + jnp.take(pair_b, tile_bucket % PAIRS_PER_GROUP)
    return pos, src, tile_ea, tile_eb


def _residual_kernel(pos_ref, x_ref, z_hbm, mod_ref, o_ref, buf, sem):
    z, finish = _gathered_tile(pos_ref, z_hbm, buf, sem)
    o_ref[...] = x_ref[...] + _mod_row(mod_ref, 5) * z
    finish()


def _residual(rows, z_sorted, pos, mod_l):
    r, d = rows.shape
    row_spec = pl.BlockSpec((GATHER_TILE, d), lambda i, pos: (i, 0))
    return pl.pallas_call(
        _residual_kernel,
        grid_spec=pltpu.PrefetchScalarGridSpec(
            num_scalar_prefetch=1,
            grid=(r // GATHER_TILE,),
            in_specs=[row_spec, pl.BlockSpec(memory_space=pl.ANY),
                      pl.BlockSpec(mod_l.shape, lambda i, pos: (0, 0))],
            out_specs=row_spec,
            scratch_shapes=[pltpu.VMEM((GATHER_DEPTH, GATHER_TILE, d), F32),
                            pltpu.SemaphoreType.DMA((GATHER_DEPTH,))],
        ),
        out_shape=jax.ShapeDtypeStruct((r, d), F32),
        compiler_params=_cparams("arbitrary"),
        name="residual",
    )(pos, rows, z_sorted, mod_l)


def _rope_tables(seq, n_ctx):
    n_rows = seq // GRID_W

    def pattern(width):
        half = width // 4
        freqs = (np.float32(ROPE_THETA) ** (-np.arange(half, dtype=np.float32) / np.float32(half))).astype(np.float32)
        ar = np.arange(n_rows, dtype=np.float32)[:, None] * freqs[None, :]
        ac = np.arange(GRID_W, dtype=np.float32)[:, None] * freqs[None, :]
        by_row = lambda tab: jnp.broadcast_to(jnp.asarray(tab, F32)[:, None, :], (n_rows, GRID_W, half)).reshape(seq, half)
        by_col = lambda tab: jnp.broadcast_to(jnp.asarray(tab, F32)[None, :, :], (n_rows, GRID_W, half)).reshape(seq, half)
        cr, sr, cc, sc = by_row(np.cos(ar)), by_row(np.sin(ar)), by_col(np.cos(ac)), by_col(np.sin(ac))
        c = jnp.concatenate([cr, cr, cc, cc], axis=1)
        s = jnp.concatenate([-sr, sr, -sc, sc], axis=1)
        return c, s

    c32, s32 = pattern(MLA_ROPE)
    c64, s64 = pattern(GQA_HD)
    one = jnp.ones((seq, 1), F32)
    cm = jnp.concatenate([one * jnp.ones((1, MLA_NOPE)), c32, one * jnp.ones((1, 32))], axis=1)
    sm = jnp.concatenate([jnp.zeros((seq, MLA_NOPE)), s32, jnp.zeros((seq, 32))], axis=1)
    cg = jnp.concatenate([c64, c64], axis=1)
    sg = jnp.concatenate([s64, s64], axis=1)
    ctx_c = jnp.ones((n_ctx, LANES), F32)
    ctx_s = jnp.zeros((n_ctx, LANES), F32)
    cat = lambda a, b: jnp.concatenate([a, b], axis=0)
    return dict(cm=cat(ctx_c, cm), sm=cat(ctx_s, sm), cg=cat(ctx_c, cg), sg=cat(ctx_s, sg))


def _swap_perm(width):
    q = width // 4
    return np.concatenate([np.arange(q, 2 * q), np.arange(0, q), np.arange(3 * q, 4 * q), np.arange(2 * q, 3 * q)])


def _layer_weights(l, p):
    d = p["w_in"].shape[1]
    w_in = p["w_in"][l]
    o = 0
    cq = w_in[:, o:o + MLA_Q_RANK]; o += MLA_Q_RANK
    ckv = w_in[:, o:o + MLA_KV_RANK]; o += MLA_KV_RANK
    kr = w_in[:, o:o + MLA_ROPE]; o += MLA_ROPE
    ssm = w_in[:, o:o + SSM_W]; o += SSM_W
    gq = w_in[:, o:o + GQA_W]; o += GQA_W
    gk = w_in[:, o:o + GQA_KV_HEADS * GQA_HD]; o += GQA_KV_HEADS * GQA_HD
    gv = w_in[:, o:o + GQA_KV_HEADS * GQA_HD]; o += GQA_KV_HEADS * GQA_HD
    gate = w_in[:, o:]
    p32 = _swap_perm(MLA_ROPE)
    p64 = _swap_perm(GQA_HD)
    perm_q = np.concatenate([p64 + GQA_HD * h for h in range(GQA_HEADS)])
    perm_k = np.concatenate([p64 + GQA_HD * h for h in range(GQA_KV_HEADS)])
    z = lambda n: jnp.zeros((d, n), F32)
    kr128 = jnp.concatenate([z(MLA_NOPE), kr, z(32)], axis=1)
    krs128 = jnp.concatenate([z(MLA_NOPE), kr[:, p32], z(32)], axis=1)
    w_cat = jnp.concatenate([cq, ckv, kr128, krs128, ssm, gq, gq[:, perm_q], gk, gk[:, perm_k], gv, gate],
                            axis=1).astype(BF16)

    w_uq = p["w_uq"][l].reshape(MLA_Q_RANK, MLA_HEADS, MLA_NOPE + MLA_ROPE)
    zq = jnp.zeros((MLA_Q_RANK, MLA_HEADS, 32), F32)
    wuq = jnp.concatenate([w_uq, zq], axis=2).reshape(MLA_Q_RANK, -1)
    wuqs = jnp.concatenate([jnp.zeros((MLA_Q_RANK, MLA_HEADS, MLA_NOPE), F32),
                            w_uq[:, :, MLA_NOPE:][:, :, p32], zq], axis=2).reshape(MLA_Q_RANK, -1)
    w_ukv = p["w_ukv"][l].reshape(MLA_KV_RANK, MLA_HEADS, MLA_NOPE + MLA_V)
    wuk = jnp.concatenate([w_ukv[:, :, :MLA_NOPE], jnp.zeros((MLA_KV_RANK, MLA_HEADS, 64), F32)],
                          axis=2).reshape(MLA_KV_RANK, -1)
    wuv = w_ukv[:, :, MLA_NOPE:].reshape(MLA_KV_RANK, -1)

    g_qn = p["g_qn"][l]
    g_kn = p["g_kn"][l]
    blk = np.arange(GQA_W) // GQA_HD
    ones = jnp.asarray((blk[:, None] == blk[None, :]).astype(np.float32), BF16)

    w_rt = jnp.zeros((d, LANES), F32)
    w_rt = w_rt.at[:, :N_GROUPS].set(p["w_group"][l]).at[:, N_GROUPS:N_GROUPS + N_EXPERTS].set(p["w_router"][l])
    w_rt_hi = w_rt.astype(BF16)
    w_rt_lo = (w_rt - w_rt_hi.astype(F32)).astype(BF16)
    b_rt = jnp.zeros((1, LANES), F32)
    b_rt = b_rt.at[0, :N_GROUPS].set(p["b_group"][l]).at[0, N_GROUPS:N_GROUPS + N_EXPERTS].set(p["b_router"][l])

    row = lambda v: v.reshape(1, -1).astype(F32)
    return dict(
        w_cat=w_cat, wuq=wuq.astype(BF16), wuqs=wuqs.astype(BF16), wuk=wuk.astype(BF16), wuv=wuv.astype(BF16),
        g_cq=row(p["g_cq"][l]), g_ckv=row(p["g_ckv"][l]),
        g_q=row(jnp.tile(g_qn, GQA_HEADS)), g_qs=row(jnp.tile(g_qn[p64], GQA_HEADS)),
        g_k=row(jnp.tile(g_kn, GQA_KV_HEADS)), g_ks=row(jnp.tile(g_kn[p64], GQA_KV_HEADS)),
        ones=ones,
        w_glu=p["w_glu"][l].astype(BF16), b_glu=row(p["b_glu"][l]),
        w_br_mla=p["w_br_mla"][l].astype(BF16), w_br_ssm=p["w_br_ssm"][l].astype(BF16),
        w_br_gqa=p["w_br_gqa"][l].astype(BF16), w_out=p["w_out"][l].astype(BF16),
        g_post_mix=row(p["g_post_mix"][l]), g_pre_ffn=row(p["g_pre_ffn"][l]),
        w_rt_hi=w_rt_hi, w_rt_lo=w_rt_lo, b_rt=b_rt,
    )


def kernel(x, c, ctx, c_ctx, w_mod, b_mod, g_pre_mix, g_post_mix, g_pre_ffn, g_post_ffn, w_in, g_cq, g_ckv, w_uq, w_ukv, g_qn, g_kn, ssm_a_re, ssm_a_im, ssm_log_dt, ssm_b_re, ssm_b_im, ssm_c_re, ssm_c_im, ssm_d, w_glu, b_glu, w_br_mla, w_br_ssm, w_br_gqa, w_out, w_group, b_group, w_router, b_router, w_exp_gate, w_exp_up, w_exp_down):
    assert x.shape[0] == 1 and ctx.shape[0] == 1
    seq, d = x.shape[1], x.shape[2]
    n_ctx = ctx.shape[1]
    assert n_ctx == ROW_TILE and seq % (SSM_CHUNK * SSM_SUPER) == 0
    depth = w_in.shape[0]
    params = dict(w_in=w_in, g_cq=g_cq, g_ckv=g_ckv, w_uq=w_uq, w_ukv=w_ukv, g_qn=g_qn, g_kn=g_kn,
                  w_glu=w_glu, b_glu=b_glu, w_br_mla=w_br_mla, w_br_ssm=w_br_ssm, w_br_gqa=w_br_gqa,
                  w_out=w_out, g_post_mix=g_post_mix, g_pre_ffn=g_pre_ffn,
                  w_group=w_group, b_group=b_group, w_router=w_router, b_router=b_router)

    cond8 = jnp.zeros((8, d), F32).at[0].set(c[0]).at[1].set(c_ctx)
    mods = _modulation(cond8, w_mod, b_mod)
    tabs = _rope_tables(seq, n_ctx)
    rows = jnp.concatenate([ctx[0], x[0]], axis=0)
    r = rows.shape[0]

    for l in range(depth):
        wts = _layer_weights(l, params)
        mod_l = mods[l]
        qm, km, vm, qg, kg, vg, u, gates = _premix(rows, mod_l, g_pre_mix[l].reshape(1, d), wts, tabs)

        o_mla = _attention(qm, km, vm, hb=2, nsub=16, ahead=6)
        o_gqa = _attention(qg, kg, vg, hb=GQA_HEADS // GQA_KV_HEADS, nsub=8, ahead=4)

        mats = _s5_matrices(ssm_a_re[l], ssm_a_im[l], ssm_log_dt[l], ssm_b_re[l], ssm_b_im[l],
                            ssm_c_re[l], ssm_c_im[l], ssm_d[l])
        ut = jnp.transpose(u.reshape(r // SSM_CHUNK, SSM_CHUNK, SSM_W), (1, 0, 2))
        yt = _s5_scan(ut, mats, n_ctx // SSM_CHUNK)
        y_ssm = jnp.transpose(yt, (1, 0, 2)).reshape(r, SSM_W)

        rows, tokens, route = _merge(rows, mod_l, o_mla, o_gqa, y_ssm, gates, wts)
        pos, src, tile_ea, tile_eb = _dispatch(route[:, 0].astype(jnp.int32), r)
        z_sorted = _moe(tokens, src, tile_ea, tile_eb,
                        w_exp_gate[l].astype(BF16), w_exp_up[l].astype(BF16), w_exp_down[l].astype(BF16),
                        g_post_ffn[l].reshape(1, d))
        rows = _residual(rows, z_sorted, pos, mod_l)

    return rows[n_ctx:].reshape(1, seq, d)
```

```python
import functools
import math

import jax
import jax.numpy as jnp
import numpy as np
from jax import lax
from jax.experimental import pallas as pl
from jax.experimental.pallas import tpu as pltpu

F32 = jnp.float32
BF16 = jnp.bfloat16

GRID_W = 64
ROPE_THETA = 10000.0
NORM_EPS = 1e-6
N_MOD = 6

MLA_HEADS = 8
MLA_NOPE = 64
MLA_ROPE = 32
MLA_V = 64
MLA_Q_RANK = 256
MLA_KV_RANK = 128
MLA_SCALE = (MLA_NOPE + MLA_ROPE) ** -0.5
MLA_HEAD_PAD = 128

SSM_GROUP = 16
SSM_GROUPS = 32
SSM_STATE = 64
SSM_W = SSM_GROUP * SSM_GROUPS
SSM_DIRS = 2
SSM_CHUNK = 16
SSM_SUPER = 16
SSM_CW = SSM_CHUNK * SSM_GROUP

GQA_HEADS = 8
GQA_KV_HEADS = 2
GQA_HD = 64
GQA_W = GQA_HEADS * GQA_HD
GQA_SCALE = GQA_HD ** -0.5
LOG2E = math.log2(math.e)

N_GROUPS = 4
EXPERTS_PER_GROUP = 4
N_EXPERTS = N_GROUPS * EXPERTS_PER_GROUP
D_EXPERT = 512
PAIRS_PER_GROUP = EXPERTS_PER_GROUP * (EXPERTS_PER_GROUP - 1) // 2
GATHER_TILE = 256
GATHER_DEPTH = 3
MOE_TILE = GATHER_TILE

ROW_TILE = 256
LANES = 128
V_ROWS = 80
VMEM_LIMIT = 56 * 1024 * 1024


def _cparams(*sem):
    return pltpu.CompilerParams(dimension_semantics=sem, vmem_limit_bytes=VMEM_LIMIT)


def _dot(a, b):
    return jnp.dot(a, b, preferred_element_type=F32)


def _dot_nt(a, b):
    return lax.dot_general(a, b, (((1,), (1,)), ((), ())), preferred_element_type=F32)


def _dot_f32(a, b):
    a_hi = a.astype(BF16)
    a_lo = (a - a_hi.astype(F32)).astype(BF16)
    return _dot(a_hi, b) + _dot(a_lo, b)


def _rms(x, g):
    return x * lax.rsqrt(jnp.mean(x * x, axis=-1, keepdims=True) + NORM_EPS) * g


def _sigmoid(x):
    return 1.0 / (1.0 + jnp.exp(-x))


def _mod_kernel(cond_ref, w_ref, b_ref, o_ref):
    a = cond_ref[...]
    s = a * _sigmoid(a)
    w = w_ref[...]
    s_hi = s.astype(BF16)
    s_lo = (s - s_hi.astype(F32)).astype(BF16)
    w_hi = w.astype(BF16)
    w_lo = (w - w_hi.astype(F32)).astype(BF16)
    acc = _dot(s_hi, w_hi) + _dot(s_lo, w_hi) + _dot(s_hi, w_lo)
    o_ref[...] = acc + b_ref[...]


def _modulation(cond8, w_mod, b_mod):
    nl, d, n = w_mod.shape
    tn = 1536
    return pl.pallas_call(
        _mod_kernel,
        grid=(nl, n // tn),
        in_specs=[
            pl.BlockSpec((8, d), lambda l, j: (0, 0)),
            pl.BlockSpec((None, d, tn), lambda l, j: (l, 0, j)),
            pl.BlockSpec((None, 1, tn), lambda l, j: (l, 0, j)),
        ],
        out_specs=pl.BlockSpec((None, 8, tn), lambda l, j: (l, 0, j)),
        out_shape=jax.ShapeDtypeStruct((nl, 8, n), F32),
        compiler_params=_cparams("parallel", "parallel"),
        name="modulation",
    )(cond8, w_mod, b_mod.reshape(nl, 1, n))


def _mod_row(mod_ref, idx):
    m = mod_ref[...]
    d = m.shape[1] // N_MOD
    lat = m[0:1, idx * d:(idx + 1) * d]
    ctx = m[1:2, idx * d:(idx + 1) * d]
    return jnp.where(pl.program_id(0) == 0, ctx, lat)


_SEG = {}
_off = 0
for _name, _w in (("cq", 256), ("ckv", 128), ("kr", 128), ("krs", 128), ("ssm", 512), ("gq", 512),
                  ("gqs", 512), ("gk", 128), ("gks", 128), ("gv", 128), ("gate", 3072)):
    _SEG[_name] = (_off, _off + _w)
    _off += _w
W_CAT = _off


def _premix_kernel(x_ref, mod_ref, gpre_ref, w_ref, wuq_ref, wuqs_ref, wuk_ref, wuv_ref,
                   gcq_ref, gckv_ref, gq_ref, gqs_ref, gk_ref, gks_ref, ones_ref,
                   cm_ref, sm_ref, cg_ref, sg_ref,
                   qm_ref, km_ref, vm_ref, qg_ref, kg_ref, vg_ref, u_ref, gate_ref):
    x = x_ref[...]
    h = _rms(x, gpre_ref[...]) * (1.0 + _mod_row(mod_ref, 1)) + _mod_row(mod_ref, 0)
    hb = h.astype(BF16)

    def proj(name):
        a, b = _SEG[name]
        return _dot(hb, w_ref[:, a:b])

    cm = cm_ref[...]
    sm = sm_ref[...]
    cg = cg_ref[...]
    sg = sg_ref[...]

    cqn = _rms(proj("cq"), gcq_ref[...]).astype(BF16)
    ckvn = _rms(proj("ckv"), gckv_ref[...]).astype(BF16)
    q = _dot(cqn, wuq_ref[...])
    qs = _dot(cqn, wuqs_ref[...])
    kn = _dot(ckvn, wuk_ref[...])
    kr = proj("kr") * cm + proj("krs") * sm
    tr = x.shape[0]
    vrow = lax.broadcasted_iota(jnp.int32, (V_ROWS - MLA_V, tr), 0)
    v_tail = jnp.where(vrow == 0, 1.0, 0.0).astype(BF16)
    for hd in range(MLA_HEADS):
        sl = slice(hd * MLA_HEAD_PAD, (hd + 1) * MLA_HEAD_PAD)
        qh = (q[:, sl] * cm + qs[:, sl] * sm) * (MLA_SCALE * LOG2E)
        qm_ref[hd] = qh.T.astype(BF16)
        km_ref[hd] = (kn[:, sl] + kr).astype(BF16)
    vmt = _dot(ckvn, wuv_ref[...]).T
    for hd in range(MLA_HEADS):
        vm_ref[hd, 0:MLA_V, :] = vmt[hd * MLA_V:(hd + 1) * MLA_V, :].astype(BF16)
        vm_ref[hd, MLA_V:V_ROWS, :] = v_tail

    ones = ones_ref[...]

    def head_rms_scale(v, width):
        ms = _dot_f32(v * v, ones[:width, :width]) * (1.0 / GQA_HD)
        return lax.rsqrt(ms + NORM_EPS)

    gq = proj("gq")
    gqs = proj("gqs")
    rq = head_rms_scale(gq, GQA_W)
    cg4 = jnp.concatenate([cg] * (GQA_W // LANES), axis=1)
    sg4 = jnp.concatenate([sg] * (GQA_W // LANES), axis=1)
    qg = rq * (gq * gq_ref[...] * cg4 + gqs * gqs_ref[...] * sg4)
    qgt = (qg * (GQA_SCALE * LOG2E)).T.astype(BF16)
    zeros_q = jnp.zeros((GQA_HD, tr), BF16)
    for hd in range(GQA_HEADS):
        kvh = hd // (GQA_HEADS // GQA_KV_HEADS)
        qh = qgt[hd * GQA_HD:(hd + 1) * GQA_HD, :]
        qg_ref[hd] = jnp.concatenate([qh, zeros_q] if kvh == 0 else [zeros_q, qh], axis=0)
    gk = proj("gk")
    gks = proj("gks")
    rk = head_rms_scale(gk, GQA_KV_HEADS * GQA_HD)
    kg_ref[0] = (rk * (gk * gk_ref[...] * cg + gks * gks_ref[...] * sg)).astype(BF16)
    vgt = proj("gv").T
    for kvh in range(GQA_KV_HEADS):
        vg_ref[kvh, 0:GQA_HD, :] = vgt[kvh * GQA_HD:(kvh + 1) * GQA_HD, :].astype(BF16)
        vg_ref[kvh, GQA_HD:V_ROWS, :] = v_tail

    u_ref[...] = proj("ssm").astype(BF16)
    gate_ref[...] = _sigmoid(proj("gate")).astype(BF16)


def _premix(rows, mod_l, g_pre, wts, tabs):
    r, d = rows.shape
    nt = r // ROW_TILE
    row_spec = lambda w: pl.BlockSpec((ROW_TILE, w), lambda i: (i, 0))
    full = lambda a: pl.BlockSpec(a.shape, lambda i: (0,) * a.ndim)
    consts = [g_pre, wts["w_cat"], wts["wuq"], wts["wuqs"], wts["wuk"], wts["wuv"],
              wts["g_cq"], wts["g_ckv"], wts["g_q"], wts["g_qs"], wts["g_k"], wts["g_ks"], wts["ones"]]
    t_spec = lambda nh, f: pl.BlockSpec((nh, f, ROW_TILE), lambda i: (0, 0, i))
    k_spec = lambda nh, f: pl.BlockSpec((nh, ROW_TILE, f), lambda i: (0, i, 0))
    outs = [(t_spec(MLA_HEADS, MLA_HEAD_PAD), (MLA_HEADS, MLA_HEAD_PAD, r)),
            (k_spec(MLA_HEADS, MLA_HEAD_PAD), (MLA_HEADS, r, MLA_HEAD_PAD)),
            (t_spec(MLA_HEADS, V_ROWS), (MLA_HEADS, V_ROWS, r)),
            (t_spec(GQA_HEADS, GQA_KV_HEADS * GQA_HD), (GQA_HEADS, GQA_KV_HEADS * GQA_HD, r)),
            (k_spec(1, GQA_KV_HEADS * GQA_HD), (1, r, GQA_KV_HEADS * GQA_HD)),
            (t_spec(GQA_KV_HEADS, V_ROWS), (GQA_KV_HEADS, V_ROWS, r)),
            (row_spec(SSM_W), (r, SSM_W)),
            (row_spec(3 * d), (r, 3 * d))]
    return pl.pallas_call(
        _premix_kernel,
        grid=(nt,),
        in_specs=[row_spec(d), full(mod_l)] + [full(a) for a in consts] + [row_spec(LANES)] * 4,
        out_specs=[s for s, _ in outs],
        out_shape=[jax.ShapeDtypeStruct(shp, BF16) for _, shp in outs],
        compiler_params=_cparams("parallel"),
        name="premix",
    )(rows, mod_l, *consts, tabs["cm"], tabs["sm"], tabs["cg"], tabs["sg"])


def _attn_kernel(q_ref, k_ref, v_ref, o_ref, s_ref, *, tk, nsub, n_ctx, n_iter, dv, hb, k_shared, v_shared,
                 ahead):
    n_items = hb * nsub
    tq = q_ref.shape[2]

    def scores(h, off, size):
        return _dot(k_ref[0 if k_shared else h, pl.ds(off, size), :], q_ref[h])

    def update(h, m, acc, s, off, size):
        m_new = jnp.maximum(m, jnp.max(s, axis=0, keepdims=True))
        p = jnp.exp2(s - m_new).astype(BF16)
        pv = _dot(v_ref[0 if v_shared else h, :, pl.ds(off, size)], p)
        return m_new, jnp.exp2(m - m_new) * acc + pv

    def issue(item, base):
        h, sub = item % hb, item // hb
        s_ref[item] = scores(h, pl.multiple_of(base + sub * tk, LANES), tk)

    m0 = jnp.full((1, tq), -1e30, F32)
    acc0 = jnp.zeros((V_ROWS, tq), F32)
    s_ctx = [scores(h, 0, n_ctx) for h in range(hb)]
    for item in range(ahead):
        issue(item, n_ctx)
    carry = tuple(update(h, m0, acc0, s_ctx[h], 0, n_ctx) for h in range(hb))

    def body(j, carry):
        carry = list(carry)
        base = n_ctx + j * (nsub * tk)
        base_next = n_ctx + jnp.minimum(j + 1, n_iter - 1) * (nsub * tk)
        for item in range(n_items):
            nxt = item + ahead
            if nxt < n_items:
                issue(nxt, base)
            else:
                issue(nxt - n_items, base_next)
            h, sub = item % hb, item // hb
            off = pl.multiple_of(base + sub * tk, LANES)
            carry[h] = update(h, carry[h][0], carry[h][1], s_ref[item], off, tk)
        return tuple(carry)

    carry = lax.fori_loop(0, jnp.where(pl.program_id(1) == 0, 0, n_iter), body, carry)
    for h in range(hb):
        acc = carry[h][1]
        o_ref[h] = acc[:dv] / acc[dv:dv + 1]


def _attention(qt, k, vt, *, hb, nsub, tq=256, tk=512, ahead=2):
    nh, dq, r = qt.shape
    k_shared = k.shape[0] != nh
    v_shared = vt.shape[0] != nh
    assert nh % hb == 0 and k.shape[0] in (nh, 1) and vt.shape[0] in (nh, nh // hb)
    dv = V_ROWS - 16
    assert (r - ROW_TILE) % (tk * nsub) == 0 and ahead < hb * nsub
    kern = functools.partial(_attn_kernel, tk=tk, nsub=nsub, n_ctx=ROW_TILE, n_iter=(r - ROW_TILE) // (tk * nsub),
                             dv=dv, hb=hb, k_shared=k_shared, v_shared=v_shared, ahead=ahead)
    k_spec = (pl.BlockSpec((1, r, dq), lambda g, i: (0, 0, 0)) if k_shared
              else pl.BlockSpec((hb, r, dq), lambda g, i: (g, 0, 0)))
    v_spec = pl.BlockSpec((1 if v_shared else hb, V_ROWS, r), lambda g, i: (g, 0, 0))
    return pl.pallas_call(
        kern,
        grid=(nh // hb, r // tq),
        in_specs=[pl.BlockSpec((hb, dq, tq), lambda g, i: (g, 0, i)), k_spec, v_spec],
        out_specs=pl.BlockSpec((hb, dv, tq), lambda g, i: (g, 0, i)),
        out_shape=jax.ShapeDtypeStruct((nh, dv, r), F32),
        scratch_shapes=[pltpu.VMEM((hb * nsub, tk, tq), F32)],
        compiler_params=_cparams("parallel", "arbitrary"),
        name="attention",
    )(qt, k, vt)


def _s5_kernel(u_ref, blk_ref, bp_ref, cpt_ref, ar_ref, ai_ref, y_ref, xr_ref, xi_ref, *, n_ctx_chunks):
    t, j, _ = u_ref.shape
    npair = t // 2
    ns = xr_ref.shape[1]
    row = lax.broadcasted_iota(jnp.int32, (j, ns), 0)

    def run(reverse):
        ucat = jnp.concatenate([u_ref[k] for k in range(t)], axis=1)
        xc = _dot(ucat, bp_ref[...])
        if reverse:
            shift = j - n_ctx_chunks - 1
            keep = row < j - 1
        else:
            shift = 1
            keep = row >= 1
        xr_ref[...] = jnp.where(keep, pltpu.roll(xc[:, :ns], shift, 0), 0.0)
        xi_ref[...] = jnp.where(keep, pltpu.roll(xc[:, ns:], shift, 0), 0.0)
        step = 1
        k = 0
        while step < j:
            ar = ar_ref[k:k + 1, :]
            ai = ai_ref[k:k + 1, :]
            xr = xr_ref[...]
            xi = xi_ref[...]
            if reverse:
                keep = row < j - step
                sr = jnp.where(keep, pltpu.roll(xr, j - step, 0), 0.0)
                si = jnp.where(keep, pltpu.roll(xi, j - step, 0), 0.0)
            else:
                keep = row >= step
                sr = jnp.where(keep, pltpu.roll(xr, step, 0), 0.0)
                si = jnp.where(keep, pltpu.roll(xi, step, 0), 0.0)
            xr_ref[...] = xr + ar * sr - ai * si
            xi_ref[...] = xi + ar * si + ai * sr
            step *= 2
            k += 1
        xin = jnp.concatenate([xr_ref[...], xi_ref[...]], axis=1)
        if reverse:
            xin = pltpu.roll(xin, n_ctx_chunks, 0)
        xin = xin.astype(BF16)
        for po in range(npair):
            y = _dot_nt(xin, cpt_ref[po * 256:(po + 1) * 256, :])
            pins = range(po, npair) if reverse else range(0, po + 1)
            for pi_ in pins:
                upair = jnp.concatenate([u_ref[2 * pi_], u_ref[2 * pi_ + 1]], axis=1)
                y += _dot(upair, blk_ref[abs(po - pi_)])
            if reverse:
                y_ref[:, 2 * po, :] += y[:, :128]
                y_ref[:, 2 * po + 1, :] += y[:, 128:]
            else:
                y_ref[:, 2 * po, :] = y[:, :128]
                y_ref[:, 2 * po + 1, :] = y[:, 128:]

    @pl.when(pl.program_id(1) == 0)
    def _():
        run(False)

    @pl.when(pl.program_id(1) == 1)
    def _():
        run(True)


def _s5_scan(ut, mats, n_ctx_chunks):
    t, j, w = ut.shape
    no = w // LANES
    ns = (LANES // SSM_GROUP) * SSM_STATE
    wspec = lambda *tail: pl.BlockSpec((None, None) + tail, lambda o, d: (d, o) + (0,) * len(tail))
    kern = functools.partial(_s5_kernel, n_ctx_chunks=n_ctx_chunks)
    return pl.pallas_call(
        kern,
        grid=(no, SSM_DIRS),
        in_specs=[pl.BlockSpec((t, j, LANES), lambda o, d: (0, 0, o)),
                  wspec(t // 2, 256, 256), wspec(t * LANES, 2 * ns), wspec(t * LANES, 2 * ns),
                  wspec(16, ns), wspec(16, ns)],
        out_specs=pl.BlockSpec((j, t, LANES), lambda o, d: (0, 0, o)),
        out_shape=jax.ShapeDtypeStruct((j, t, w), F32),
        scratch_shapes=[pltpu.VMEM((j, ns), F32)] * 2,
        compiler_params=_cparams("parallel", "arbitrary"),
        name="s5_scan",
    )(ut, mats["blk"], mats["bp"], mats["cpt"], mats["ar"], mats["ai"])


def _s5_matrices(a_re, a_im, log_dt, b_re, b_im, c_re, c_im, d_skip):
    t = SSM_CHUNK
    a_re = a_re.astype(F32)
    a_im = a_im.astype(F32)
    dt = jnp.exp(log_dt.astype(F32))[..., None]
    den = a_re * a_re + a_im * a_im

    def lpow(k):
        kf = jnp.asarray(k, F32)
        mag = jnp.exp(a_re * dt * kf)
        return mag * jnp.cos(a_im * dt * kf), mag * jnp.sin(a_im * dt * kf)

    lr, li = lpow(1.0)
    cr = ((lr - 1.0) * a_re + li * a_im) / den
    ci = (li * a_re - (lr - 1.0) * a_im) / den
    bbr = cr[..., None] * b_re - ci[..., None] * b_im
    bbi = cr[..., None] * b_im + ci[..., None] * b_re
    c_re = c_re.astype(F32)
    c_im = c_im.astype(F32)

    ks = jnp.arange(t + 1, dtype=F32)[:, None, None, None]
    pr, pi = lpow(ks)
    lbr = pr[:t, ..., None] * bbr - pi[:t, ..., None] * bbi
    lbi = pr[:t, ..., None] * bbi + pi[:t, ..., None] * bbr
    kern = (jnp.einsum('dgop,kdgpc->dgkoc', c_re, lbr, precision='highest')
            - jnp.einsum('dgop,kdgpc->dgkoc', c_im, lbi, precision='highest'))
    skip = d_skip.astype(F32).reshape(SSM_GROUPS, SSM_GROUP)
    eye = jnp.eye(SSM_GROUP, dtype=F32)
    kern = kern.at[0, :, 0].add(skip[:, :, None] * eye)
    gl = LANES // SSM_GROUP
    no = SSM_GROUPS // gl
    def widen(compact, col_of, n_cols, row_group, col_group):
        expand = np.zeros((compact.shape[-1], n_cols), np.float32)
        for h in range(gl):
            src_cols = np.arange(compact.shape[-1])
            expand[src_cols, col_of(src_cols, h)] = 1.0
        wide = jnp.einsum('...rj,jk->...rk', compact.astype(BF16), jnp.asarray(expand, BF16),
                          preferred_element_type=BF16)
        rows_g = row_group(lax.broadcasted_iota(jnp.int32, wide.shape[-2:], 0))
        cols_g = col_group(lax.broadcasted_iota(jnp.int32, wide.shape[-2:], 1))
        return jnp.where(rows_g == cols_g, wide, jnp.zeros((), BF16))

    dd = jnp.arange(t // 2)[:, None, None]
    ti = jnp.arange(2)[None, :, None]
    to = jnp.arange(2)[None, None, :]
    lag = jnp.stack([2 * dd + to - ti, 2 * dd + ti - to])
    pick = jax.vmap(lambda kd, ld: jnp.take(kd, jnp.clip(ld, 0, t - 1).reshape(-1), axis=1))(kern, lag)
    pick = pick.reshape(SSM_DIRS, no, gl, t // 2, 2, 2, SSM_GROUP, SSM_GROUP)
    pick = jnp.where((lag >= 0)[:, None, None, :, :, :, None, None], pick, 0.0)
    kc = jnp.transpose(pick, (0, 1, 3, 4, 2, 7, 5, 6)).reshape(SSM_DIRS, no, t // 2, 256, 2 * SSM_GROUP)
    blk = widen(kc, lambda j, h: (j // SSM_GROUP) * LANES + h * SSM_GROUP + j % SSM_GROUP, 256,
                lambda r: (r // SSM_GROUP) % gl, lambda c: (c // SSM_GROUP) % gl)
    def contrib(lb):
        both = jnp.stack([lb[::-1, 0], lb[:, 1]])
        both = both.reshape(SSM_DIRS, t, no, gl, SSM_STATE, SSM_GROUP)
        return jnp.transpose(both, (0, 2, 1, 3, 5, 4)).reshape(SSM_DIRS, no, t * LANES, SSM_STATE)
    bp = widen(jnp.concatenate([contrib(lbr), contrib(lbi)], axis=-1),
               lambda j, h: (j // SSM_STATE) * (gl * SSM_STATE) + h * SSM_STATE + j % SSM_STATE, 2 * gl * SSM_STATE,
               lambda r: (r // SSM_GROUP) % gl, lambda c: (c // SSM_STATE) % gl)
    p1r = pr[1:, :, :, None, :]
    p1i = pi[1:, :, :, None, :]
    rd_r = c_re * p1r - c_im * p1i
    rd_i = -(c_re * p1i + c_im * p1r)
    def readout(rd):
        both = jnp.stack([rd[:, 0], rd[::-1, 1]])
        both = both.reshape(SSM_DIRS, t, no, gl, SSM_GROUP, SSM_STATE)
        return jnp.transpose(both, (0, 2, 1, 3, 4, 5)).reshape(SSM_DIRS, no, t * LANES, SSM_STATE)
    cpt = widen(jnp.concatenate([readout(rd_r), readout(rd_i)], axis=-1),
                lambda j, h: (j // SSM_STATE) * (gl * SSM_STATE) + h * SSM_STATE + j % SSM_STATE, 2 * gl * SSM_STATE,
                lambda r: (r // SSM_GROUP) % gl, lambda c: (c // SSM_STATE) % gl)
    kk = (t * 2.0 ** jnp.minimum(jnp.arange(16), 10)).astype(F32)[:, None, None, None]
    qr, qi = lpow(kk)
    lanes = lambda q: jnp.transpose(q, (1, 2, 0, 3)).reshape(SSM_DIRS, no, gl, 16, SSM_STATE)
    ar = jnp.transpose(lanes(qr), (0, 1, 3, 2, 4)).reshape(SSM_DIRS, no, 16, gl * SSM_STATE)
    ai = jnp.transpose(lanes(qi), (0, 1, 3, 2, 4)).reshape(SSM_DIRS, no, 16, gl * SSM_STATE)
    return dict(blk=blk, bp=bp, cpt=cpt, ar=ar, ai=ai)


def _merge_kernel(x_ref, mod_ref, om_ref, og_ref, y_ref, gate_ref,
                  wglu_ref, bglu_ref, wbm_ref, wbs_ref, wbg_ref, wout_ref,
                  gpost_ref, gffn_ref, wrt_hi_ref, wrt_lo_ref, brt_ref,
                  xo_ref, h_ref, bk_ref):
    d = x_ref.shape[1]
    y = y_ref[...]
    y1 = 0.5 * y * (1.0 + jnp.tanh(math.sqrt(2.0 / math.pi) * (y + 0.044715 * (y * y * y))))
    ssm = y1 * _sigmoid(_dot(y1.astype(BF16), wglu_ref[...]) + bglu_ref[...])
    gate = gate_ref[...]
    o_mla = om_ref[...].reshape(-1, om_ref.shape[2]).T.astype(BF16)
    o_gqa = og_ref[...].reshape(-1, og_ref.shape[2]).T.astype(BF16)
    m = (gate[:, 0:d].astype(F32) * _dot(o_mla, wbm_ref[...])
         + gate[:, d:2 * d].astype(F32) * _dot(ssm.astype(BF16), wbs_ref[...])
         + gate[:, 2 * d:3 * d].astype(F32) * _dot(o_gqa, wbg_ref[...]))
    ymix = _dot(m.astype(BF16), wout_ref[...])
    x = x_ref[...] + _mod_row(mod_ref, 2) * _rms(ymix, gpost_ref[...])
    xo_ref[...] = x
    h = _rms(x, gffn_ref[...]) * (1.0 + _mod_row(mod_ref, 4)) + _mod_row(mod_ref, 3)
    h_ref[...] = h

    h_hi = h.astype(BF16)
    h_lo = (h - h_hi.astype(F32)).astype(BF16)
    lg = (_dot(h_hi, wrt_hi_ref[...]) + _dot(h_lo, wrt_hi_ref[...]) + _dot(h_hi, wrt_lo_ref[...])
          + brt_ref[...])
    lane = lax.broadcasted_iota(jnp.int32, lg.shape, 1)
    neg = jnp.float32(-1e30)
    is_g = lane < N_GROUPS
    gl = jnp.where(is_g, lg, neg)
    gmax = jnp.max(gl, axis=-1, keepdims=True)
    gsel = jnp.min(jnp.where(is_g & (gl == gmax), lane, LANES), axis=-1, keepdims=True)
    pg = 1.0 / jnp.sum(jnp.where(is_g, jnp.exp(gl - gmax), 0.0), axis=-1, keepdims=True)
    lo = N_GROUPS + gsel * EXPERTS_PER_GROUP
    in_grp = (lane >= lo) & (lane < lo + EXPERTS_PER_GROUP)
    el = jnp.where(in_grp, lg, neg)
    m1 = jnp.max(el, axis=-1, keepdims=True)
    i1 = jnp.min(jnp.where(in_grp & (el == m1), lane, LANES), axis=-1, keepdims=True)
    el2 = jnp.where(lane == i1, neg, el)
    m2 = jnp.max(el2, axis=-1, keepdims=True)
    i2 = jnp.min(jnp.where(in_grp & (el2 == m2), lane, LANES), axis=-1, keepdims=True)
    e2 = jnp.exp(m2 - m1)
    t1 = pg / (1.0 + e2)
    first = i1 < i2
    ea = jnp.minimum(i1, i2) - lo
    eb = jnp.maximum(i1, i2) - lo
    bucket = gsel * PAIRS_PER_GROUP + jnp.right_shift(ea * (7 - ea), 1) + (eb - ea - 1)
    w_a = jnp.where(first, t1, t1 * e2)
    w_b = jnp.where(first, t1 * e2, t1)
    route = (jnp.where(lane == 0, bucket.astype(F32), 0.0) + jnp.where(lane == 1, w_a, 0.0)
             + jnp.where(lane == 2, w_b, 0.0))
    bk_ref[...] = route


def _merge(rows, mod_l, o_mla, o_gqa, y_ssm, gates, wts):
    r, d = rows.shape
    nt = r // ROW_TILE
    row_spec = lambda w: pl.BlockSpec((ROW_TILE, w), lambda i: (i, 0))
    full = lambda a: pl.BlockSpec(a.shape, lambda i: (0,) * a.ndim)
    consts = [wts["w_glu"], wts["b_glu"], wts["w_br_mla"], wts["w_br_ssm"], wts["w_br_gqa"], wts["w_out"],
              wts["g_post_mix"], wts["g_pre_ffn"], wts["w_rt_hi"], wts["w_rt_lo"], wts["b_rt"]]
    return pl.pallas_call(
        _merge_kernel,
        grid=(nt,),
        in_specs=[row_spec(d), full(mod_l),
                  pl.BlockSpec(o_mla.shape[:2] + (ROW_TILE,), lambda i: (0, 0, i)),
                  pl.BlockSpec(o_gqa.shape[:2] + (ROW_TILE,), lambda i: (0, 0, i)),
                  row_spec(SSM_W), row_spec(3 * d)] + [full(a) for a in consts],
        out_specs=[row_spec(d), row_spec(d), row_spec(LANES)],
        out_shape=[jax.ShapeDtypeStruct((r, d), F32), jax.ShapeDtypeStruct((r, d), F32),
                   jax.ShapeDtypeStruct((r, LANES), F32)],
        compiler_params=_cparams("parallel"),
        name="merge",
    )(rows, mod_l, o_mla, o_gqa, y_ssm, gates, *consts)


def _row_copy(idx, r, src_hbm, buf, sem, slot):
    return pltpu.make_async_copy(src_hbm.at[pl.ds(idx, 1)], buf.at[slot, pl.ds(r, 1)], sem.at[slot])


def _gathered_tile(idx_ref, src_hbm, buf, sem):
    i = pl.program_id(0)
    last = pl.num_programs(0) - 1
    depth, n = buf.shape[0], buf.shape[1]
    ahead = depth - 1

    def start(step, dst_slot):
        for r in range(n):
            _row_copy(idx_ref[jnp.minimum(step, last) * n + r], r, src_hbm, buf, sem, dst_slot).start()

    def wait(dst_slot):
        for r in range(n):
            _row_copy(0, r, src_hbm, buf, sem, dst_slot).wait()

    @pl.when(i == 0)
    def _():
        for k in range(ahead):
            start(k, k)

    slot = i % depth
    wait(slot)
    tile = buf[slot]

    def finish():
        start(i + ahead, (i + ahead) % depth)

        @pl.when(i == last)
        def _():
            for k in range(1, depth):
                wait((i + k) % depth)

    return tile, finish


def _moe_kernel(src_ref, ea_ref, eb_ref, tok_hbm, rt_ref, wga_ref, wua_ref, wda_ref, wgb_ref, wub_ref, wdb_ref,
                gpost_ref, o_ref, buf, sem):
    tok, finish = _gathered_tile(src_ref, tok_hbm, buf, sem)
    h = tok.astype(BF16)
    rt = rt_ref[...]
    lane = lax.broadcasted_iota(jnp.int32, rt.shape, 1)
    w_a = jnp.sum(jnp.where(lane == 1, rt, 0.0), axis=-1, keepdims=True)
    w_b = jnp.sum(jnp.where(lane == 2, rt, 0.0), axis=-1, keepdims=True)

    def expert(wg_ref, wu_ref, wd_ref):
        a = _dot(h, wg_ref[...])
        hid = (a * _sigmoid(a)) * _dot(h, wu_ref[...])
        return _dot(hid.astype(BF16), wd_ref[...])

    y = w_a * expert(wga_ref, wua_ref, wda_ref) + w_b * expert(wgb_ref, wub_ref, wdb_ref)
    o_ref[...] = _rms(y, gpost_ref[...]).astype(o_ref.dtype)
    finish()


def _moe(tokens, route_sorted, src, tile_ea, tile_eb, wg, wu, wd, g_post):
    w = d = tokens.shape[1]
    ne, _, de = wg.shape
    nt = src.shape[0] // GATHER_TILE
    w_in = lambda sel: pl.BlockSpec((None, d, de), lambda i, src, ea, eb: ((ea, eb)[sel][i], 0, 0))
    w_out = lambda sel: pl.BlockSpec((None, de, d), lambda i, src, ea, eb: ((ea, eb)[sel][i], 0, 0))
    return pl.pallas_call(
        _moe_kernel,
        grid_spec=pltpu.PrefetchScalarGridSpec(
            num_scalar_prefetch=3,
            grid=(nt,),
            in_specs=[
                pl.BlockSpec(memory_space=pl.ANY),
                pl.BlockSpec((GATHER_TILE, LANES), lambda i, src, ea, eb: (i, 0)),
                w_in(0), w_in(0), w_out(0), w_in(1), w_in(1), w_out(1),
                pl.BlockSpec(g_post.shape, lambda i, src, ea, eb: (0, 0)),
            ],
            out_specs=pl.BlockSpec((GATHER_TILE, d), lambda i, src, ea, eb: (i, 0)),
            scratch_shapes=[pltpu.VMEM((GATHER_DEPTH, GATHER_TILE, w), F32),
                            pltpu.SemaphoreType.DMA((GATHER_DEPTH,))],
        ),
        out_shape=jax.ShapeDtypeStruct((src.shape[0], d), F32),
        compiler_params=_cparams("arbitrary"),
        name="moe",
    )(src, tile_ea, tile_eb, tokens, route_sorted, wg, wu, wd, wg, wu, wd, g_post)


def _dispatch(bucket, n_rows):
    nb = N_GROUPS * PAIRS_PER_GROUP
    n_tiles = n_rows // MOE_TILE + nb
    onehot = (bucket[:, None] == jnp.arange(nb, dtype=jnp.int32)[None, :]).astype(jnp.int32)
    counts = jnp.sum(onehot, axis=0)
    rank = jnp.sum((jnp.cumsum(onehot, axis=0) - onehot) * onehot, axis=1)
    tiles_per = (counts + MOE_TILE - 1) // MOE_TILE
    tile_end = jnp.cumsum(tiles_per)
    start = (tile_end - tiles_per) * MOE_TILE
    pos = jnp.take(start, bucket) + rank
    src = jnp.zeros((n_tiles * MOE_TILE,), jnp.int32).at[pos].set(jnp.arange(n_rows, dtype=jnp.int32))
    tile_id = jnp.arange(n_tiles, dtype=jnp.int32)
    tile_bucket = jnp.minimum(jnp.searchsorted(tile_end, tile_id, side='right'), nb - 1).astype(jnp.int32)
    tile_used = (tile_id < tile_end[-1]).astype(jnp.int32)
    last_bucket = jnp.take(tile_bucket, jnp.maximum(tile_end[-1] - 1, 0))
    tile_bucket = jnp.where(tile_used == 1, tile_bucket, last_bucket)
    pair_a = jnp.asarray([0, 0, 0, 1, 1, 2], jnp.int32)
    pair_b = jnp.asarray([1, 2, 3, 2, 3, 3], jnp.int32)
    grp = tile_bucket // PAIRS_PER_GROUP
    tile_ea = grp * EXPERTS_PER_GROUP + jnp.take(pair_a, tile_bucket % PAIRS_PER_GROUP)
    tile_eb = grp * EXPERTS_PER_GROUP + jnp.take(pair_b, tile_bucket % PAIRS_PER_GROUP)
    return pos, src, tile_ea, tile_eb


def _residual_kernel(pos_ref, x_ref, z_hbm, mod_ref, o_ref, buf, sem):
    z, finish = _gathered_tile(pos_ref, z_hbm, buf, sem)
    o_ref[...] = x_ref[...] + _mod_row(mod_ref, 5) * z
    finish()


def _residual(rows, z_sorted, pos, mod_l):
    r, d = rows.shape
    row_spec = pl.BlockSpec((GATHER_TILE, d), lambda i, pos: (i, 0))
    return pl.pallas_call(
        _residual_kernel,
        grid_spec=pltpu.PrefetchScalarGridSpec(
            num_scalar_prefetch=1,
            grid=(r // GATHER_TILE,),
            in_specs=[row_spec, pl.BlockSpec(memory_space=pl.ANY),
                      pl.BlockSpec(mod_l.shape, lambda i, pos: (0, 0))],
            out_specs=row_spec,
            scratch_shapes=[pltpu.VMEM((GATHER_DEPTH, GATHER_TILE, d), F32),
                            pltpu.SemaphoreType.DMA((GATHER_DEPTH,))],
        ),
        out_shape=jax.ShapeDtypeStruct((r, d), F32),
        compiler_params=_cparams("arbitrary"),
        name="residual",
    )(pos, rows, z_sorted, mod_l)


def _rope_tables(seq, n_ctx):
    n_rows = seq // GRID_W

    def pattern(width):
        half = width // 4
        freqs = (np.float32(ROPE_THETA) ** (-np.arange(half, dtype=np.float32) / np.float32(half))).astype(np.float32)
        ar = np.arange(n_rows, dtype=np.float32)[:, None] * freqs[None, :]
        ac = np.arange(GRID_W, dtype=np.float32)[:, None] * freqs[None, :]
        by_row = lambda tab: jnp.broadcast_to(jnp.asarray(tab, F32)[:, None, :], (n_rows, GRID_W, half)).reshape(seq, half)
        by_col = lambda tab: jnp.broadcast_to(jnp.asarray(tab, F32)[None, :, :], (n_rows, GRID_W, half)).reshape(seq, half)
        cr, sr, cc, sc = by_row(np.cos(ar)), by_row(np.sin(ar)), by_col(np.cos(ac)), by_col(np.sin(ac))
        c = jnp.concatenate([cr, cr, cc, cc], axis=1)
        s = jnp.concatenate([-sr, sr, -sc, sc], axis=1)
        return c, s

    c32, s32 = pattern(MLA_ROPE)
    c64, s64 = pattern(GQA_HD)
    one = jnp.ones((seq, 1), F32)
    cm = jnp.concatenate([one * jnp.ones((1, MLA_NOPE)), c32, one * jnp.ones((1, 32))], axis=1)
    sm = jnp.concatenate([jnp.zeros((seq, MLA_NOPE)), s32, jnp.zeros((seq, 32))], axis=1)
    cg = jnp.concatenate([c64, c64], axis=1)
    sg = jnp.concatenate([s64, s64], axis=1)
    ctx_c = jnp.ones((n_ctx, LANES), F32)
    ctx_s = jnp.zeros((n_ctx, LANES), F32)
    cat = lambda a, b: jnp.concatenate([a, b], axis=0)
    return dict(cm=cat(ctx_c, cm), sm=cat(ctx_s, sm), cg=cat(ctx_c, cg), sg=cat(ctx_s, sg))


def _swap_perm(width):
    q = width // 4
    return np.concatenate([np.arange(q, 2 * q), np.arange(0, q), np.arange(3 * q, 4 * q), np.arange(2 * q, 3 * q)])


def _layer_weights(l, p):
    d = p["w_in"].shape[1]
    w_in = p["w_in"][l]
    o = 0
    cq = w_in[:, o:o + MLA_Q_RANK]; o += MLA_Q_RANK
    ckv = w_in[:, o:o + MLA_KV_RANK]; o += MLA_KV_RANK
    kr = w_in[:, o:o + MLA_ROPE]; o += MLA_ROPE
    ssm = w_in[:, o:o + SSM_W]; o += SSM_W
    gq = w_in[:, o:o + GQA_W]; o += GQA_W
    gk = w_in[:, o:o + GQA_KV_HEADS * GQA_HD]; o += GQA_KV_HEADS * GQA_HD
    gv = w_in[:, o:o + GQA_KV_HEADS * GQA_HD]; o += GQA_KV_HEADS * GQA_HD
    gate = w_in[:, o:]
    p32 = _swap_perm(MLA_ROPE)
    p64 = _swap_perm(GQA_HD)
    perm_q = np.concatenate([p64 + GQA_HD * h for h in range(GQA_HEADS)])
    perm_k = np.concatenate([p64 + GQA_HD * h for h in range(GQA_KV_HEADS)])
    z = lambda n: jnp.zeros((d, n), F32)
    kr128 = jnp.concatenate([z(MLA_NOPE), kr, z(32)], axis=1)
    krs128 = jnp.concatenate([z(MLA_NOPE), kr[:, p32], z(32)], axis=1)
    w_cat = jnp.concatenate([cq, ckv, kr128, krs128, ssm, gq, gq[:, perm_q], gk, gk[:, perm_k], gv, gate],
                            axis=1).astype(BF16)

    w_uq = p["w_uq"][l].reshape(MLA_Q_RANK, MLA_HEADS, MLA_NOPE + MLA_ROPE)
    zq = jnp.zeros((MLA_Q_RANK, MLA_HEADS, 32), F32)
    wuq = jnp.concatenate([w_uq, zq], axis=2).reshape(MLA_Q_RANK, -1)
    wuqs = jnp.concatenate([jnp.zeros((MLA_Q_RANK, MLA_HEADS, MLA_NOPE), F32),
                            w_uq[:, :, MLA_NOPE:][:, :, p32], zq], axis=2).reshape(MLA_Q_RANK, -1)
    w_ukv = p["w_ukv"][l].reshape(MLA_KV_RANK, MLA_HEADS, MLA_NOPE + MLA_V)
    wuk = jnp.concatenate([w_ukv[:, :, :MLA_NOPE], jnp.zeros((MLA_KV_RANK, MLA_HEADS, 64), F32)],
                          axis=2).reshape(MLA_KV_RANK, -1)
    wuv = w_ukv[:, :, MLA_NOPE:].reshape(MLA_KV_RANK, -1)

    g_qn = p["g_qn"][l]
    g_kn = p["g_kn"][l]
    blk = np.arange(GQA_W) // GQA_HD
    ones = jnp.asarray((blk[:, None] == blk[None, :]).astype(np.float32), BF16)

    w_rt = jnp.zeros((d, LANES), F32)
    w_rt = w_rt.at[:, :N_GROUPS].set(p["w_group"][l]).at[:, N_GROUPS:N_GROUPS + N_EXPERTS].set(p["w_router"][l])
    w_rt_hi = w_rt.astype(BF16)
    w_rt_lo = (w_rt - w_rt_hi.astype(F32)).astype(BF16)
    b_rt = jnp.zeros((1, LANES), F32)
    b_rt = b_rt.at[0, :N_GROUPS].set(p["b_group"][l]).at[0, N_GROUPS:N_GROUPS + N_EXPERTS].set(p["b_router"][l])

    row = lambda v: v.reshape(1, -1).astype(F32)
    return dict(
        w_cat=w_cat, wuq=wuq.astype(BF16), wuqs=wuqs.astype(BF16), wuk=wuk.astype(BF16), wuv=wuv.astype(BF16),
        g_cq=row(p["g_cq"][l]), g_ckv=row(p["g_ckv"][l]),
        g_q=row(jnp.tile(g_qn, GQA_HEADS)), g_qs=row(jnp.tile(g_qn[p64], GQA_HEADS)),
        g_k=row(jnp.tile(g_kn, GQA_KV_HEADS)), g_ks=row(jnp.tile(g_kn[p64], GQA_KV_HEADS)),
        ones=ones,
        w_glu=p["w_glu"][l].astype(BF16), b_glu=row(p["b_glu"][l]),
        w_br_mla=p["w_br_mla"][l].astype(BF16), w_br_ssm=p["w_br_ssm"][l].astype(BF16),
        w_br_gqa=p["w_br_gqa"][l].astype(BF16), w_out=p["w_out"][l].astype(BF16),
        g_post_mix=row(p["g_post_mix"][l]), g_pre_ffn=row(p["g_pre_ffn"][l]),
        w_rt_hi=w_rt_hi, w_rt_lo=w_rt_lo, b_rt=b_rt,
    )


def kernel(x, c, ctx, c_ctx, w_mod, b_mod, g_pre_mix, g_post_mix, g_pre_ffn, g_post_ffn, w_in, g_cq, g_ckv, w_uq, w_ukv, g_qn, g_kn, ssm_a_re, ssm_a_im, ssm_log_dt, ssm_b_re, ssm_b_im, ssm_c_re, ssm_c_im, ssm_d, w_glu, b_glu, w_br_mla, w_br_ssm, w_br_gqa, w_out, w_group, b_group, w_router, b_router, w_exp_gate, w_exp_up, w_exp_down):
    assert x.shape[0] == 1 and ctx.shape[0] == 1
    seq, d = x.shape[1], x.shape[2]
    n_ctx = ctx.shape[1]
    assert n_ctx == ROW_TILE and seq % (SSM_CHUNK * SSM_SUPER) == 0
    depth = w_in.shape[0]
    params = dict(w_in=w_in, g_cq=g_cq, g_ckv=g_ckv, w_uq=w_uq, w_ukv=w_ukv, g_qn=g_qn, g_kn=g_kn,
                  w_glu=w_glu, b_glu=b_glu, w_br_mla=w_br_mla, w_br_ssm=w_br_ssm, w_br_gqa=w_br_gqa,
                  w_out=w_out, g_post_mix=g_post_mix, g_pre_ffn=g_pre_ffn,
                  w_group=w_group, b_group=b_group, w_router=w_router, b_router=b_router)

    cond8 = jnp.zeros((8, d), F32).at[0].set(c[0]).at[1].set(c_ctx)
    mods = _modulation(cond8, w_mod, b_mod)
    tabs = _rope_tables(seq, n_ctx)
    rows = jnp.concatenate([ctx[0], x[0]], axis=0)
    r = rows.shape[0]

    for l in range(depth):
        wts = _layer_weights(l, params)
        mod_l = mods[l]
        qm, km, vm, qg, kg, vg, u, gates = _premix(rows, mod_l, g_pre_mix[l].reshape(1, d), wts, tabs)

        o_mla = _attention(qm, km, vm, hb=2, nsub=16, ahead=6)
        o_gqa = _attention(qg, kg, vg, hb=GQA_HEADS // GQA_KV_HEADS, nsub=8, ahead=4)

        mats = _s5_matrices(ssm_a_re[l], ssm_a_im[l], ssm_log_dt[l], ssm_b_re[l], ssm_b_im[l],
                            ssm_c_re[l], ssm_c_im[l], ssm_d[l])
        ut = jnp.transpose(u.reshape(r // SSM_CHUNK, SSM_CHUNK, SSM_W), (1, 0, 2))
        y_ssm = _s5_scan(ut, mats, n_ctx // SSM_CHUNK).reshape(r, SSM_W)

        rows, tokens, route = _merge(rows, mod_l, o_mla, o_gqa, y_ssm, gates, wts)
        pos, src, tile_ea, tile_eb = _dispatch(route[:, 0].astype(jnp.int32), r)
        z_sorted = _moe(tokens, jnp.take(route, src, axis=0), src, tile_ea, tile_eb,
                        w_exp_gate[l].astype(BF16), w_exp_up[l].astype(BF16), w_exp_down[l].astype(BF16),
                        g_post_ffn[l].reshape(1, d))
        rows = _residual(rows, z_sorted, pos, mod_l)

    return rows[n_ctx:].reshape(1, seq, d)
```

```python
import functools
import math

import jax
import jax.numpy as jnp
import numpy as np
from jax import lax
from jax.experimental import pallas as pl
from jax.experimental.pallas import tpu as pltpu

F32 = jnp.float32
BF16 = jnp.bfloat16

GRID_W = 64
ROPE_THETA = 10000.0
NORM_EPS = 1e-6
N_MOD = 6

MLA_HEADS = 8
MLA_NOPE = 64
MLA_ROPE = 32
MLA_V = 64
MLA_Q_RANK = 256
MLA_KV_RANK = 128
MLA_SCALE = (MLA_NOPE + MLA_ROPE) ** -0.5
MLA_HEAD_PAD = 128

SSM_GROUP = 16
SSM_GROUPS = 32
SSM_STATE = 64
SSM_W = SSM_GROUP * SSM_GROUPS
SSM_DIRS = 2
SSM_CHUNK = 16
SSM_SUPER = 16
SSM_CW = SSM_CHUNK * SSM_GROUP

GQA_HEADS = 8
GQA_KV_HEADS = 2
GQA_HD = 64
GQA_W = GQA_HEADS * GQA_HD
GQA_SCALE = GQA_HD ** -0.5
LOG2E = math.log2(math.e)

N_GROUPS = 4
EXPERTS_PER_GROUP = 4
N_EXPERTS = N_GROUPS * EXPERTS_PER_GROUP
D_EXPERT = 512
PAIRS_PER_GROUP = EXPERTS_PER_GROUP * (EXPERTS_PER_GROUP - 1) // 2
GATHER_TILE = 256
GATHER_DEPTH = 3
MOE_TILE = GATHER_TILE

ROW_TILE = 256
LANES = 128
V_ROWS = 80
VMEM_LIMIT = 56 * 1024 * 1024


def _cparams(*sem):
    return pltpu.CompilerParams(dimension_semantics=sem, vmem_limit_bytes=VMEM_LIMIT)


def _dot(a, b):
    return jnp.dot(a, b, preferred_element_type=F32)


def _dot_nt(a, b):
    return lax.dot_general(a, b, (((1,), (1,)), ((), ())), preferred_element_type=F32)


def _dot_f32(a, b):
    a_hi = a.astype(BF16)
    a_lo = (a - a_hi.astype(F32)).astype(BF16)
    return _dot(a_hi, b) + _dot(a_lo, b)


def _rms(x, g):
    return x * lax.rsqrt(jnp.mean(x * x, axis=-1, keepdims=True) + NORM_EPS) * g


def _sigmoid(x):
    return 1.0 / (1.0 + jnp.exp(-x))


def _mod_kernel(cond_ref, w_ref, b_ref, o_ref):
    a = cond_ref[...]
    s = a * _sigmoid(a)
    w = w_ref[...]
    s_hi = s.astype(BF16)
    s_lo = (s - s_hi.astype(F32)).astype(BF16)
    w_hi = w.astype(BF16)
    w_lo = (w - w_hi.astype(F32)).astype(BF16)
    acc = _dot(s_hi, w_hi) + _dot(s_lo, w_hi) + _dot(s_hi, w_lo)
    o_ref[...] = acc + b_ref[...]


def _modulation(cond8, w_mod, b_mod):
    nl, d, n = w_mod.shape
    tn = 1536
    return pl.pallas_call(
        _mod_kernel,
        grid=(nl, n // tn),
        in_specs=[
            pl.BlockSpec((8, d), lambda l, j: (0, 0)),
            pl.BlockSpec((None, d, tn), lambda l, j: (l, 0, j)),
            pl.BlockSpec((None, 1, tn), lambda l, j: (l, 0, j)),
        ],
        out_specs=pl.BlockSpec((None, 8, tn), lambda l, j: (l, 0, j)),
        out_shape=jax.ShapeDtypeStruct((nl, 8, n), F32),
        compiler_params=_cparams("parallel", "parallel"),
        name="modulation",
    )(cond8, w_mod, b_mod.reshape(nl, 1, n))


def _mod_row(mod_ref, idx):
    m = mod_ref[...]
    d = m.shape[1] // N_MOD
    lat = m[0:1, idx * d:(idx + 1) * d]
    ctx = m[1:2, idx * d:(idx + 1) * d]
    return jnp.where(pl.program_id(0) == 0, ctx, lat)


_SEG = {}
_off = 0
for _name, _w in (("cq", 256), ("ckv", 128), ("kr", 128), ("krs", 128), ("ssm", 512), ("gq", 512),
                  ("gqs", 512), ("gk", 128), ("gks", 128), ("gv", 128), ("gate", 3072)):
    _SEG[_name] = (_off, _off + _w)
    _off += _w
W_CAT = _off


def _premix_kernel(x_ref, mod_ref, gpre_ref, w_ref, wuq_ref, wuqs_ref, wuk_ref, wuv_ref,
                   gcq_ref, gckv_ref, gq_ref, gqs_ref, gk_ref, gks_ref, ones_ref,
                   cm_ref, sm_ref, cg_ref, sg_ref,
                   qm_ref, km_ref, vm_ref, qg_ref, kg_ref, vg_ref, u_ref, gate_ref):
    x = x_ref[...]
    h = _rms(x, gpre_ref[...]) * (1.0 + _mod_row(mod_ref, 1)) + _mod_row(mod_ref, 0)
    hb = h.astype(BF16)

    def proj(name):
        a, b = _SEG[name]
        return _dot(hb, w_ref[:, a:b])

    cm = cm_ref[...]
    sm = sm_ref[...]
    cg = cg_ref[...]
    sg = sg_ref[...]

    cqn = _rms(proj("cq"), gcq_ref[...]).astype(BF16)
    ckvn = _rms(proj("ckv"), gckv_ref[...]).astype(BF16)
    q = _dot(cqn, wuq_ref[...])
    qs = _dot(cqn, wuqs_ref[...])
    kn = _dot(ckvn, wuk_ref[...])
    kr = proj("kr") * cm + proj("krs") * sm
    tr = x.shape[0]
    vrow = lax.broadcasted_iota(jnp.int32, (V_ROWS - MLA_V, tr), 0)
    v_tail = jnp.where(vrow == 0, 1.0, 0.0).astype(BF16)
    for hd in range(MLA_HEADS):
        sl = slice(hd * MLA_HEAD_PAD, (hd + 1) * MLA_HEAD_PAD)
        qh = (q[:, sl] * cm + qs[:, sl] * sm) * (MLA_SCALE * LOG2E)
        qm_ref[hd] = qh.T.astype(BF16)
        km_ref[hd] = (kn[:, sl] + kr).astype(BF16)
    vmt = _dot(ckvn, wuv_ref[...]).T
    for hd in range(MLA_HEADS):
        vm_ref[hd, 0:MLA_V, :] = vmt[hd * MLA_V:(hd + 1) * MLA_V, :].astype(BF16)
        vm_ref[hd, MLA_V:V_ROWS, :] = v_tail

    ones = ones_ref[...]

    def head_rms_scale(v, width):
        ms = _dot_f32(v * v, ones[:width, :width]) * (1.0 / GQA_HD)
        return lax.rsqrt(ms + NORM_EPS)

    gq = proj("gq")
    gqs = proj("gqs")
    rq = head_rms_scale(gq, GQA_W)
    cg4 = jnp.concatenate([cg] * (GQA_W // LANES), axis=1)
    sg4 = jnp.concatenate([sg] * (GQA_W // LANES), axis=1)
    qg = rq * (gq * gq_ref[...] * cg4 + gqs * gqs_ref[...] * sg4)
    qgt = (qg * (GQA_SCALE * LOG2E)).T.astype(BF16)
    zeros_q = jnp.zeros((GQA_HD, tr), BF16)
    for hd in range(GQA_HEADS):
        kvh = hd // (GQA_HEADS // GQA_KV_HEADS)
        qh = qgt[hd * GQA_HD:(hd + 1) * GQA_HD, :]
        qg_ref[hd] = jnp.concatenate([qh, zeros_q] if kvh == 0 else [zeros_q, qh], axis=0)
    gk = proj("gk")
    gks = proj("gks")
    rk = head_rms_scale(gk, GQA_KV_HEADS * GQA_HD)
    kg_ref[0] = (rk * (gk * gk_ref[...] * cg + gks * gks_ref[...] * sg)).astype(BF16)
    vgt = proj("gv").T
    for kvh in range(GQA_KV_HEADS):
        vg_ref[kvh, 0:GQA_HD, :] = vgt[kvh * GQA_HD:(kvh + 1) * GQA_HD, :].astype(BF16)
        vg_ref[kvh, GQA_HD:V_ROWS, :] = v_tail

    u_ref[...] = proj("ssm").astype(BF16)
    gate_ref[...] = _sigmoid(proj("gate")).astype(BF16)


def _premix(rows, mod_l, g_pre, wts, tabs):
    r, d = rows.shape
    nt = r // ROW_TILE
    row_spec = lambda w: pl.BlockSpec((ROW_TILE, w), lambda i: (i, 0))
    full = lambda a: pl.BlockSpec(a.shape, lambda i: (0,) * a.ndim)
    consts = [g_pre, wts["w_cat"], wts["wuq"], wts["wuqs"], wts["wuk"], wts["wuv"],
              wts["g_cq"], wts["g_ckv"], wts["g_q"], wts["g_qs"], wts["g_k"], wts["g_ks"], wts["ones"]]
    t_spec = lambda nh, f: pl.BlockSpec((nh, f, ROW_TILE), lambda i: (0, 0, i))
    k_spec = lambda nh, f: pl.BlockSpec((nh, ROW_TILE, f), lambda i: (0, i, 0))
    outs = [(t_spec(MLA_HEADS, MLA_HEAD_PAD), (MLA_HEADS, MLA_HEAD_PAD, r)),
            (k_spec(MLA_HEADS, MLA_HEAD_PAD), (MLA_HEADS, r, MLA_HEAD_PAD)),
            (t_spec(MLA_HEADS, V_ROWS), (MLA_HEADS, V_ROWS, r)),
            (t_spec(GQA_HEADS, GQA_KV_HEADS * GQA_HD), (GQA_HEADS, GQA_KV_HEADS * GQA_HD, r)),
            (k_spec(1, GQA_KV_HEADS * GQA_HD), (1, r, GQA_KV_HEADS * GQA_HD)),
            (t_spec(GQA_KV_HEADS, V_ROWS), (GQA_KV_HEADS, V_ROWS, r)),
            (row_spec(SSM_W), (r, SSM_W)),
            (row_spec(3 * d), (r, 3 * d))]
    return pl.pallas_call(
        _premix_kernel,
        grid=(nt,),
        in_specs=[row_spec(d), full(mod_l)] + [full(a) for a in consts] + [row_spec(LANES)] * 4,
        out_specs=[s for s, _ in outs],
        out_shape=[jax.ShapeDtypeStruct(shp, BF16) for _, shp in outs],
        compiler_params=_cparams("parallel"),
        name="premix",
    )(rows, mod_l, *consts, tabs["cm"], tabs["sm"], tabs["cg"], tabs["sg"])


def _attn_kernel(q_ref, k_ref, v_ref, o_ref, s_ref, *, tk, nsub, n_ctx, n_iter, dv, hb, k_shared, v_shared,
                 ahead):
    n_items = hb * nsub
    tq = q_ref.shape[2]

    def scores(h, off, size):
        return _dot(k_ref[0 if k_shared else h, pl.ds(off, size), :], q_ref[h])

    def update(h, m, acc, s, off, size):
        m_new = jnp.maximum(m, jnp.max(s, axis=0, keepdims=True))
        p = jnp.exp2(s - m_new).astype(BF16)
        pv = _dot(v_ref[0 if v_shared else h, :, pl.ds(off, size)], p)
        return m_new, jnp.exp2(m - m_new) * acc + pv

    def issue(item, base):
        h, sub = item % hb, item // hb
        s_ref[item] = scores(h, pl.multiple_of(base + sub * tk, LANES), tk)

    m0 = jnp.full((1, tq), -1e30, F32)
    acc0 = jnp.zeros((V_ROWS, tq), F32)
    s_ctx = [scores(h, 0, n_ctx) for h in range(hb)]
    for item in range(ahead):
        issue(item, n_ctx)
    carry = tuple(update(h, m0, acc0, s_ctx[h], 0, n_ctx) for h in range(hb))

    def body(j, carry):
        carry = list(carry)
        base = n_ctx + j * (nsub * tk)
        base_next = n_ctx + jnp.minimum(j + 1, n_iter - 1) * (nsub * tk)
        for item in range(n_items):
            nxt = item + ahead
            if nxt < n_items:
                issue(nxt, base)
            else:
                issue(nxt - n_items, base_next)
            h, sub = item % hb, item // hb
            off = pl.multiple_of(base + sub * tk, LANES)
            carry[h] = update(h, carry[h][0], carry[h][1], s_ref[item], off, tk)
        return tuple(carry)

    carry = lax.fori_loop(0, jnp.where(pl.program_id(1) == 0, 0, n_iter), body, carry)
    for h in range(hb):
        acc = carry[h][1]
        o_ref[h] = acc[:dv] / acc[dv:dv + 1]


def _attention(qt, k, vt, *, hb, nsub, tq=256, tk=512, ahead=2):
    nh, dq, r = qt.shape
    k_shared = k.shape[0] != nh
    v_shared = vt.shape[0] != nh
    assert nh % hb == 0 and k.shape[0] in (nh, 1) and vt.shape[0] in (nh, nh // hb)
    dv = V_ROWS - 16
    assert (r - ROW_TILE) % (tk * nsub) == 0 and ahead < hb * nsub
    kern = functools.partial(_attn_kernel, tk=tk, nsub=nsub, n_ctx=ROW_TILE, n_iter=(r - ROW_TILE) // (tk * nsub),
                             dv=dv, hb=hb, k_shared=k_shared, v_shared=v_shared, ahead=ahead)
    k_spec = (pl.BlockSpec((1, r, dq), lambda g, i: (0, 0, 0)) if k_shared
              else pl.BlockSpec((hb, r, dq), lambda g, i: (g, 0, 0)))
    v_spec = pl.BlockSpec((1 if v_shared else hb, V_ROWS, r), lambda g, i: (g, 0, 0))
    return pl.pallas_call(
        kern,
        grid=(nh // hb, r // tq),
        in_specs=[pl.BlockSpec((hb, dq, tq), lambda g, i: (g, 0, i)), k_spec, v_spec],
        out_specs=pl.BlockSpec((hb, dv, tq), lambda g, i: (g, 0, i)),
        out_shape=jax.ShapeDtypeStruct((nh, dv, r), F32),
        scratch_shapes=[pltpu.VMEM((hb * nsub, tk, tq), F32)],
        compiler_params=_cparams("parallel", "arbitrary"),
        name="attention",
    )(qt, k, vt)


def _s5_kernel(u_ref, blk_ref, bp_ref, cpt_ref, ar_ref, ai_ref, y_ref, xr_ref, xi_ref, *, n_ctx_chunks):
    t, j, _ = u_ref.shape
    npair = t // 2
    ns = xr_ref.shape[1]
    row = lax.broadcasted_iota(jnp.int32, (j, ns), 0)

    def run(reverse):
        ucat = jnp.concatenate([u_ref[k] for k in range(t)], axis=1)
        xc = _dot(ucat, bp_ref[...])
        if reverse:
            shift = j - n_ctx_chunks - 1
            keep = row < j - 1
        else:
            shift = 1
            keep = row >= 1
        xr_ref[...] = jnp.where(keep, pltpu.roll(xc[:, :ns], shift, 0), 0.0)
        xi_ref[...] = jnp.where(keep, pltpu.roll(xc[:, ns:], shift, 0), 0.0)
        step = 1
        k = 0
        while step < j:
            ar = ar_ref[k:k + 1, :]
            ai = ai_ref[k:k + 1, :]
            xr = xr_ref[...]
            xi = xi_ref[...]
            if reverse:
                keep = row < j - step
                sr = jnp.where(keep, pltpu.roll(xr, j - step, 0), 0.0)
                si = jnp.where(keep, pltpu.roll(xi, j - step, 0), 0.0)
            else:
                keep = row >= step
                sr = jnp.where(keep, pltpu.roll(xr, step, 0), 0.0)
                si = jnp.where(keep, pltpu.roll(xi, step, 0), 0.0)
            xr_ref[...] = xr + ar * sr - ai * si
            xi_ref[...] = xi + ar * si + ai * sr
            step *= 2
            k += 1
        xin = jnp.concatenate([xr_ref[...], xi_ref[...]], axis=1)
        if reverse:
            xin = pltpu.roll(xin, n_ctx_chunks, 0)
        xin = xin.astype(BF16)
        for po in range(npair):
            y = _dot_nt(xin, cpt_ref[po * 256:(po + 1) * 256, :])
            pins = range(po, npair) if reverse else range(0, po + 1)
            for pi_ in pins:
                upair = jnp.concatenate([u_ref[2 * pi_], u_ref[2 * pi_ + 1]], axis=1)
                y += _dot(upair, blk_ref[abs(po - pi_)])
            if reverse:
                y_ref[2 * po] += y[:, :128]
                y_ref[2 * po + 1] += y[:, 128:]
            else:
                y_ref[2 * po] = y[:, :128]
                y_ref[2 * po + 1] = y[:, 128:]

    @pl.when(pl.program_id(1) == 0)
    def _():
        run(False)

    @pl.when(pl.program_id(1) == 1)
    def _():
        run(True)


def _s5_scan(ut, mats, n_ctx_chunks):
    t, j, w = ut.shape
    no = w // LANES
    ns = (LANES // SSM_GROUP) * SSM_STATE
    wspec = lambda *tail: pl.BlockSpec((None, None) + tail, lambda o, d: (d, o) + (0,) * len(tail))
    kern = functools.partial(_s5_kernel, n_ctx_chunks=n_ctx_chunks)
    return pl.pallas_call(
        kern,
        grid=(no, SSM_DIRS),
        in_specs=[pl.BlockSpec((t, j, LANES), lambda o, d: (0, 0, o)),
                  wspec(t // 2, 256, 256), wspec(t * LANES, 2 * ns), wspec(t * LANES, 2 * ns),
                  wspec(16, ns), wspec(16, ns)],
        out_specs=pl.BlockSpec((t, j, LANES), lambda o, d: (0, 0, o)),
        out_shape=jax.ShapeDtypeStruct((t, j, w), F32),
        scratch_shapes=[pltpu.VMEM((j, ns), F32)] * 2,
        compiler_params=_cparams("parallel", "arbitrary"),
        name="s5_scan",
    )(ut, mats["blk"], mats["bp"], mats["cpt"], mats["ar"], mats["ai"])


def _s5_matrices(a_re, a_im, log_dt, b_re, b_im, c_re, c_im, d_skip):
    t = SSM_CHUNK
    a_re = a_re.astype(F32)
    a_im = a_im.astype(F32)
    dt = jnp.exp(log_dt.astype(F32))[..., None]
    den = a_re * a_re + a_im * a_im

    def lpow(k):
        kf = jnp.asarray(k, F32)
        mag = jnp.exp(a_re * dt * kf)
        return mag * jnp.cos(a_im * dt * kf), mag * jnp.sin(a_im * dt * kf)

    lr, li = lpow(1.0)
    cr = ((lr - 1.0) * a_re + li * a_im) / den
    ci = (li * a_re - (lr - 1.0) * a_im) / den
    bbr = cr[..., None] * b_re - ci[..., None] * b_im
    bbi = cr[..., None] * b_im + ci[..., None] * b_re
    c_re = c_re.astype(F32)
    c_im = c_im.astype(F32)

    ks = jnp.arange(t + 1, dtype=F32)[:, None, None, None]
    pr, pi = lpow(ks)
    lbr = pr[:t, ..., None] * bbr - pi[:t, ..., None] * bbi
    lbi = pr[:t, ..., None] * bbi + pi[:t, ..., None] * bbr
    kern = (jnp.einsum('dgop,kdgpc->dgkoc', c_re, lbr, precision='highest')
            - jnp.einsum('dgop,kdgpc->dgkoc', c_im, lbi, precision='highest'))
    skip = d_skip.astype(F32).reshape(SSM_GROUPS, SSM_GROUP)
    eye = jnp.eye(SSM_GROUP, dtype=F32)
    kern = kern.at[0, :, 0].add(skip[:, :, None] * eye)
    gl = LANES // SSM_GROUP
    no = SSM_GROUPS // gl
    def widen(compact, col_of, n_cols, row_group, col_group):
        expand = np.zeros((compact.shape[-1], n_cols), np.float32)
        for h in range(gl):
            src_cols = np.arange(compact.shape[-1])
            expand[src_cols, col_of(src_cols, h)] = 1.0
        wide = jnp.einsum('...rj,jk->...rk', compact.astype(BF16), jnp.asarray(expand, BF16),
                          preferred_element_type=BF16)
        rows_g = row_group(lax.broadcasted_iota(jnp.int32, wide.shape[-2:], 0))
        cols_g = col_group(lax.broadcasted_iota(jnp.int32, wide.shape[-2:], 1))
        return jnp.where(rows_g == cols_g, wide, jnp.zeros((), BF16))

    dd = jnp.arange(t // 2)[:, None, None]
    ti = jnp.arange(2)[None, :, None]
    to = jnp.arange(2)[None, None, :]
    lag = jnp.stack([2 * dd + to - ti, 2 * dd + ti - to])
    pick = jax.vmap(lambda kd, ld: jnp.take(kd, jnp.clip(ld, 0, t - 1).reshape(-1), axis=1))(kern, lag)
    pick = pick.reshape(SSM_DIRS, no, gl, t // 2, 2, 2, SSM_GROUP, SSM_GROUP)
    pick = jnp.where((lag >= 0)[:, None, None, :, :, :, None, None], pick, 0.0)
    kc = jnp.transpose(pick, (0, 1, 3, 4, 2, 7, 5, 6)).reshape(SSM_DIRS, no, t // 2, 256, 2 * SSM_GROUP)
    blk = widen(kc, lambda j, h: (j // SSM_GROUP) * LANES + h * SSM_GROUP + j % SSM_GROUP, 256,
                lambda r: (r // SSM_GROUP) % gl, lambda c: (c // SSM_GROUP) % gl)
    def contrib(lb):
        both = jnp.stack([lb[::-1, 0], lb[:, 1]])
        both = both.reshape(SSM_DIRS, t, no, gl, SSM_STATE, SSM_GROUP)
        return jnp.transpose(both, (0, 2, 1, 3, 5, 4)).reshape(SSM_DIRS, no, t * LANES, SSM_STATE)
    bp = widen(jnp.concatenate([contrib(lbr), contrib(lbi)], axis=-1),
               lambda j, h: (j // SSM_STATE) * (gl * SSM_STATE) + h * SSM_STATE + j % SSM_STATE, 2 * gl * SSM_STATE,
               lambda r: (r // SSM_GROUP) % gl, lambda c: (c // SSM_STATE) % gl)
    p1r = pr[1:, :, :, None, :]
    p1i = pi[1:, :, :, None, :]
    rd_r = c_re * p1r - c_im * p1i
    rd_i = -(c_re * p1i + c_im * p1r)
    def readout(rd):
        both = jnp.stack([rd[:, 0], rd[::-1, 1]])
        both = both.reshape(SSM_DIRS, t, no, gl, SSM_GROUP, SSM_STATE)
        return jnp.transpose(both, (0, 2, 1, 3, 4, 5)).reshape(SSM_DIRS, no, t * LANES, SSM_STATE)
    cpt = widen(jnp.concatenate([readout(rd_r), readout(rd_i)], axis=-1),
                lambda j, h: (j // SSM_STATE) * (gl * SSM_STATE) + h * SSM_STATE + j % SSM_STATE, 2 * gl * SSM_STATE,
                lambda r: (r // SSM_GROUP) % gl, lambda c: (c // SSM_STATE) % gl)
    kk = (t * 2.0 ** jnp.minimum(jnp.arange(16), 10)).astype(F32)[:, None, None, None]
    qr, qi = lpow(kk)
    lanes = lambda q: jnp.transpose(q, (1, 2, 0, 3)).reshape(SSM_DIRS, no, gl, 16, SSM_STATE)
    ar = jnp.transpose(lanes(qr), (0, 1, 3, 2, 4)).reshape(SSM_DIRS, no, 16, gl * SSM_STATE)
    ai = jnp.transpose(lanes(qi), (0, 1, 3, 2, 4)).reshape(SSM_DIRS, no, 16, gl * SSM_STATE)
    return dict(blk=blk, bp=bp, cpt=cpt, ar=ar, ai=ai)


def _merge_kernel(x_ref, mod_ref, om_ref, og_ref, y_ref, gate_ref,
                  wglu_ref, bglu_ref, wbm_ref, wbs_ref, wbg_ref, wout_ref,
                  gpost_ref, gffn_ref, wrt_hi_ref, wrt_lo_ref, brt_ref,
                  xo_ref, h_ref, bk_ref):
    d = x_ref.shape[1]
    y = y_ref[...]
    y1 = 0.5 * y * (1.0 + jnp.tanh(math.sqrt(2.0 / math.pi) * (y + 0.044715 * (y * y * y))))
    ssm = y1 * _sigmoid(_dot(y1.astype(BF16), wglu_ref[...]) + bglu_ref[...])
    gate = gate_ref[...]
    o_mla = om_ref[...].reshape(-1, om_ref.shape[2]).T.astype(BF16)
    o_gqa = og_ref[...].reshape(-1, og_ref.shape[2]).T.astype(BF16)
    m = (gate[:, 0:d].astype(F32) * _dot(o_mla, wbm_ref[...])
         + gate[:, d:2 * d].astype(F32) * _dot(ssm.astype(BF16), wbs_ref[...])
         + gate[:, 2 * d:3 * d].astype(F32) * _dot(o_gqa, wbg_ref[...]))
    ymix = _dot(m.astype(BF16), wout_ref[...])
    x = x_ref[...] + _mod_row(mod_ref, 2) * _rms(ymix, gpost_ref[...])
    xo_ref[...] = x
    h = _rms(x, gffn_ref[...]) * (1.0 + _mod_row(mod_ref, 4)) + _mod_row(mod_ref, 3)
    h_ref[:, 0:d] = h

    h_hi = h.astype(BF16)
    h_lo = (h - h_hi.astype(F32)).astype(BF16)
    lg = (_dot(h_hi, wrt_hi_ref[...]) + _dot(h_lo, wrt_hi_ref[...]) + _dot(h_hi, wrt_lo_ref[...])
          + brt_ref[...])
    lane = lax.broadcasted_iota(jnp.int32, lg.shape, 1)
    neg = jnp.float32(-1e30)
    is_g = lane < N_GROUPS
    gl = jnp.where(is_g, lg, neg)
    gmax = jnp.max(gl, axis=-1, keepdims=True)
    gsel = jnp.min(jnp.where(is_g & (gl == gmax), lane, LANES), axis=-1, keepdims=True)
    pg = 1.0 / jnp.sum(jnp.where(is_g, jnp.exp(gl - gmax), 0.0), axis=-1, keepdims=True)
    lo = N_GROUPS + gsel * EXPERTS_PER_GROUP
    in_grp = (lane >= lo) & (lane < lo + EXPERTS_PER_GROUP)
    el = jnp.where(in_grp, lg, neg)
    m1 = jnp.max(el, axis=-1, keepdims=True)
    i1 = jnp.min(jnp.where(in_grp & (el == m1), lane, LANES), axis=-1, keepdims=True)
    el2 = jnp.where(lane == i1, neg, el)
    m2 = jnp.max(el2, axis=-1, keepdims=True)
    i2 = jnp.min(jnp.where(in_grp & (el2 == m2), lane, LANES), axis=-1, keepdims=True)
    e2 = jnp.exp(m2 - m1)
    t1 = pg / (1.0 + e2)
    first = i1 < i2
    ea = jnp.minimum(i1, i2) - lo
    eb = jnp.maximum(i1, i2) - lo
    bucket = gsel * PAIRS_PER_GROUP + jnp.right_shift(ea * (7 - ea), 1) + (eb - ea - 1)
    w_a = jnp.where(first, t1, t1 * e2)
    w_b = jnp.where(first, t1 * e2, t1)
    route = (jnp.where(lane == 0, bucket.astype(F32), 0.0) + jnp.where(lane == 1, w_a, 0.0)
             + jnp.where(lane == 2, w_b, 0.0))
    h_ref[:, d:d + LANES] = route
    bk_ref[...] = route


def _merge(rows, mod_l, o_mla, o_gqa, y_ssm, gates, wts):
    r, d = rows.shape
    nt = r // ROW_TILE
    row_spec = lambda w: pl.BlockSpec((ROW_TILE, w), lambda i: (i, 0))
    full = lambda a: pl.BlockSpec(a.shape, lambda i: (0,) * a.ndim)
    consts = [wts["w_glu"], wts["b_glu"], wts["w_br_mla"], wts["w_br_ssm"], wts["w_br_gqa"], wts["w_out"],
              wts["g_post_mix"], wts["g_pre_ffn"], wts["w_rt_hi"], wts["w_rt_lo"], wts["b_rt"]]
    return pl.pallas_call(
        _merge_kernel,
        grid=(nt,),
        in_specs=[row_spec(d), full(mod_l),
                  pl.BlockSpec(o_mla.shape[:2] + (ROW_TILE,), lambda i: (0, 0, i)),
                  pl.BlockSpec(o_gqa.shape[:2] + (ROW_TILE,), lambda i: (0, 0, i)),
                  row_spec(SSM_W), row_spec(3 * d)] + [full(a) for a in consts],
        out_specs=[row_spec(d), row_spec(d + LANES), row_spec(LANES)],
        out_shape=[jax.ShapeDtypeStruct((r, d), F32), jax.ShapeDtypeStruct((r, d + LANES), F32),
                   jax.ShapeDtypeStruct((r, LANES), F32)],
        compiler_params=_cparams("parallel"),
        name="merge",
    )(rows, mod_l, o_mla, o_gqa, y_ssm, gates, *consts)


def _row_copy(idx, r, src_hbm, buf, sem, slot):
    return pltpu.make_async_copy(src_hbm.at[pl.ds(idx, 1)], buf.at[slot, pl.ds(r, 1)], sem.at[slot])


def _gathered_tile(idx_ref, src_hbm, buf, sem):
    i = pl.program_id(0)
    last = pl.num_programs(0) - 1
    depth, n = buf.shape[0], buf.shape[1]
    ahead = depth - 1

    def start(step, dst_slot):
        for r in range(n):
            _row_copy(idx_ref[jnp.minimum(step, last) * n + r], r, src_hbm, buf, sem, dst_slot).start()

    def wait(dst_slot):
        for r in range(n):
            _row_copy(0, r, src_hbm, buf, sem, dst_slot).wait()

    @pl.when(i == 0)
    def _():
        for k in range(ahead):
            start(k, k)

    slot = i % depth
    wait(slot)
    tile = buf[slot]

    def finish():
        start(i + ahead, (i + ahead) % depth)

        @pl.when(i == last)
        def _():
            for k in range(1, depth):
                wait((i + k) % depth)

    return tile, finish


def _moe_kernel(src_ref, ea_ref, eb_ref, tok_hbm, wga_ref, wua_ref, wda_ref, wgb_ref, wub_ref, wdb_ref,
                gpost_ref, o_ref, buf, sem):
    tok, finish = _gathered_tile(src_ref, tok_hbm, buf, sem)
    d = o_ref.shape[1]
    h = tok[:, 0:d].astype(BF16)
    rt = tok[:, d:d + LANES]
    lane = lax.broadcasted_iota(jnp.int32, rt.shape, 1)
    w_a = jnp.sum(jnp.where(lane == 1, rt, 0.0), axis=-1, keepdims=True)
    w_b = jnp.sum(jnp.where(lane == 2, rt, 0.0), axis=-1, keepdims=True)

    def expert(wg_ref, wu_ref, wd_ref):
        a = _dot(h, wg_ref[...])
        hid = (a * _sigmoid(a)) * _dot(h, wu_ref[...])
        return _dot(hid.astype(BF16), wd_ref[...])

    y = w_a * expert(wga_ref, wua_ref, wda_ref) + w_b * expert(wgb_ref, wub_ref, wdb_ref)
    o_ref[...] = _rms(y, gpost_ref[...]).astype(o_ref.dtype)
    finish()


def _moe(tokens, src, tile_ea, tile_eb, wg, wu, wd, g_post):
    w = tokens.shape[1]
    d = w - LANES
    ne, _, de = wg.shape
    nt = src.shape[0] // GATHER_TILE
    w_in = lambda sel: pl.BlockSpec((None, d, de), lambda i, src, ea, eb: ((ea, eb)[sel][i], 0, 0))
    w_out = lambda sel: pl.BlockSpec((None, de, d), lambda i, src, ea, eb: ((ea, eb)[sel][i], 0, 0))
    return pl.pallas_call(
        _moe_kernel,
        grid_spec=pltpu.PrefetchScalarGridSpec(
            num_scalar_prefetch=3,
            grid=(nt,),
            in_specs=[
                pl.BlockSpec(memory_space=pl.ANY),
                w_in(0), w_in(0), w_out(0), w_in(1), w_in(1), w_out(1),
                pl.BlockSpec(g_post.shape, lambda i, src, ea, eb: (0, 0)),
            ],
            out_specs=pl.BlockSpec((GATHER_TILE, d), lambda i, src, ea, eb: (i, 0)),
            scratch_shapes=[pltpu.VMEM((GATHER_DEPTH, GATHER_TILE, w), F32),
                            pltpu.SemaphoreType.DMA((GATHER_DEPTH,))],
        ),
        out_shape=jax.ShapeDtypeStruct((src.shape[0], d), F32),
        compiler_params=_cparams("arbitrary"),
        name="moe",
    )(src, tile_ea, tile_eb, tokens, wg, wu, wd, wg, wu, wd, g_post)


def _dispatch(bucket, n_rows):
    nb = N_GROUPS * PAIRS_PER_GROUP
    n_tiles = n_rows // MOE_TILE + nb
    onehot = (bucket[:, None] == jnp.arange(nb, dtype=jnp.int32)[None, :]).astype(jnp.int32)
    counts = jnp.sum(onehot, axis=0)
    rank = jnp.sum((jnp.cumsum(onehot, axis=0) - onehot) * onehot, axis=1)
    tiles_per = (counts + MOE_TILE - 1) // MOE_TILE
    tile_end = jnp.cumsum(tiles_per)
    start = (tile_end - tiles_per) * MOE_TILE
    pos = jnp.take(start, bucket) + rank
    src = jnp.zeros((n_tiles * MOE_TILE,), jnp.int32).at[pos].set(jnp.arange(n_rows, dtype=jnp.int32))
    tile_id = jnp.arange(n_tiles, dtype=jnp.int32)
    tile_bucket = jnp.minimum(jnp.searchsorted(tile_end, tile_id, side='right'), nb - 1).astype(jnp.int32)
    tile_used = (tile_id < tile_end[-1]).astype(jnp.int32)
    last_bucket = jnp.take(tile_bucket, jnp.maximum(tile_end[-1] - 1, 0))
    tile_bucket = jnp.where(tile_used == 1, tile_bucket, last_bucket)
    pair_a = jnp.asarray([0, 0, 0, 1, 1, 2], jnp.int32)
    pair_b = jnp.asarray([1, 2, 3, 2, 3, 3], jnp.int32)
    grp = tile_bucket // PAIRS_PER_GROUP
    tile_ea = grp * EXPERTS_PER_GROUP + jnp.take(pair_a, tile_bucket % PAIRS_PER_GROUP)
    tile_eb = grp * EXPERTS_PER_GROUP + jnp.take(pair_b, tile_bucket % PAIRS_PER_GROUP)
    return pos, src, tile_ea, tile_eb


def _residual_kernel(pos_ref, x_ref, z_hbm, mod_ref, o_ref, buf, sem):
    z, finish = _gathered_tile(pos_ref, z_hbm, buf, sem)
    o_ref[...] = x_ref[...] + _mod_row(mod_ref, 5) * z
    finish()


def _residual(rows, z_sorted, pos, mod_l, drop_context):
    r, d = rows.shape
    row_spec = pl.BlockSpec((GATHER_TILE, d), lambda i, pos: (i, 0))
    out_rows = r - GATHER_TILE if drop_context else r
    out_spec = pl.BlockSpec((GATHER_TILE, d), lambda i, pos: (jnp.maximum(i - 1, 0), 0)) if drop_context else row_spec
    return pl.pallas_call(
        _residual_kernel,
        grid_spec=pltpu.PrefetchScalarGridSpec(
            num_scalar_prefetch=1,
            grid=(r // GATHER_TILE,),
            in_specs=[row_spec, pl.BlockSpec(memory_space=pl.ANY),
                      pl.BlockSpec(mod_l.shape, lambda i, pos: (0, 0))],
            out_specs=out_spec,
            scratch_shapes=[pltpu.VMEM((GATHER_DEPTH, GATHER_TILE, d), F32),
                            pltpu.SemaphoreType.DMA((GATHER_DEPTH,))],
        ),
        out_shape=jax.ShapeDtypeStruct((out_rows, d), F32),
        compiler_params=_cparams("arbitrary"),
        name="residual",
    )(pos, rows, z_sorted, mod_l)


def _rope_tables(seq, n_ctx):
    n_rows = seq // GRID_W

    def pattern(width):
        half = width // 4
        freqs = (np.float32(ROPE_THETA) ** (-np.arange(half, dtype=np.float32) / np.float32(half))).astype(np.float32)
        ar = np.arange(n_rows, dtype=np.float32)[:, None] * freqs[None, :]
        ac = np.arange(GRID_W, dtype=np.float32)[:, None] * freqs[None, :]
        by_row = lambda tab: jnp.broadcast_to(jnp.asarray(tab, F32)[:, None, :], (n_rows, GRID_W, half)).reshape(seq, half)
        by_col = lambda tab: jnp.broadcast_to(jnp.asarray(tab, F32)[None, :, :], (n_rows, GRID_W, half)).reshape(seq, half)
        cr, sr, cc, sc = by_row(np.cos(ar)), by_row(np.sin(ar)), by_col(np.cos(ac)), by_col(np.sin(ac))
        c = jnp.concatenate([cr, cr, cc, cc], axis=1)
        s = jnp.concatenate([-sr, sr, -sc, sc], axis=1)
        return c, s

    c32, s32 = pattern(MLA_ROPE)
    c64, s64 = pattern(GQA_HD)
    one = jnp.ones((seq, 1), F32)
    cm = jnp.concatenate([one * jnp.ones((1, MLA_NOPE)), c32, one * jnp.ones((1, 32))], axis=1)
    sm = jnp.concatenate([jnp.zeros((seq, MLA_NOPE)), s32, jnp.zeros((seq, 32))], axis=1)
    cg = jnp.concatenate([c64, c64], axis=1)
    sg = jnp.concatenate([s64, s64], axis=1)
    ctx_c = jnp.ones((n_ctx, LANES), F32)
    ctx_s = jnp.zeros((n_ctx, LANES), F32)
    cat = lambda a, b: jnp.concatenate([a, b], axis=0)
    return dict(cm=cat(ctx_c, cm), sm=cat(ctx_s, sm), cg=cat(ctx_c, cg), sg=cat(ctx_s, sg))


def _swap_perm(width):
    q = width // 4
    return np.concatenate([np.arange(q, 2 * q), np.arange(0, q), np.arange(3 * q, 4 * q), np.arange(2 * q, 3 * q)])


def _layer_weights(l, p):
    d = p["w_in"].shape[1]
    w_in = p["w_in"][l]
    o = 0
    cq = w_in[:, o:o + MLA_Q_RANK]; o += MLA_Q_RANK
    ckv = w_in[:, o:o + MLA_KV_RANK]; o += MLA_KV_RANK
    kr = w_in[:, o:o + MLA_ROPE]; o += MLA_ROPE
    ssm = w_in[:, o:o + SSM_W]; o += SSM_W
    gq = w_in[:, o:o + GQA_W]; o += GQA_W
    gk = w_in[:, o:o + GQA_KV_HEADS * GQA_HD]; o += GQA_KV_HEADS * GQA_HD
    gv = w_in[:, o:o + GQA_KV_HEADS * GQA_HD]; o += GQA_KV_HEADS * GQA_HD
    gate = w_in[:, o:]
    p32 = _swap_perm(MLA_ROPE)
    p64 = _swap_perm(GQA_HD)
    perm_q = np.concatenate([p64 + GQA_HD * h for h in range(GQA_HEADS)])
    perm_k = np.concatenate([p64 + GQA_HD * h for h in range(GQA_KV_HEADS)])
    z = lambda n: jnp.zeros((d, n), F32)
    kr128 = jnp.concatenate([z(MLA_NOPE), kr, z(32)], axis=1)
    krs128 = jnp.concatenate([z(MLA_NOPE), kr[:, p32], z(32)], axis=1)
    w_cat = jnp.concatenate([cq, ckv, kr128, krs128, ssm, gq, gq[:, perm_q], gk, gk[:, perm_k], gv, gate],
                            axis=1).astype(BF16)

    w_uq = p["w_uq"][l].reshape(MLA_Q_RANK, MLA_HEADS, MLA_NOPE + MLA_ROPE)
    zq = jnp.zeros((MLA_Q_RANK, MLA_HEADS, 32), F32)
    wuq = jnp.concatenate([w_uq, zq], axis=2).reshape(MLA_Q_RANK, -1)
    wuqs = jnp.concatenate([jnp.zeros((MLA_Q_RANK, MLA_HEADS, MLA_NOPE), F32),
                            w_uq[:, :, MLA_NOPE:][:, :, p32], zq], axis=2).reshape(MLA_Q_RANK, -1)
    w_ukv = p["w_ukv"][l].reshape(MLA_KV_RANK, MLA_HEADS, MLA_NOPE + MLA_V)
    wuk = jnp.concatenate([w_ukv[:, :, :MLA_NOPE], jnp.zeros((MLA_KV_RANK, MLA_HEADS, 64), F32)],
                          axis=2).reshape(MLA_KV_RANK, -1)
    wuv = w_ukv[:, :, MLA_NOPE:].reshape(MLA_KV_RANK, -1)

    g_qn = p["g_qn"][l]
    g_kn = p["g_kn"][l]
    blk = np.arange(GQA_W) // GQA_HD
    ones = jnp.asarray((blk[:, None] == blk[None, :]).astype(np.float32), BF16)

    w_rt = jnp.zeros((d, LANES), F32)
    w_rt = w_rt.at[:, :N_GROUPS].set(p["w_group"][l]).at[:, N_GROUPS:N_GROUPS + N_EXPERTS].set(p["w_router"][l])
    w_rt_hi = w_rt.astype(BF16)
    w_rt_lo = (w_rt - w_rt_hi.astype(F32)).astype(BF16)
    b_rt = jnp.zeros((1, LANES), F32)
    b_rt = b_rt.at[0, :N_GROUPS].set(p["b_group"][l]).at[0, N_GROUPS:N_GROUPS + N_EXPERTS].set(p["b_router"][l])

    row = lambda v: v.reshape(1, -1).astype(F32)
    return dict(
        w_cat=w_cat, wuq=wuq.astype(BF16), wuqs=wuqs.astype(BF16), wuk=wuk.astype(BF16), wuv=wuv.astype(BF16),
        g_cq=row(p["g_cq"][l]), g_ckv=row(p["g_ckv"][l]),
        g_q=row(jnp.tile(g_qn, GQA_HEADS)), g_qs=row(jnp.tile(g_qn[p64], GQA_HEADS)),
        g_k=row(jnp.tile(g_kn, GQA_KV_HEADS)), g_ks=row(jnp.tile(g_kn[p64], GQA_KV_HEADS)),
        ones=ones,
        w_glu=p["w_glu"][l].astype(BF16), b_glu=row(p["b_glu"][l]),
        w_br_mla=p["w_br_mla"][l].astype(BF16), w_br_ssm=p["w_br_ssm"][l].astype(BF16),
        w_br_gqa=p["w_br_gqa"][l].astype(BF16), w_out=p["w_out"][l].astype(BF16),
        g_post_mix=row(p["g_post_mix"][l]), g_pre_ffn=row(p["g_pre_ffn"][l]),
        w_rt_hi=w_rt_hi, w_rt_lo=w_rt_lo, b_rt=b_rt,
    )


def kernel(x, c, ctx, c_ctx, w_mod, b_mod, g_pre_mix, g_post_mix, g_pre_ffn, g_post_ffn, w_in, g_cq, g_ckv, w_uq, w_ukv, g_qn, g_kn, ssm_a_re, ssm_a_im, ssm_log_dt, ssm_b_re, ssm_b_im, ssm_c_re, ssm_c_im, ssm_d, w_glu, b_glu, w_br_mla, w_br_ssm, w_br_gqa, w_out, w_group, b_group, w_router, b_router, w_exp_gate, w_exp_up, w_exp_down):
    assert x.shape[0] == 1 and ctx.shape[0] == 1
    seq, d = x.shape[1], x.shape[2]
    n_ctx = ctx.shape[1]
    assert n_ctx == ROW_TILE and seq % (SSM_CHUNK * SSM_SUPER) == 0
    depth = w_in.shape[0]
    params = dict(w_in=w_in, g_cq=g_cq, g_ckv=g_ckv, w_uq=w_uq, w_ukv=w_ukv, g_qn=g_qn, g_kn=g_kn,
                  w_glu=w_glu, b_glu=b_glu, w_br_mla=w_br_mla, w_br_ssm=w_br_ssm, w_br_gqa=w_br_gqa,
                  w_out=w_out, g_post_mix=g_post_mix, g_pre_ffn=g_pre_ffn,
                  w_group=w_group, b_group=b_group, w_router=w_router, b_router=b_router)

    cond8 = jnp.zeros((8, d), F32).at[0].set(c[0]).at[1].set(c_ctx)
    mods = _modulation(cond8, w_mod, b_mod)
    tabs = _rope_tables(seq, n_ctx)
    rows = jnp.concatenate([ctx[0], x[0]], axis=0)
    r = rows.shape[0]

    for l in range(depth):
        wts = _layer_weights(l, params)
        mod_l = mods[l]
        qm, km, vm, qg, kg, vg, u, gates = _premix(rows, mod_l, g_pre_mix[l].reshape(1, d), wts, tabs)

        o_mla = _attention(qm, km, vm, hb=2, nsub=16, ahead=4)
        o_gqa = _attention(qg, kg, vg, hb=GQA_HEADS // GQA_KV_HEADS, nsub=8, ahead=4)

        mats = _s5_matrices(ssm_a_re[l], ssm_a_im[l], ssm_log_dt[l], ssm_b_re[l], ssm_b_im[l],
                            ssm_c_re[l], ssm_c_im[l], ssm_d[l])
        ut = jnp.transpose(u.reshape(r // SSM_CHUNK, SSM_CHUNK, SSM_W), (1, 0, 2))
        yt = _s5_scan(ut, mats, n_ctx // SSM_CHUNK)
        y_ssm = jnp.transpose(yt, (1, 0, 2)).reshape(r, SSM_W)

        rows, tokens, route = _merge(rows, mod_l, o_mla, o_gqa, y_ssm, gates, wts)
        pos, src, tile_ea, tile_eb = _dispatch(route[:, 0].astype(jnp.int32), r)
        z_sorted = _moe(tokens, src, tile_ea, tile_eb,
                        w_exp_gate[l].astype(BF16), w_exp_up[l].astype(BF16), w_exp_down[l].astype(BF16),
                        g_post_ffn[l].reshape(1, d))
        rows = _residual(rows, z_sorted, pos, mod_l, drop_context=l == depth - 1)

    return rows.reshape(1, seq, d)
```

```python
import functools
import math

import jax
import jax.numpy as jnp
import numpy as np
from jax import lax
from jax.experimental import pallas as pl
from jax.experimental.pallas import tpu as pltpu

F32 = jnp.float32
BF16 = jnp.bfloat16

GRID_W = 64
ROPE_THETA = 10000.0
NORM_EPS = 1e-6
N_MOD = 6

MLA_HEADS = 8
MLA_NOPE = 64
MLA_ROPE = 32
MLA_V = 64
MLA_Q_RANK = 256
MLA_KV_RANK = 128
MLA_SCALE = (MLA_NOPE + MLA_ROPE) ** -0.5
MLA_HEAD_PAD = 128

SSM_GROUP = 16
SSM_GROUPS = 32
SSM_STATE = 64
SSM_W = SSM_GROUP * SSM_GROUPS
SSM_DIRS = 2
SSM_CHUNK = 16
SSM_SUPER = 16
SSM_CW = SSM_CHUNK * SSM_GROUP

GQA_HEADS = 8
GQA_KV_HEADS = 2
GQA_HD = 64
GQA_W = GQA_HEADS * GQA_HD
GQA_SCALE = GQA_HD ** -0.5
LOG2E = math.log2(math.e)

N_GROUPS = 4
EXPERTS_PER_GROUP = 4
N_EXPERTS = N_GROUPS * EXPERTS_PER_GROUP
D_EXPERT = 512
PAIRS_PER_GROUP = EXPERTS_PER_GROUP * (EXPERTS_PER_GROUP - 1) // 2
GATHER_TILE = 256
GATHER_DEPTH = 3
MOE_TILE = GATHER_TILE

ROW_TILE = 256
LANES = 128
V_ROWS = 80
VMEM_LIMIT = 56 * 1024 * 1024


def _cparams(*sem):
    return pltpu.CompilerParams(dimension_semantics=sem, vmem_limit_bytes=VMEM_LIMIT)


def _dot(a, b):
    return jnp.dot(a, b, preferred_element_type=F32)


def _dot_nt(a, b):
    return lax.dot_general(a, b, (((1,), (1,)), ((), ())), preferred_element_type=F32)


def _dot_f32(a, b):
    a_hi = a.astype(BF16)
    a_lo = (a - a_hi.astype(F32)).astype(BF16)
    return _dot(a_hi, b) + _dot(a_lo, b)


def _rms(x, g):
    return x * lax.rsqrt(jnp.mean(x * x, axis=-1, keepdims=True) + NORM_EPS) * g


def _sigmoid(x):
    return 1.0 / (1.0 + jnp.exp(-x))


def _mod_kernel(cond_ref, w_ref, b_ref, o_ref):
    a = cond_ref[...]
    s = a * _sigmoid(a)
    w = w_ref[...]
    s_hi = s.astype(BF16)
    s_lo = (s - s_hi.astype(F32)).astype(BF16)
    w_hi = w.astype(BF16)
    w_lo = (w - w_hi.astype(F32)).astype(BF16)
    acc = _dot(s_hi, w_hi) + _dot(s_lo, w_hi) + _dot(s_hi, w_lo)
    o_ref[...] = acc + b_ref[...]


def _modulation(cond8, w_mod, b_mod):
    nl, d, n = w_mod.shape
    tn = 1536
    return pl.pallas_call(
        _mod_kernel,
        grid=(nl, n // tn),
        in_specs=[
            pl.BlockSpec((8, d), lambda l, j: (0, 0)),
            pl.BlockSpec((None, d, tn), lambda l, j: (l, 0, j)),
            pl.BlockSpec((None, 1, tn), lambda l, j: (l, 0, j)),
        ],
        out_specs=pl.BlockSpec((None, 8, tn), lambda l, j: (l, 0, j)),
        out_shape=jax.ShapeDtypeStruct((nl, 8, n), F32),
        compiler_params=_cparams("parallel", "parallel"),
        name="modulation",
    )(cond8, w_mod, b_mod.reshape(nl, 1, n))


def _mod_row(mod_ref, idx):
    m = mod_ref[...]
    d = m.shape[1] // N_MOD
    lat = m[0:1, idx * d:(idx + 1) * d]
    ctx = m[1:2, idx * d:(idx + 1) * d]
    return jnp.where(pl.program_id(0) == 0, ctx, lat)


_SEG = {}
_off = 0
for _name, _w in (("cq", 256), ("ckv", 128), ("kr", 128), ("krs", 128), ("ssm", 512), ("gq", 512),
                  ("gqs", 512), ("gk", 128), ("gks", 128), ("gv", 128), ("gate", 3072)):
    _SEG[_name] = (_off, _off + _w)
    _off += _w
W_CAT = _off


def _premix_kernel(x_ref, mod_ref, gpre_ref, w_ref, wuq_ref, wuqs_ref, wuk_ref, wuv_ref,
                   gcq_ref, gckv_ref, gq_ref, gqs_ref, gk_ref, gks_ref, ones_ref,
                   cm_ref, sm_ref, cg_ref, sg_ref,
                   qm_ref, km_ref, vm_ref, qg_ref, kg_ref, vg_ref, u_ref, gate_ref):
    x = x_ref[...]
    h = _rms(x, gpre_ref[...]) * (1.0 + _mod_row(mod_ref, 1)) + _mod_row(mod_ref, 0)
    hb = h.astype(BF16)

    def proj(name):
        a, b = _SEG[name]
        return _dot(hb, w_ref[:, a:b])

    cm = cm_ref[...]
    sm = sm_ref[...]
    cg = cg_ref[...]
    sg = sg_ref[...]

    cqn = _rms(proj("cq"), gcq_ref[...]).astype(BF16)
    ckvn = _rms(proj("ckv"), gckv_ref[...]).astype(BF16)
    q = _dot(cqn, wuq_ref[...])
    qs = _dot(cqn, wuqs_ref[...])
    kn = _dot(ckvn, wuk_ref[...])
    kr = proj("kr") * cm + proj("krs") * sm
    tr = x.shape[0]
    vrow = lax.broadcasted_iota(jnp.int32, (V_ROWS - MLA_V, tr), 0)
    v_tail = jnp.where(vrow == 0, 1.0, 0.0).astype(BF16)
    for hd in range(MLA_HEADS):
        sl = slice(hd * MLA_HEAD_PAD, (hd + 1) * MLA_HEAD_PAD)
        qh = (q[:, sl] * cm + qs[:, sl] * sm) * (MLA_SCALE * LOG2E)
        qm_ref[hd] = qh.T.astype(BF16)
        km_ref[hd] = (kn[:, sl] + kr).astype(BF16)
    vmt = _dot(ckvn, wuv_ref[...]).T
    for hd in range(MLA_HEADS):
        vm_ref[hd, 0:MLA_V, :] = vmt[hd * MLA_V:(hd + 1) * MLA_V, :].astype(BF16)
        vm_ref[hd, MLA_V:V_ROWS, :] = v_tail

    ones = ones_ref[...]

    def head_rms_scale(v, width):
        ms = _dot_f32(v * v, ones[:width, :width]) * (1.0 / GQA_HD)
        return lax.rsqrt(ms + NORM_EPS)

    gq = proj("gq")
    gqs = proj("gqs")
    rq = head_rms_scale(gq, GQA_W)
    cg4 = jnp.concatenate([cg] * (GQA_W // LANES), axis=1)
    sg4 = jnp.concatenate([sg] * (GQA_W // LANES), axis=1)
    qg = rq * (gq * gq_ref[...] * cg4 + gqs * gqs_ref[...] * sg4)
    qgt = (qg * (GQA_SCALE * LOG2E)).T.astype(BF16)
    zeros_q = jnp.zeros((GQA_HD, tr), BF16)
    for hd in range(GQA_HEADS):
        kvh = hd // (GQA_HEADS // GQA_KV_HEADS)
        qh = qgt[hd * GQA_HD:(hd + 1) * GQA_HD, :]
        qg_ref[hd] = jnp.concatenate([qh, zeros_q] if kvh == 0 else [zeros_q, qh], axis=0)
    gk = proj("gk")
    gks = proj("gks")
    rk = head_rms_scale(gk, GQA_KV_HEADS * GQA_HD)
    kg_ref[0] = (rk * (gk * gk_ref[...] * cg + gks * gks_ref[...] * sg)).astype(BF16)
    vgt = proj("gv").T
    for kvh in range(GQA_KV_HEADS):
        vg_ref[kvh, 0:GQA_HD, :] = vgt[kvh * GQA_HD:(kvh + 1) * GQA_HD, :].astype(BF16)
        vg_ref[kvh, GQA_HD:V_ROWS, :] = v_tail

    u_ref[...] = proj("ssm").astype(BF16)
    gate_ref[...] = _sigmoid(proj("gate")).astype(BF16)


def _premix(rows, mod_l, g_pre, wts, tabs):
    r, d = rows.shape
    nt = r // ROW_TILE
    row_spec = lambda w: pl.BlockSpec((ROW_TILE, w), lambda i: (i, 0))
    full = lambda a: pl.BlockSpec(a.shape, lambda i: (0,) * a.ndim)
    consts = [g_pre, wts["w_cat"], wts["wuq"], wts["wuqs"], wts["wuk"], wts["wuv"],
              wts["g_cq"], wts["g_ckv"], wts["g_q"], wts["g_qs"], wts["g_k"], wts["g_ks"], wts["ones"]]
    t_spec = lambda nh, f: pl.BlockSpec((nh, f, ROW_TILE), lambda i: (0, 0, i))
    k_spec = lambda nh, f: pl.BlockSpec((nh, ROW_TILE, f), lambda i: (0, i, 0))
    outs = [(t_spec(MLA_HEADS, MLA_HEAD_PAD), (MLA_HEADS, MLA_HEAD_PAD, r)),
            (k_spec(MLA_HEADS, MLA_HEAD_PAD), (MLA_HEADS, r, MLA_HEAD_PAD)),
            (t_spec(MLA_HEADS, V_ROWS), (MLA_HEADS, V_ROWS, r)),
            (t_spec(GQA_HEADS, GQA_KV_HEADS * GQA_HD), (GQA_HEADS, GQA_KV_HEADS * GQA_HD, r)),
            (k_spec(1, GQA_KV_HEADS * GQA_HD), (1, r, GQA_KV_HEADS * GQA_HD)),
            (t_spec(GQA_KV_HEADS, V_ROWS), (GQA_KV_HEADS, V_ROWS, r)),
            (row_spec(SSM_W), (r, SSM_W)),
            (row_spec(3 * d), (r, 3 * d))]
    return pl.pallas_call(
        _premix_kernel,
        grid=(nt,),
        in_specs=[row_spec(d), full(mod_l)] + [full(a) for a in consts] + [row_spec(LANES)] * 4,
        out_specs=[s for s, _ in outs],
        out_shape=[jax.ShapeDtypeStruct(shp, BF16) for _, shp in outs],
        compiler_params=_cparams("parallel"),
        name="premix",
    )(rows, mod_l, *consts, tabs["cm"], tabs["sm"], tabs["cg"], tabs["sg"])


def _attn_kernel(q_ref, k_ref, v_ref, o_ref, s_ref, *, tk, nsub, n_ctx, n_iter, dv, hb, k_shared, v_shared,
                 ahead):
    n_items = hb * nsub
    tq = q_ref.shape[2]

    def scores(h, off, size):
        return _dot(k_ref[0 if k_shared else h, pl.ds(off, size), :], q_ref[h])

    def update(h, m, acc, s, off, size):
        m_new = jnp.maximum(m, jnp.max(s, axis=0, keepdims=True))
        p = jnp.exp2(s - m_new).astype(BF16)
        pv = _dot(v_ref[0 if v_shared else h, :, pl.ds(off, size)], p)
        return m_new, jnp.exp2(m - m_new) * acc + pv

    def issue(item, base):
        h, sub = item % hb, item // hb
        s_ref[item] = scores(h, pl.multiple_of(base + sub * tk, LANES), tk)

    m0 = jnp.full((1, tq), -1e30, F32)
    acc0 = jnp.zeros((V_ROWS, tq), F32)
    s_ctx = [scores(h, 0, n_ctx) for h in range(hb)]
    for item in range(ahead):
        issue(item, n_ctx)
    carry = tuple(update(h, m0, acc0, s_ctx[h], 0, n_ctx) for h in range(hb))

    def body(j, carry):
        carry = list(carry)
        base = n_ctx + j * (nsub * tk)
        base_next = n_ctx + jnp.minimum(j + 1, n_iter - 1) * (nsub * tk)
        for item in range(n_items):
            nxt = item + ahead
            if nxt < n_items:
                issue(nxt, base)
            else:
                issue(nxt - n_items, base_next)
            h, sub = item % hb, item // hb
            off = pl.multiple_of(base + sub * tk, LANES)
            carry[h] = update(h, carry[h][0], carry[h][1], s_ref[item], off, tk)
        return tuple(carry)

    carry = lax.fori_loop(0, jnp.where(pl.program_id(1) == 0, 0, n_iter), body, carry)
    for h in range(hb):
        acc = carry[h][1]
        o_ref[h] = acc[:dv] / acc[dv:dv + 1]


def _attention(qt, k, vt, *, hb, nsub, tq=256, tk=512, ahead=2):
    nh, dq, r = qt.shape
    k_shared = k.shape[0] != nh
    v_shared = vt.shape[0] != nh
    assert nh % hb == 0 and k.shape[0] in (nh, 1) and vt.shape[0] in (nh, nh // hb)
    dv = V_ROWS - 16
    assert (r - ROW_TILE) % (tk * nsub) == 0 and ahead < hb * nsub
    kern = functools.partial(_attn_kernel, tk=tk, nsub=nsub, n_ctx=ROW_TILE, n_iter=(r - ROW_TILE) // (tk * nsub),
                             dv=dv, hb=hb, k_shared=k_shared, v_shared=v_shared, ahead=ahead)
    k_spec = (pl.BlockSpec((1, r, dq), lambda g, i: (0, 0, 0)) if k_shared
              else pl.BlockSpec((hb, r, dq), lambda g, i: (g, 0, 0)))
    v_spec = pl.BlockSpec((1 if v_shared else hb, V_ROWS, r), lambda g, i: (g, 0, 0))
    return pl.pallas_call(
        kern,
        grid=(nh // hb, r // tq),
        in_specs=[pl.BlockSpec((hb, dq, tq), lambda g, i: (g, 0, i)), k_spec, v_spec],
        out_specs=pl.BlockSpec((hb, dv, tq), lambda g, i: (g, 0, i)),
        out_shape=jax.ShapeDtypeStruct((nh, dv, r), F32),
        scratch_shapes=[pltpu.VMEM((hb * nsub, tk, tq), F32)],
        compiler_params=_cparams("parallel", "arbitrary"),
        name="attention",
    )(qt, k, vt)


def _s5_kernel(u_ref, blk_ref, bp_ref, cpt_ref, ar_ref, ai_ref, y_ref, xr_ref, xi_ref, *, n_ctx_chunks):
    t, j, _ = u_ref.shape
    npair = t // 2
    ns = xr_ref.shape[1]
    row = lax.broadcasted_iota(jnp.int32, (j, ns), 0)

    def run(reverse):
        ucat = jnp.concatenate([u_ref[k] for k in range(t)], axis=1)
        xc = _dot(ucat, bp_ref[...])
        if reverse:
            shift = j - n_ctx_chunks - 1
            keep = row < j - 1
        else:
            shift = 1
            keep = row >= 1
        xr_ref[...] = jnp.where(keep, pltpu.roll(xc[:, :ns], shift, 0), 0.0)
        xi_ref[...] = jnp.where(keep, pltpu.roll(xc[:, ns:], shift, 0), 0.0)
        step = 1
        k = 0
        while step < j:
            ar = ar_ref[k:k + 1, :]
            ai = ai_ref[k:k + 1, :]
            xr = xr_ref[...]
            xi = xi_ref[...]
            if reverse:
                keep = row < j - step
                sr = jnp.where(keep, pltpu.roll(xr, j - step, 0), 0.0)
                si = jnp.where(keep, pltpu.roll(xi, j - step, 0), 0.0)
            else:
                keep = row >= step
                sr = jnp.where(keep, pltpu.roll(xr, step, 0), 0.0)
                si = jnp.where(keep, pltpu.roll(xi, step, 0), 0.0)
            xr_ref[...] = xr + ar * sr - ai * si
            xi_ref[...] = xi + ar * si + ai * sr
            step *= 2
            k += 1
        xin = jnp.concatenate([xr_ref[...], xi_ref[...]], axis=1)
        if reverse:
            xin = pltpu.roll(xin, n_ctx_chunks, 0)
        xin = xin.astype(BF16)
        for po in range(npair):
            y = _dot_nt(xin, cpt_ref[po * 256:(po + 1) * 256, :])
            pins = range(po, npair) if reverse else range(0, po + 1)
            for pi_ in pins:
                upair = jnp.concatenate([u_ref[2 * pi_], u_ref[2 * pi_ + 1]], axis=1)
                y += _dot(upair, blk_ref[abs(po - pi_)])
            if reverse:
                y_ref[2 * po] += y[:, :128]
                y_ref[2 * po + 1] += y[:, 128:]
            else:
                y_ref[2 * po] = y[:, :128]
                y_ref[2 * po + 1] = y[:, 128:]

    @pl.when(pl.program_id(1) == 0)
    def _():
        run(False)

    @pl.when(pl.program_id(1) == 1)
    def _():
        run(True)


def _s5_scan(ut, mats, n_ctx_chunks):
    t, j, w = ut.shape
    no = w // LANES
    ns = (LANES // SSM_GROUP) * SSM_STATE
    wspec = lambda *tail: pl.BlockSpec((None, None) + tail, lambda o, d: (d, o) + (0,) * len(tail))
    kern = functools.partial(_s5_kernel, n_ctx_chunks=n_ctx_chunks)
    return pl.pallas_call(
        kern,
        grid=(no, SSM_DIRS),
        in_specs=[pl.BlockSpec((t, j, LANES), lambda o, d: (0, 0, o)),
                  wspec(t // 2, 256, 256), wspec(t * LANES, 2 * ns), wspec(t * LANES, 2 * ns),
                  wspec(16, ns), wspec(16, ns)],
        out_specs=pl.BlockSpec((t, j, LANES), lambda o, d: (0, 0, o)),
        out_shape=jax.ShapeDtypeStruct((t, j, w), F32),
        scratch_shapes=[pltpu.VMEM((j, ns), F32)] * 2,
        compiler_params=_cparams("parallel", "arbitrary"),
        name="s5_scan",
    )(ut, mats["blk"], mats["bp"], mats["cpt"], mats["ar"], mats["ai"])


def _s5_matrices(a_re, a_im, log_dt, b_re, b_im, c_re, c_im, d_skip):
    t = SSM_CHUNK
    a_re = a_re.astype(F32)
    a_im = a_im.astype(F32)
    dt = jnp.exp(log_dt.astype(F32))[..., None]
    den = a_re * a_re + a_im * a_im

    def lpow(k):
        kf = jnp.asarray(k, F32)
        mag = jnp.exp(a_re * dt * kf)
        return mag * jnp.cos(a_im * dt * kf), mag * jnp.sin(a_im * dt * kf)

    lr, li = lpow(1.0)
    cr = ((lr - 1.0) * a_re + li * a_im) / den
    ci = (li * a_re - (lr - 1.0) * a_im) / den
    bbr = cr[..., None] * b_re - ci[..., None] * b_im
    bbi = cr[..., None] * b_im + ci[..., None] * b_re
    c_re = c_re.astype(F32)
    c_im = c_im.astype(F32)

    ks = jnp.arange(t + 1, dtype=F32)[:, None, None, None]
    pr, pi = lpow(ks)
    lbr = pr[:t, ..., None] * bbr - pi[:t, ..., None] * bbi
    lbi = pr[:t, ..., None] * bbi + pi[:t, ..., None] * bbr
    kern = (jnp.einsum('dgop,kdgpc->dgkoc', c_re, lbr, precision='highest')
            - jnp.einsum('dgop,kdgpc->dgkoc', c_im, lbi, precision='highest'))
    skip = d_skip.astype(F32).reshape(SSM_GROUPS, SSM_GROUP)
    eye = jnp.eye(SSM_GROUP, dtype=F32)
    kern = kern.at[0, :, 0].add(skip[:, :, None] * eye)
    gl = LANES // SSM_GROUP
    no = SSM_GROUPS // gl
    def widen(compact, col_of, n_cols, row_group, col_group):
        expand = np.zeros((compact.shape[-1], n_cols), np.float32)
        for h in range(gl):
            src_cols = np.arange(compact.shape[-1])
            expand[src_cols, col_of(src_cols, h)] = 1.0
        wide = jnp.einsum('...rj,jk->...rk', compact.astype(BF16), jnp.asarray(expand, BF16),
                          preferred_element_type=BF16)
        rows_g = row_group(lax.broadcasted_iota(jnp.int32, wide.shape[-2:], 0))
        cols_g = col_group(lax.broadcasted_iota(jnp.int32, wide.shape[-2:], 1))
        return jnp.where(rows_g == cols_g, wide, jnp.zeros((), BF16))

    dd = jnp.arange(t // 2)[:, None, None]
    ti = jnp.arange(2)[None, :, None]
    to = jnp.arange(2)[None, None, :]
    lag = jnp.stack([2 * dd + to - ti, 2 * dd + ti - to])
    pick = jax.vmap(lambda kd, ld: jnp.take(kd, jnp.clip(ld, 0, t - 1).reshape(-1), axis=1))(kern, lag)
    pick = pick.reshape(SSM_DIRS, no, gl, t // 2, 2, 2, SSM_GROUP, SSM_GROUP)
    pick = jnp.where((lag >= 0)[:, None, None, :, :, :, None, None], pick, 0.0)
    kc = jnp.transpose(pick, (0, 1, 3, 4, 2, 7, 5, 6)).reshape(SSM_DIRS, no, t // 2, 256, 2 * SSM_GROUP)
    blk = widen(kc, lambda j, h: (j // SSM_GROUP) * LANES + h * SSM_GROUP + j % SSM_GROUP, 256,
                lambda r: (r // SSM_GROUP) % gl, lambda c: (c // SSM_GROUP) % gl)
    def contrib(lb):
        both = jnp.stack([lb[::-1, 0], lb[:, 1]])
        both = both.reshape(SSM_DIRS, t, no, gl, SSM_STATE, SSM_GROUP)
        return jnp.transpose(both, (0, 2, 1, 3, 5, 4)).reshape(SSM_DIRS, no, t * LANES, SSM_STATE)
    bp = widen(jnp.concatenate([contrib(lbr), contrib(lbi)], axis=-1),
               lambda j, h: (j // SSM_STATE) * (gl * SSM_STATE) + h * SSM_STATE + j % SSM_STATE, 2 * gl * SSM_STATE,
               lambda r: (r // SSM_GROUP) % gl, lambda c: (c // SSM_STATE) % gl)
    p1r = pr[1:, :, :, None, :]
    p1i = pi[1:, :, :, None, :]
    rd_r = c_re * p1r - c_im * p1i
    rd_i = -(c_re * p1i + c_im * p1r)
    def readout(rd):
        both = jnp.stack([rd[:, 0], rd[::-1, 1]])
        both = both.reshape(SSM_DIRS, t, no, gl, SSM_GROUP, SSM_STATE)
        return jnp.transpose(both, (0, 2, 1, 3, 4, 5)).reshape(SSM_DIRS, no, t * LANES, SSM_STATE)
    cpt = widen(jnp.concatenate([readout(rd_r), readout(rd_i)], axis=-1),
                lambda j, h: (j // SSM_STATE) * (gl * SSM_STATE) + h * SSM_STATE + j % SSM_STATE, 2 * gl * SSM_STATE,
                lambda r: (r // SSM_GROUP) % gl, lambda c: (c // SSM_STATE) % gl)
    kk = (t * 2.0 ** jnp.minimum(jnp.arange(16), 10)).astype(F32)[:, None, None, None]
    qr, qi = lpow(kk)
    lanes = lambda q: jnp.transpose(q, (1, 2, 0, 3)).reshape(SSM_DIRS, no, gl, 16, SSM_STATE)
    ar = jnp.transpose(lanes(qr), (0, 1, 3, 2, 4)).reshape(SSM_DIRS, no, 16, gl * SSM_STATE)
    ai = jnp.transpose(lanes(qi), (0, 1, 3, 2, 4)).reshape(SSM_DIRS, no, 16, gl * SSM_STATE)
    return dict(blk=blk, bp=bp, cpt=cpt, ar=ar, ai=ai)


def _merge_kernel(x_ref, mod_ref, om_ref, og_ref, y_ref, gate_ref,
                  wglu_ref, bglu_ref, wbm_ref, wbs_ref, wbg_ref, wout_ref,
                  gpost_ref, gffn_ref, wrt_hi_ref, wrt_lo_ref, brt_ref,
                  xo_ref, h_ref, bk_ref):
    d = x_ref.shape[1]
    y = y_ref[...]
    y1 = 0.5 * y * (1.0 + jnp.tanh(math.sqrt(2.0 / math.pi) * (y + 0.044715 * (y * y * y))))
    ssm = y1 * _sigmoid(_dot(y1.astype(BF16), wglu_ref[...]) + bglu_ref[...])
    gate = gate_ref[...]
    o_mla = om_ref[...].reshape(-1, om_ref.shape[2]).T.astype(BF16)
    o_gqa = og_ref[...].reshape(-1, og_ref.shape[2]).T.astype(BF16)
    m = (gate[:, 0:d].astype(F32) * _dot(o_mla, wbm_ref[...])
         + gate[:, d:2 * d].astype(F32) * _dot(ssm.astype(BF16), wbs_ref[...])
         + gate[:, 2 * d:3 * d].astype(F32) * _dot(o_gqa, wbg_ref[...]))
    ymix = _dot(m.astype(BF16), wout_ref[...])
    x = x_ref[...] + _mod_row(mod_ref, 2) * _rms(ymix, gpost_ref[...])
    xo_ref[...] = x
    h = _rms(x, gffn_ref[...]) * (1.0 + _mod_row(mod_ref, 4)) + _mod_row(mod_ref, 3)
    h_ref[:, 0:d] = h

    h_hi = h.astype(BF16)
    h_lo = (h - h_hi.astype(F32)).astype(BF16)
    lg = (_dot(h_hi, wrt_hi_ref[...]) + _dot(h_lo, wrt_hi_ref[...]) + _dot(h_hi, wrt_lo_ref[...])
          + brt_ref[...])
    lane = lax.broadcasted_iota(jnp.int32, lg.shape, 1)
    neg = jnp.float32(-1e30)
    is_g = lane < N_GROUPS
    gl = jnp.where(is_g, lg, neg)
    gmax = jnp.max(gl, axis=-1, keepdims=True)
    gsel = jnp.min(jnp.where(is_g & (gl == gmax), lane, LANES), axis=-1, keepdims=True)
    pg = 1.0 / jnp.sum(jnp.where(is_g, jnp.exp(gl - gmax), 0.0), axis=-1, keepdims=True)
    lo = N_GROUPS + gsel * EXPERTS_PER_GROUP
    in_grp = (lane >= lo) & (lane < lo + EXPERTS_PER_GROUP)
    el = jnp.where(in_grp, lg, neg)
    m1 = jnp.max(el, axis=-1, keepdims=True)
    i1 = jnp.min(jnp.where(in_grp & (el == m1), lane, LANES), axis=-1, keepdims=True)
    el2 = jnp.where(lane == i1, neg, el)
    m2 = jnp.max(el2, axis=-1, keepdims=True)
    i2 = jnp.min(jnp.where(in_grp & (el2 == m2), lane, LANES), axis=-1, keepdims=True)
    e2 = jnp.exp(m2 - m1)
    t1 = pg / (1.0 + e2)
    first = i1 < i2
    ea = jnp.minimum(i1, i2) - lo
    eb = jnp.maximum(i1, i2) - lo
    bucket = gsel * PAIRS_PER_GROUP + jnp.right_shift(ea * (7 - ea), 1) + (eb - ea - 1)
    w_a = jnp.where(first, t1, t1 * e2)
    w_b = jnp.where(first, t1 * e2, t1)
    route = (jnp.where(lane == 0, bucket.astype(F32), 0.0) + jnp.where(lane == 1, w_a, 0.0)
             + jnp.where(lane == 2, w_b, 0.0))
    h_ref[:, d:d + LANES] = route
    bk_ref[...] = route


def _merge(rows, mod_l, o_mla, o_gqa, y_ssm, gates, wts):
    r, d = rows.shape
    nt = r // ROW_TILE
    row_spec = lambda w: pl.BlockSpec((ROW_TILE, w), lambda i: (i, 0))
    full = lambda a: pl.BlockSpec(a.shape, lambda i: (0,) * a.ndim)
    consts = [wts["w_glu"], wts["b_glu"], wts["w_br_mla"], wts["w_br_ssm"], wts["w_br_gqa"], wts["w_out"],
              wts["g_post_mix"], wts["g_pre_ffn"], wts["w_rt_hi"], wts["w_rt_lo"], wts["b_rt"]]
    return pl.pallas_call(
        _merge_kernel,
        grid=(nt,),
        in_specs=[row_spec(d), full(mod_l),
                  pl.BlockSpec(o_mla.shape[:2] + (ROW_TILE,), lambda i: (0, 0, i)),
                  pl.BlockSpec(o_gqa.shape[:2] + (ROW_TILE,), lambda i: (0, 0, i)),
                  row_spec(SSM_W), row_spec(3 * d)] + [full(a) for a in consts],
        out_specs=[row_spec(d), row_spec(d + LANES), row_spec(LANES)],
        out_shape=[jax.ShapeDtypeStruct((r, d), F32), jax.ShapeDtypeStruct((r, d + LANES), F32),
                   jax.ShapeDtypeStruct((r, LANES), F32)],
        compiler_params=_cparams("parallel"),
        name="merge",
    )(rows, mod_l, o_mla, o_gqa, y_ssm, gates, *consts)


def _row_copy(idx, r, src_hbm, buf, sem, slot):
    return pltpu.make_async_copy(src_hbm.at[pl.ds(idx, 1)], buf.at[slot, pl.ds(r, 1)], sem.at[slot])


def _gathered_tile(idx_ref, src_hbm, buf, sem):
    i = pl.program_id(0)
    last = pl.num_programs(0) - 1
    depth, n = buf.shape[0], buf.shape[1]
    ahead = depth - 1

    def start(step, dst_slot):
        for r in range(n):
            _row_copy(idx_ref[jnp.minimum(step, last) * n + r], r, src_hbm, buf, sem, dst_slot).start(priority=r % 2)

    def wait(dst_slot):
        for r in range(n):
            _row_copy(0, r, src_hbm, buf, sem, dst_slot).wait()

    @pl.when(i == 0)
    def _():
        for k in range(ahead):
            start(k, k)

    slot = i % depth
    wait(slot)
    tile = buf[slot]

    def finish():
        start(i + ahead, (i + ahead) % depth)

        @pl.when(i == last)
        def _():
            for k in range(1, depth):
                wait((i + k) % depth)

    return tile, finish


def _moe_kernel(src_ref, ea_ref, eb_ref, tok_hbm, wga_ref, wua_ref, wda_ref, wgb_ref, wub_ref, wdb_ref,
                gpost_ref, o_ref, buf, sem):
    tok, finish = _gathered_tile(src_ref, tok_hbm, buf, sem)
    d = o_ref.shape[1]
    h = tok[:, 0:d].astype(BF16)
    rt = tok[:, d:d + LANES]
    lane = lax.broadcasted_iota(jnp.int32, rt.shape, 1)
    w_a = jnp.sum(jnp.where(lane == 1, rt, 0.0), axis=-1, keepdims=True)
    w_b = jnp.sum(jnp.where(lane == 2, rt, 0.0), axis=-1, keepdims=True)

    def expert(wg_ref, wu_ref, wd_ref):
        a = _dot(h, wg_ref[...])
        hid = (a * _sigmoid(a)) * _dot(h, wu_ref[...])
        return _dot(hid.astype(BF16), wd_ref[...])

    y = w_a * expert(wga_ref, wua_ref, wda_ref) + w_b * expert(wgb_ref, wub_ref, wdb_ref)
    o_ref[...] = _rms(y, gpost_ref[...]).astype(o_ref.dtype)
    finish()


def _moe(tokens, src, tile_ea, tile_eb, wg, wu, wd, g_post):
    w = tokens.shape[1]
    d = w - LANES
    ne, _, de = wg.shape
    nt = src.shape[0] // GATHER_TILE
    w_in = lambda sel: pl.BlockSpec((None, d, de), lambda i, src, ea, eb: ((ea, eb)[sel][i], 0, 0))
    w_out = lambda sel: pl.BlockSpec((None, de, d), lambda i, src, ea, eb: ((ea, eb)[sel][i], 0, 0))
    return pl.pallas_call(
        _moe_kernel,
        grid_spec=pltpu.PrefetchScalarGridSpec(
            num_scalar_prefetch=3,
            grid=(nt,),
            in_specs=[
                pl.BlockSpec(memory_space=pl.ANY),
                w_in(0), w_in(0), w_out(0), w_in(1), w_in(1), w_out(1),
                pl.BlockSpec(g_post.shape, lambda i, src, ea, eb: (0, 0)),
            ],
            out_specs=pl.BlockSpec((GATHER_TILE, d), lambda i, src, ea, eb: (i, 0)),
            scratch_shapes=[pltpu.VMEM((GATHER_DEPTH, GATHER_TILE, w), F32),
                            pltpu.SemaphoreType.DMA((GATHER_DEPTH,))],
        ),
        out_shape=jax.ShapeDtypeStruct((src.shape[0], d), F32),
        compiler_params=_cparams("arbitrary"),
        name="moe",
    )(src, tile_ea, tile_eb, tokens, wg, wu, wd, wg, wu, wd, g_post)


def _dispatch(bucket, n_rows):
    nb = N_GROUPS * PAIRS_PER_GROUP
    n_tiles = n_rows // MOE_TILE + nb
    onehot = (bucket[:, None] == jnp.arange(nb, dtype=jnp.int32)[None, :]).astype(jnp.int32)
    counts = jnp.sum(onehot, axis=0)
    rank = jnp.sum((jnp.cumsum(onehot, axis=0) - onehot) * onehot, axis=1)
    tiles_per = (counts + MOE_TILE - 1) // MOE_TILE
    tile_end = jnp.cumsum(tiles_per)
    start = (tile_end - tiles_per) * MOE_TILE
    pos = jnp.take(start, bucket) + rank
    src = jnp.zeros((n_tiles * MOE_TILE,), jnp.int32).at[pos].set(jnp.arange(n_rows, dtype=jnp.int32))
    tile_id = jnp.arange(n_tiles, dtype=jnp.int32)
    tile_bucket = jnp.minimum(jnp.searchsorted(tile_end, tile_id, side='right'), nb - 1).astype(jnp.int32)
    tile_used = (tile_id < tile_end[-1]).astype(jnp.int32)
    last_bucket = jnp.take(tile_bucket, jnp.maximum(tile_end[-1] - 1, 0))
    tile_bucket = jnp.where(tile_used == 1, tile_bucket, last_bucket)
    pair_a = jnp.asarray([0, 0, 0, 1, 1, 2], jnp.int32)
    pair_b = jnp.asarray([1, 2, 3, 2, 3, 3], jnp.int32)
    grp = tile_bucket // PAIRS_PER_GROUP
    tile_ea = grp * EXPERTS_PER_GROUP + jnp.take(pair_a, tile_bucket % PAIRS_PER_GROUP)
    tile_eb = grp * EXPERTS_PER_GROUP + jnp.take(pair_b, tile_bucket % PAIRS_PER_GROUP)
    return pos, src, tile_ea, tile_eb


def _residual_kernel(pos_ref, x_ref, z_hbm, mod_ref, o_ref, buf, sem):
    z, finish = _gathered_tile(pos_ref, z_hbm, buf, sem)
    o_ref[...] = x_ref[...] + _mod_row(mod_ref, 5) * z
    finish()


def _residual(rows, z_sorted, pos, mod_l, drop_context):
    r, d = rows.shape
    row_spec = pl.BlockSpec((GATHER_TILE, d), lambda i, pos: (i, 0))
    out_rows = r - GATHER_TILE if drop_context else r
    out_spec = pl.BlockSpec((GATHER_TILE, d), lambda i, pos: (jnp.maximum(i - 1, 0), 0)) if drop_context else row_spec
    return pl.pallas_call(
        _residual_kernel,
        grid_spec=pltpu.PrefetchScalarGridSpec(
            num_scalar_prefetch=1,
            grid=(r // GATHER_TILE,),
            in_specs=[row_spec, pl.BlockSpec(memory_space=pl.ANY),
                      pl.BlockSpec(mod_l.shape, lambda i, pos: (0, 0))],
            out_specs=out_spec,
            scratch_shapes=[pltpu.VMEM((GATHER_DEPTH, GATHER_TILE, d), F32),
                            pltpu.SemaphoreType.DMA((GATHER_DEPTH,))],
        ),
        out_shape=jax.ShapeDtypeStruct((out_rows, d), F32),
        compiler_params=_cparams("arbitrary"),
        name="residual",
    )(pos, rows, z_sorted, mod_l)


def _rope_tables(seq, n_ctx):
    n_rows = seq // GRID_W

    def pattern(width):
        half = width // 4
        freqs = (np.float32(ROPE_THETA) ** (-np.arange(half, dtype=np.float32) / np.float32(half))).astype(np.float32)
        ar = np.arange(n_rows, dtype=np.float32)[:, None] * freqs[None, :]
        ac = np.arange(GRID_W, dtype=np.float32)[:, None] * freqs[None, :]
        by_row = lambda tab: jnp.broadcast_to(jnp.asarray(tab, F32)[:, None, :], (n_rows, GRID_W, half)).reshape(seq, half)
        by_col = lambda tab: jnp.broadcast_to(jnp.asarray(tab, F32)[None, :, :], (n_rows, GRID_W, half)).reshape(seq, half)
        cr, sr, cc, sc = by_row(np.cos(ar)), by_row(np.sin(ar)), by_col(np.cos(ac)), by_col(np.sin(ac))
        c = jnp.concatenate([cr, cr, cc, cc], axis=1)
        s = jnp.concatenate([-sr, sr, -sc, sc], axis=1)
        return c, s

    c32, s32 = pattern(MLA_ROPE)
    c64, s64 = pattern(GQA_HD)
    one = jnp.ones((seq, 1), F32)
    cm = jnp.concatenate([one * jnp.ones((1, MLA_NOPE)), c32, one * jnp.ones((1, 32))], axis=1)
    sm = jnp.concatenate([jnp.zeros((seq, MLA_NOPE)), s32, jnp.zeros((seq, 32))], axis=1)
    cg = jnp.concatenate([c64, c64], axis=1)
    sg = jnp.concatenate([s64, s64], axis=1)
    ctx_c = jnp.ones((n_ctx, LANES), F32)
    ctx_s = jnp.zeros((n_ctx, LANES), F32)
    cat = lambda a, b: jnp.concatenate([a, b], axis=0)
    return dict(cm=cat(ctx_c, cm), sm=cat(ctx_s, sm), cg=cat(ctx_c, cg), sg=cat(ctx_s, sg))


def _swap_perm(width):
    q = width // 4
    return np.concatenate([np.arange(q, 2 * q), np.arange(0, q), np.arange(3 * q, 4 * q), np.arange(2 * q, 3 * q)])


def _layer_weights(l, p):
    d = p["w_in"].shape[1]
    w_in = p["w_in"][l]
    o = 0
    cq = w_in[:, o:o + MLA_Q_RANK]; o += MLA_Q_RANK
    ckv = w_in[:, o:o + MLA_KV_RANK]; o += MLA_KV_RANK
    kr = w_in[:, o:o + MLA_ROPE]; o += MLA_ROPE
    ssm = w_in[:, o:o + SSM_W]; o += SSM_W
    gq = w_in[:, o:o + GQA_W]; o += GQA_W
    gk = w_in[:, o:o + GQA_KV_HEADS * GQA_HD]; o += GQA_KV_HEADS * GQA_HD
    gv = w_in[:, o:o + GQA_KV_HEADS * GQA_HD]; o += GQA_KV_HEADS * GQA_HD
    gate = w_in[:, o:]
    p32 = _swap_perm(MLA_ROPE)
    p64 = _swap_perm(GQA_HD)
    perm_q = np.concatenate([p64 + GQA_HD * h for h in range(GQA_HEADS)])
    perm_k = np.concatenate([p64 + GQA_HD * h for h in range(GQA_KV_HEADS)])
    z = lambda n: jnp.zeros((d, n), F32)
    kr128 = jnp.concatenate([z(MLA_NOPE), kr, z(32)], axis=1)
    krs128 = jnp.concatenate([z(MLA_NOPE), kr[:, p32], z(32)], axis=1)
    w_cat = jnp.concatenate([cq, ckv, kr128, krs128, ssm, gq, gq[:, perm_q], gk, gk[:, perm_k], gv, gate],
                            axis=1).astype(BF16)

    w_uq = p["w_uq"][l].reshape(MLA_Q_RANK, MLA_HEADS, MLA_NOPE + MLA_ROPE)
    zq = jnp.zeros((MLA_Q_RANK, MLA_HEADS, 32), F32)
    wuq = jnp.concatenate([w_uq, zq], axis=2).reshape(MLA_Q_RANK, -1)
    wuqs = jnp.concatenate([jnp.zeros((MLA_Q_RANK, MLA_HEADS, MLA_NOPE), F32),
                            w_uq[:, :, MLA_NOPE:][:, :, p32], zq], axis=2).reshape(MLA_Q_RANK, -1)
    w_ukv = p["w_ukv"][l].reshape(MLA_KV_RANK, MLA_HEADS, MLA_NOPE + MLA_V)
    wuk = jnp.concatenate([w_ukv[:, :, :MLA_NOPE], jnp.zeros((MLA_KV_RANK, MLA_HEADS, 64), F32)],
                          axis=2).reshape(MLA_KV_RANK, -1)
    wuv = w_ukv[:, :, MLA_NOPE:].reshape(MLA_KV_RANK, -1)

    g_qn = p["g_qn"][l]
    g_kn = p["g_kn"][l]
    blk = np.arange(GQA_W) // GQA_HD
    ones = jnp.asarray((blk[:, None] == blk[None, :]).astype(np.float32), BF16)

    w_rt = jnp.zeros((d, LANES), F32)
    w_rt = w_rt.at[:, :N_GROUPS].set(p["w_group"][l]).at[:, N_GROUPS:N_GROUPS + N_EXPERTS].set(p["w_router"][l])
    w_rt_hi = w_rt.astype(BF16)
    w_rt_lo = (w_rt - w_rt_hi.astype(F32)).astype(BF16)
    b_rt = jnp.zeros((1, LANES), F32)
    b_rt = b_rt.at[0, :N_GROUPS].set(p["b_group"][l]).at[0, N_GROUPS:N_GROUPS + N_EXPERTS].set(p["b_router"][l])

    row = lambda v: v.reshape(1, -1).astype(F32)
    return dict(
        w_cat=w_cat, wuq=wuq.astype(BF16), wuqs=wuqs.astype(BF16), wuk=wuk.astype(BF16), wuv=wuv.astype(BF16),
        g_cq=row(p["g_cq"][l]), g_ckv=row(p["g_ckv"][l]),
        g_q=row(jnp.tile(g_qn, GQA_HEADS)), g_qs=row(jnp.tile(g_qn[p64], GQA_HEADS)),
        g_k=row(jnp.tile(g_kn, GQA_KV_HEADS)), g_ks=row(jnp.tile(g_kn[p64], GQA_KV_HEADS)),
        ones=ones,
        w_glu=p["w_glu"][l].astype(BF16), b_glu=row(p["b_glu"][l]),
        w_br_mla=p["w_br_mla"][l].astype(BF16), w_br_ssm=p["w_br_ssm"][l].astype(BF16),
        w_br_gqa=p["w_br_gqa"][l].astype(BF16), w_out=p["w_out"][l].astype(BF16),
        g_post_mix=row(p["g_post_mix"][l]), g_pre_ffn=row(p["g_pre_ffn"][l]),
        w_rt_hi=w_rt_hi, w_rt_lo=w_rt_lo, b_rt=b_rt,
    )


def kernel(x, c, ctx, c_ctx, w_mod, b_mod, g_pre_mix, g_post_mix, g_pre_ffn, g_post_ffn, w_in, g_cq, g_ckv, w_uq, w_ukv, g_qn, g_kn, ssm_a_re, ssm_a_im, ssm_log_dt, ssm_b_re, ssm_b_im, ssm_c_re, ssm_c_im, ssm_d, w_glu, b_glu, w_br_mla, w_br_ssm, w_br_gqa, w_out, w_group, b_group, w_router, b_router, w_exp_gate, w_exp_up, w_exp_down):
    assert x.shape[0] == 1 and ctx.shape[0] == 1
    seq, d = x.shape[1], x.shape[2]
    n_ctx = ctx.shape[1]
    assert n_ctx == ROW_TILE and seq % (SSM_CHUNK * SSM_SUPER) == 0
    depth = w_in.shape[0]
    params = dict(w_in=w_in, g_cq=g_cq, g_ckv=g_ckv, w_uq=w_uq, w_ukv=w_ukv, g_qn=g_qn, g_kn=g_kn,
                  w_glu=w_glu, b_glu=b_glu, w_br_mla=w_br_mla, w_br_ssm=w_br_ssm, w_br_gqa=w_br_gqa,
                  w_out=w_out, g_post_mix=g_post_mix, g_pre_ffn=g_pre_ffn,
                  w_group=w_group, b_group=b_group, w_router=w_router, b_router=b_router)

    cond8 = jnp.zeros((8, d), F32).at[0].set(c[0]).at[1].set(c_ctx)
    mods = _modulation(cond8, w_mod, b_mod)
    tabs = _rope_tables(seq, n_ctx)
    rows = jnp.concatenate([ctx[0], x[0]], axis=0)
    r = rows.shape[0]

    for l in range(depth):
        wts = _layer_weights(l, params)
        mod_l = mods[l]
        qm, km, vm, qg, kg, vg, u, gates = _premix(rows, mod_l, g_pre_mix[l].reshape(1, d), wts, tabs)

        o_mla = _attention(qm, km, vm, hb=2, nsub=16, ahead=4)
        o_gqa = _attention(qg, kg, vg, hb=GQA_HEADS // GQA_KV_HEADS, nsub=8, ahead=4)

        mats = _s5_matrices(ssm_a_re[l], ssm_a_im[l], ssm_log_dt[l], ssm_b_re[l], ssm_b_im[l],
                            ssm_c_re[l], ssm_c_im[l], ssm_d[l])
        ut = jnp.transpose(u.reshape(r // SSM_CHUNK, SSM_CHUNK, SSM_W), (1, 0, 2))
        yt = _s5_scan(ut, mats, n_ctx // SSM_CHUNK)
        y_ssm = jnp.transpose(yt, (1, 0, 2)).reshape(r, SSM_W)

        rows, tokens, route = _merge(rows, mod_l, o_mla, o_gqa, y_ssm, gates, wts)
        pos, src, tile_ea, tile_eb = _dispatch(route[:, 0].astype(jnp.int32), r)
        z_sorted = _moe(tokens, src, tile_ea, tile_eb,
                        w_exp_gate[l].astype(BF16), w_exp_up[l].astype(BF16), w_exp_down[l].astype(BF16),
                        g_post_ffn[l].reshape(1, d))
        rows = _residual(rows, z_sorted, pos, mod_l, drop_context=l == depth - 1)

    return rows.reshape(1, seq, d)
```
